```python
import math
import jax, jax.numpy as jnp
from jax import lax
import numpy as np

D_MODEL = 1024
BATCH = 8
SEQ = 4096
DEPTH = 1

GRID_W = 64
CTX_LEN = 256
HEAD_DIM = 64
D_MIX = D_MODEL
NA_HEADS = (D_MIX // 2) // HEAD_DIM
NA_WIDTH = NA_HEADS * HEAD_DIM
NA_WIN_R = 8
NA_WIN_C = 16
DIFF_QK = HEAD_DIM
DIFF_V = 2 * DIFF_QK
DIFF_HEADS = (D_MIX - NA_WIDTH) // DIFF_V
DIFF_WIDTH = DIFF_HEADS * DIFF_V
DIFF_QK_WIDTH = 2 * DIFF_HEADS * DIFF_QK
QKV_WIDTH = 3 * NA_WIDTH + 2 * DIFF_QK_WIDTH + DIFF_WIDTH
Q_BLOCK = 128
ROPE_THETA = 10000.0
PEER_HEADS = 8
PEER_NKEYS = 128
PEER_EXPERTS = PEER_NKEYS * PEER_NKEYS
PEER_QDIM = 256
PEER_TOPK = 16
TOKEN_BLOCK = 128
N_MOD = 6
EPS = 1e-6

kernel_name = "hybrid_natten_diffattn_peer_block"


def rmsnorm(x, g):
    xf = x.astype(jnp.float32)
    y = xf * lax.rsqrt(jnp.mean(xf * xf, axis=-1, keepdims=True) + EPS)
    return (y * g.astype(jnp.float32)).astype(x.dtype)


def modulate(h, shift, scale):
    return h * (1.0 + scale) + shift


def axial_rope_tables(n_tokens, dim):
    t = jnp.arange(n_tokens)
    row = (t // GRID_W).astype(jnp.float32)
    col = (t % GRID_W).astype(jnp.float32)
    n_freq = dim // 4
    inv = ROPE_THETA ** (-jnp.arange(n_freq, dtype=jnp.float32) / n_freq)
    ang = jnp.concatenate([row[:, None] * inv, col[:, None] * inv], axis=-1)
    return jnp.cos(ang), jnp.sin(ang)


def apply_rope(x, cos, sin):
    xp = x.reshape(x.shape[:-1] + (x.shape[-1] // 2, 2))
    x0, x1 = xp[..., 0], xp[..., 1]
    cos = cos.astype(x.dtype)
    sin = sin.astype(x.dtype)
    return jnp.stack([x0 * cos - x1 * sin, x0 * sin + x1 * cos], axis=-1).reshape(x.shape)


def split_projection(p):
    B, L, _ = p.shape
    cuts = [int(i) for i in np.cumsum([NA_WIDTH, NA_WIDTH, NA_WIDTH, DIFF_QK_WIDTH, DIFF_QK_WIDTH])]
    qa, ka, va, qd, kd, vd = jnp.split(p, cuts, axis=-1)
    heads_a = lambda t: t.reshape(B, L, NA_HEADS, HEAD_DIM).transpose(0, 2, 1, 3)
    heads_qk = lambda t: t.reshape(B, L, 2, DIFF_HEADS, DIFF_QK).transpose(0, 2, 3, 1, 4)
    heads_v = lambda t: t.reshape(B, L, DIFF_HEADS, DIFF_V).transpose(0, 2, 1, 3)
    return heads_a(qa), heads_a(ka), heads_a(va), heads_qk(qd), heads_qk(kd), heads_v(vd)


def merge_heads(out_a, out_d, w_out):
    B, _, L, _ = out_a.shape
    ya = out_a.transpose(0, 2, 1, 3).reshape(B, L, NA_WIDTH)
    yd = out_d.transpose(0, 2, 1, 3).reshape(B, L, DIFF_WIDTH)
    return jnp.concatenate([ya, yd], axis=-1) @ w_out


def neighbourhood_attention(q, k, v, k_ctx, v_ctx, rel_bias):
    B, H, T, d = q.shape
    C = k_ctx.shape[2]
    rows = T // GRID_W
    kr = min(NA_WIN_R, rows)
    kc = NA_WIN_C
    qg = q.reshape(B, H, rows, GRID_W, d)
    kg = k.reshape(B, H, rows, GRID_W, d)
    vg = v.reshape(B, H, rows, GRID_W, d)
    cols = jnp.arange(GRID_W)
    c_start = jnp.clip(cols - kc // 2, 0, GRID_W - kc)
    c_idx = c_start[:, None] + jnp.arange(kc)[None, :]
    c_off = c_idx - cols[:, None] + (NA_WIN_C - 1)
    scale = d ** -0.5

    def row_block(r):
        r_start = jnp.clip(r - kr // 2, 0, rows - kr)
        q_r = lax.dynamic_index_in_dim(qg, r, axis=2, keepdims=False)
        k_r = lax.dynamic_slice_in_dim(kg, r_start, kr, axis=2)
        v_r = lax.dynamic_slice_in_dim(vg, r_start, kr, axis=2)
        k_nb = k_r[:, :, :, c_idx]
        v_nb = v_r[:, :, :, c_idx]
        r_off = r_start + jnp.arange(kr) - r + (NA_WIN_R - 1)
        bias = rel_bias[:, r_off][:, :, c_off].transpose(0, 2, 1, 3)
        s_loc = jnp.einsum('bhwd,bhawkd->bhwak', q_r, k_nb) * scale + bias[None]
        s_ctx = jnp.einsum('bhwd,bhcd->bhwc', q_r, k_ctx) * scale
        s = jnp.concatenate([s_ctx, s_loc.reshape(B, H, GRID_W, kr * kc)], axis=-1)
        p = jax.nn.softmax(s.astype(jnp.float32), axis=-1).astype(v.dtype)
        p_ctx = p[..., :C]
        p_loc = p[..., C:].reshape(B, H, GRID_W, kr, kc)
        return (jnp.einsum('bhwc,bhcd->bhwd', p_ctx, v_ctx)
                + jnp.einsum('bhwak,bhawkd->bhwd', p_loc, v_nb))

    out = lax.map(row_block, jnp.arange(rows))
    return out.transpose(1, 2, 0, 3, 4).reshape(B, H, T, d)


def dense_attention(q, k, v):
    s = jnp.einsum('bhqd,bhkd->bhqk', q, k) * (q.shape[-1] ** -0.5)
    p = jax.nn.softmax(s.astype(jnp.float32), axis=-1).astype(v.dtype)
    return jnp.einsum('bhqk,bhkd->bhqd', p, v)


def diff_lambda_value(lam_params, lambda_init):
    lp = lam_params.astype(jnp.float32)
    return jnp.exp(jnp.sum(lp[0] * lp[1])) - jnp.exp(jnp.sum(lp[2] * lp[3])) + lambda_init


def diff_attention_block(q, k, v, lam):
    s = jnp.einsum('bmhqd,bmhkd->bmhqk', q, k).astype(jnp.float32) * (q.shape[-1] ** -0.5)
    p = jax.nn.softmax(s, axis=-1)
    a = p[:, 0] - lam * p[:, 1]
    return jnp.einsum('bhqk,bhkd->bhqd', a.astype(v.dtype), v)


def diff_attention_latent(q, k, v, k_ctx, v_ctx, lam):
    B, M, H, T, d = q.shape
    k_all = jnp.concatenate([k_ctx, k], axis=3)
    v_all = jnp.concatenate([v_ctx, v], axis=2)
    nb = T // Q_BLOCK
    qb = q.reshape(B, M, H, nb, Q_BLOCK, d).transpose(3, 0, 1, 2, 4, 5)
    out = lax.map(lambda qq: diff_attention_block(qq, k_all, v_all, lam), qb)
    return out.transpose(1, 2, 0, 3, 4).reshape(B, H, T, v.shape[-1])


def peer_ffn(h, w_q, keys, u, v):
    shp = h.shape
    hb = h.reshape(-1, TOKEN_BLOCK, shp[-1])

    def block(xb):
        tb = xb.shape[0]
        q = (xb @ w_q).reshape(tb, PEER_HEADS, 2, PEER_QDIM // 2)
        s = jnp.einsum('thpd,hpnd->thpn', q, keys).astype(jnp.float32)
        s_top, i_top = lax.top_k(s, PEER_TOPK)
        cand = s_top[:, :, 0, :, None] + s_top[:, :, 1, None, :]
        cand_idx = i_top[:, :, 0, :, None] * PEER_NKEYS + i_top[:, :, 1, None, :]
        best, pos = lax.top_k(cand.reshape(tb, PEER_HEADS, PEER_TOPK * PEER_TOPK), PEER_TOPK)
        experts = jnp.take_along_axis(cand_idx.reshape(tb, PEER_HEADS, -1), pos, axis=-1)
        gates = jax.nn.softmax(best, axis=-1).astype(xb.dtype)
        u_sel = u[experts]
        v_sel = v[experts]
        act = jax.nn.gelu(jnp.einsum('td,thkd->thk', xb, u_sel), approximate=False)
        return jnp.einsum('thk,thkd->td', gates * act, v_sel)

    return lax.map(block, hb).reshape(shp)


def setup_inputs(seed: int = 0) -> dict:
    key = jax.random.key(seed)
    ks = jax.random.split(key, 16)
    nrm = lambda k, shape, s: s * jax.random.normal(k, shape, jnp.float32)
    return {
        "x": nrm(ks[0], (BATCH, SEQ, D_MODEL), 1.0),
        "c": nrm(ks[1], (BATCH, D_MODEL), 1.0),
        "ctx": nrm(ks[2], (BATCH, CTX_LEN, D_MODEL), 1.0),
        "c_ctx": nrm(ks[3], (D_MODEL,), 1.0),
        "w_ada": nrm(ks[4], (DEPTH, D_MODEL, N_MOD * D_MODEL), 0.5 * D_MODEL ** -0.5),
        "b_ada": nrm(ks[5], (DEPTH, N_MOD * D_MODEL), 0.02),
        "g_norm": 1.0 + nrm(ks[6], (DEPTH, 4, D_MODEL), 0.02),
        "w_in": nrm(ks[7], (DEPTH, D_MODEL, QKV_WIDTH), D_MODEL ** -0.5),
        "na_rel_bias": nrm(ks[8], (DEPTH, NA_HEADS, 2 * NA_WIN_R - 1, 2 * NA_WIN_C - 1), 0.1),
        "diff_lambda": nrm(ks[9], (DEPTH, 4, DIFF_QK), 0.1),
        "diff_subln": 1.0 + nrm(ks[10], (DEPTH, DIFF_V), 0.02),
        "w_out": nrm(ks[11], (DEPTH, D_MIX, D_MODEL), D_MIX ** -0.5),
        "peer_wq": nrm(ks[12], (DEPTH, D_MODEL, PEER_HEADS * PEER_QDIM), D_MODEL ** -0.5),
        "peer_keys": nrm(ks[13], (DEPTH, PEER_HEADS, 2, PEER_NKEYS, PEER_QDIM // 2), (PEER_QDIM // 2) ** -0.5),
        "peer_u": nrm(ks[14], (DEPTH, PEER_EXPERTS, D_MODEL), D_MODEL ** -0.5),
        "peer_v": nrm(ks[15], (DEPTH, PEER_EXPERTS, D_MODEL), D_MODEL ** -0.5),
    }


def reference(x, c, ctx, c_ctx, w_ada, b_ada, g_norm, w_in, na_rel_bias, diff_lambda,
              diff_subln, w_out, peer_wq, peer_keys, peer_u, peer_v):
    T = x.shape[1]
    cos, sin = axial_rope_tables(T, DIFF_QK)
    x_lat, x_ctx = x, ctx
    for l in range(DEPTH):
        lambda_init = 0.8 - 0.6 * math.exp(-0.3 * l)
        mod_lat = (jax.nn.silu(c) @ w_ada[l] + b_ada[l])[:, None, :]
        mod_ctx = (jax.nn.silu(c_ctx) @ w_ada[l] + b_ada[l])[None, None, :]
        sh1, sc1, gt1, sh2, sc2, gt2 = jnp.split(mod_lat, N_MOD, axis=-1)
        csh1, csc1, cgt1, csh2, csc2, cgt2 = jnp.split(mod_ctx, N_MOD, axis=-1)
        lam = diff_lambda_value(diff_lambda[l], lambda_init)

        h_lat = modulate(rmsnorm(x_lat, g_norm[l, 0]), sh1, sc1)
        h_ctx = modulate(rmsnorm(x_ctx, g_norm[l, 0]), csh1, csc1)
        qa, ka, va, qd, kd, vd = split_projection(h_lat @ w_in[l])
        qa_c, ka_c, va_c, qd_c, kd_c, vd_c = split_projection(h_ctx @ w_in[l])
        qd = apply_rope(qd, cos, sin)
        kd = apply_rope(kd, cos, sin)

        out_a = neighbourhood_attention(qa, ka, va, ka_c, va_c, na_rel_bias[l])
        out_d = diff_attention_latent(qd, kd, vd, kd_c, vd_c, lam)
        out_d = rmsnorm(out_d, diff_subln[l]) * (1.0 - lambda_init)
        y = merge_heads(out_a, out_d, w_out[l])
        x_lat_new = x_lat + gt1 * rmsnorm(y, g_norm[l, 1])

        h2 = modulate(rmsnorm(x_lat_new, g_norm[l, 2]), sh2, sc2)
        x_lat_new = x_lat_new + gt2 * rmsnorm(peer_ffn(h2, peer_wq[l], peer_keys[l], peer_u[l], peer_v[l]), g_norm[l, 3])

        if l < DEPTH - 1:
            out_a_c = dense_attention(qa_c, ka_c, va_c)
            out_d_c = rmsnorm(diff_attention_block(qd_c, kd_c, vd_c, lam), diff_subln[l]) * (1.0 - lambda_init)
            y_c = merge_heads(out_a_c, out_d_c, w_out[l])
            x_ctx = x_ctx + cgt1 * rmsnorm(y_c, g_norm[l, 1])
            h2c = modulate(rmsnorm(x_ctx, g_norm[l, 2]), csh2, csc2)
            x_ctx = x_ctx + cgt2 * rmsnorm(peer_ffn(h2c, peer_wq[l], peer_keys[l], peer_u[l], peer_v[l]), g_norm[l, 3])
        x_lat = x_lat_new
    return x_lat
```

```python
import functools
import math

import numpy as np
import jax
import jax.numpy as jnp
from jax import lax
from jax.experimental import pallas as pl
from jax.experimental.pallas import tpu as pltpu

F32 = jnp.float32
BF16 = jnp.bfloat16
I32 = jnp.int32

GRID_W = 64
HEAD_DIM = 64
NA_HEADS = 8
NA_WIDTH = NA_HEADS * HEAD_DIM
NA_WIN_R = 8
NA_WIN_C = 16
DIFF_HEADS = 4
DIFF_QK = HEAD_DIM
DIFF_V = 2 * DIFF_QK
ROPE_THETA = 10000.0
PEER_HEADS = 8
PEER_NKEYS = 128
PEER_TOPK = 16
N_MOD = 6
EPS = 1e-6
LAMBDA_INIT = 0.8 - 0.6 * math.exp(-0.3 * 0)

LANES = 128
VMEM_LIMIT = 56 * 1024 * 1024
NEG = -1e30

NA_QROWS = 8
NA_KROWS = 16
NA_QSUB = 128
PEER_TM = 32
PEER_NJ = PEER_HEADS * PEER_TOPK
TILE_STRIDE = PEER_NJ + 1


def _dot_nt(a, b):
    return lax.dot_general(a, b, (((1,), (1,)), ((), ())), preferred_element_type=F32)


def _rms(x, g):
    return x * lax.rsqrt(jnp.mean(x * x, axis=-1, keepdims=True) + EPS) * g


def _ada_kernel(c_ref, w_ref, b_ref, o_ref):
    c = c_ref[...]
    s = c / (1.0 + jnp.exp(-c))
    o_ref[...] = jnp.dot(s, w_ref[...], precision=lax.Precision.HIGHEST,
                         preferred_element_type=F32) + b_ref[...]


def _ada(cc, w, b):
    m, d = cc.shape
    n = w.shape[1]
    tn = 1024
    return pl.pallas_call(
        _ada_kernel,
        grid=(n // tn,),
        in_specs=[pl.BlockSpec((m, d), lambda j: (0, 0)),
                  pl.BlockSpec((d, tn), lambda j: (0, j)),
                  pl.BlockSpec((1, tn), lambda j: (0, j))],
        out_specs=pl.BlockSpec((m, tn), lambda j: (0, j)),
        out_shape=jax.ShapeDtypeStruct((m, n), F32),
        compiler_params=pltpu.CompilerParams(vmem_limit_bytes=VMEM_LIMIT),
        name="ada",
    )(cc, w, b)


def _qkv_kernel(x_ref, sh_ref, sc_ref, g_ref, w_ref, cos_ref, sin_ref,
                qa_ref, ka_ref, va_ref, qd_ref, kd_ref, vd_ref, *, rope):
    x = x_ref[0]
    h = _rms(x, g_ref[...]) * (1.0 + sc_ref[0]) + sh_ref[0]
    hb = h.astype(BF16)
    gw = NA_WIDTH
    scale = HEAD_DIM ** -0.5

    def proj(g):
        return jnp.dot(hb, w_ref[:, g * gw:(g + 1) * gw], preferred_element_type=F32)

    def roped(p):
        if not rope:
            return p
        cos2 = cos_ref[...]
        sin2 = sin_ref[...]
        even = (lax.broadcasted_iota(I32, cos2.shape, 1) % 2) == 0
        outs = []
        for c in range(gw // LANES):
            v = p[:, c * LANES:(c + 1) * LANES]
            nxt = pltpu.roll(v, LANES - 1, axis=1)
            prv = pltpu.roll(v, 1, axis=1)
            outs.append(v * cos2 + jnp.where(even, nxt, prv) * sin2)
        return jnp.concatenate(outs, axis=1)

    qa_ref[0] = (proj(0) * scale).astype(BF16)
    ka_ref[0] = proj(1).astype(BF16)
    va_ref[0] = proj(2).astype(BF16)
    qd_ref[0] = (roped(proj(3)) * scale).astype(BF16)
    kd_ref[0] = roped(proj(4)).astype(BF16)
    vd_ref[0] = proj(5).astype(BF16)


def _qkv(x, sh, sc, g, w, cos2, sin2, *, rope, per_batch_mod, tm):
    b, l, d = x.shape
    n = w.shape[1]
    gw = NA_WIDTH
    mod_map = (lambda i, j: (i, 0, 0)) if per_batch_mod else (lambda i, j: (0, 0, 0))
    out_sds = jax.ShapeDtypeStruct((b, l, gw), BF16)
    out_spec = pl.BlockSpec((1, tm, gw), lambda i, j: (i, j, 0))
    return pl.pallas_call(
        functools.partial(_qkv_kernel, rope=rope),
        grid=(b, l // tm),
        in_specs=[pl.BlockSpec((1, tm, d), lambda i, j: (i, j, 0)),
                  pl.BlockSpec((1, 1, d), mod_map),
                  pl.BlockSpec((1, 1, d), mod_map),
                  pl.BlockSpec((1, d), lambda i, j: (0, 0)),
                  pl.BlockSpec((d, n), lambda i, j: (0, 0)),
                  pl.BlockSpec((tm, LANES), lambda i, j: (j, 0)),
                  pl.BlockSpec((tm, LANES), lambda i, j: (j, 0))],
        out_specs=[out_spec] * 6,
        out_shape=[out_sds] * 6,
        compiler_params=pltpu.CompilerParams(vmem_limit_bytes=VMEM_LIMIT),
        name="qkv_rope" if rope else "qkv_ctx",
    )(x, sh, sc, g, w, cos2, sin2)


def _na_kernel(q_ref, k_ref, v_ref, kc_ref, vc_ref, bias_ref, o_ref, *, rows):
    rb = pl.program_id(2)
    ks = jnp.clip(rb * NA_QROWS - NA_WIN_R // 2, 0, rows - NA_KROWS)
    start = pl.multiple_of(ks * GRID_W, GRID_W)
    nk = NA_KROWS * GRID_W
    kw = k_ref[0, pl.ds(start, nk), :]
    vw = v_ref[0, pl.ds(start, nk), :]
    kc = kc_ref[0]
    vc = vc_ref[0]
    lane = lax.broadcasted_iota(I32, (1, LANES), 1)
    for s in range(NA_QROWS * GRID_W // NA_QSUB):
        sl = slice(s * NA_QSUB, (s + 1) * NA_QSUB)
        q = q_ref[0, sl, :]
        outs = []
        for hh in range(2):
            qh = jnp.where((lane // HEAD_DIM) == hh, q, jnp.zeros_like(q))
            s_loc = _dot_nt(qh, kw) + bias_ref[0, hh, sl, :]
            s_ctx = _dot_nt(qh, kc)
            mx = jnp.maximum(jnp.max(s_loc, axis=-1, keepdims=True),
                             jnp.max(s_ctx, axis=-1, keepdims=True))
            p_loc = jnp.exp(s_loc - mx)
            p_ctx = jnp.exp(s_ctx - mx)
            l = jnp.sum(p_loc, axis=-1, keepdims=True) + jnp.sum(p_ctx, axis=-1, keepdims=True)
            o = (jnp.dot(p_loc.astype(BF16), vw, preferred_element_type=F32)
                 + jnp.dot(p_ctx.astype(BF16), vc, preferred_element_type=F32))
            outs.append(o / l)
        o_ref[0, sl, :] = jnp.where(lane < HEAD_DIM, outs[0], outs[1]).astype(BF16)


def _na_bias_table(rel_bias, rows):
    nrb = rows // NA_QROWS
    j = np.arange(NA_QROWS)[:, None, None, None]
    qc = np.arange(GRID_W)[None, :, None, None]
    krl = np.arange(NA_KROWS)[None, None, :, None]
    kc = np.arange(GRID_W)[None, None, None, :]
    tabs = []
    for rb in (0, 1, nrb - 1):
        r0 = rb * NA_QROWS
        ks = int(np.clip(r0 - NA_WIN_R // 2, 0, rows - NA_KROWS))
        qr = r0 + j
        kr = ks + krl
        r_start = np.clip(qr - NA_WIN_R // 2, 0, rows - NA_WIN_R)
        c_start = np.clip(qc - NA_WIN_C // 2, 0, GRID_W - NA_WIN_C)
        valid = (kr >= r_start) & (kr < r_start + NA_WIN_R) & (kc >= c_start) & (kc < c_start + NA_WIN_C)
        ro = np.clip(kr - qr + NA_WIN_R - 1, 0, 2 * NA_WIN_R - 2)
        co = np.clip(kc - qc + NA_WIN_C - 1, 0, 2 * NA_WIN_C - 2)
        shape = (NA_QROWS, GRID_W, NA_KROWS, GRID_W)
        flat = (np.broadcast_to(ro, shape) * (2 * NA_WIN_C - 1) + np.broadcast_to(co, shape)).reshape(-1)
        vals = jnp.take(rel_bias.reshape(NA_HEADS, -1), jnp.asarray(flat, I32), axis=1)
        vals = vals.reshape((NA_HEADS,) + shape)
        vals = jnp.where(jnp.asarray(np.broadcast_to(valid, shape))[None], vals, NEG)
        tabs.append(vals.reshape(NA_HEADS, NA_QROWS * GRID_W, NA_KROWS * GRID_W))
    return jnp.stack(tabs)


def _na(qa, ka, va, ka_c, va_c, bias):
    b, t, _ = qa.shape
    c = ka_c.shape[1]
    rows = t // GRID_W
    nrb = rows // NA_QROWS
    tq = NA_QROWS * GRID_W
    nk = NA_KROWS * GRID_W

    def bias_map(i, hp, rb):
        pat = jnp.where(rb == 0, 0, jnp.where(rb == nrb - 1, 2, 1))
        return (pat, hp, 0, 0)

    return pl.pallas_call(
        functools.partial(_na_kernel, rows=rows),
        grid=(b, NA_HEADS // 2, nrb),
        in_specs=[pl.BlockSpec((1, tq, LANES), lambda i, hp, rb: (i, rb, hp)),
                  pl.BlockSpec((1, t, LANES), lambda i, hp, rb: (i, 0, hp)),
                  pl.BlockSpec((1, t, LANES), lambda i, hp, rb: (i, 0, hp)),
                  pl.BlockSpec((1, c, LANES), lambda i, hp, rb: (i, 0, hp)),
                  pl.BlockSpec((1, c, LANES), lambda i, hp, rb: (i, 0, hp)),
                  pl.BlockSpec((1, 2, tq, nk), bias_map)],
        out_specs=pl.BlockSpec((1, tq, LANES), lambda i, hp, rb: (i, rb, hp)),
        out_shape=jax.ShapeDtypeStruct((b, t, NA_WIDTH), BF16),
        compiler_params=pltpu.CompilerParams(vmem_limit_bytes=VMEM_LIMIT),
        name="na_attn",
    )(qa, ka, va, ka_c, va_c, bias)


def _diff_kernel(lam_ref, q_ref, k_ref, v_ref, kc_ref, vc_ref, g_ref, o_ref, *, tk):
    q = q_ref[0]
    tq = q.shape[0]
    t = k_ref.shape[1]
    lane = lax.broadcasted_iota(I32, (1, LANES), 1)
    zero = jnp.zeros_like(q)
    qs = (jnp.where(lane < DIFF_QK, q, zero), jnp.where(lane >= DIFF_QK, q, zero))

    def update(kb, vb, carry):
        new = []
        for mi in range(2):
            m, l, a = carry[mi]
            s = _dot_nt(qs[mi], kb)
            mn = jnp.maximum(m, jnp.max(s, axis=-1, keepdims=True))
            alpha = jnp.exp(m - mn)
            p = jnp.exp(s - mn)
            l = alpha * l + jnp.sum(p, axis=-1, keepdims=True)
            a = alpha * a + jnp.dot(p.astype(BF16), vb, preferred_element_type=F32)
            new.append((mn, l, a))
        return tuple(new)

    init = tuple((jnp.full((tq, 1), NEG, F32), jnp.zeros((tq, 1), F32), jnp.zeros((tq, DIFF_V), F32))
                 for _ in range(2))
    carry = update(kc_ref[0], vc_ref[0], init)

    def body(i, carry):
        st = pl.multiple_of(i * tk, tk)
        return update(k_ref[0, pl.ds(st, tk), :], v_ref[0, pl.ds(st, tk), :], carry)

    carry = lax.fori_loop(0, t // tk, body, carry)
    lp = lam_ref[...]
    lam = (jnp.exp(jnp.sum(lp[0:1] * lp[1:2], axis=-1, keepdims=True))
           - jnp.exp(jnp.sum(lp[2:3] * lp[3:4], axis=-1, keepdims=True)) + LAMBDA_INIT)
    (_, l0, a0), (_, l1, a1) = carry
    out = a0 / l0 - lam * (a1 / l1)
    o_ref[0] = (_rms(out, g_ref[...]) * (1.0 - LAMBDA_INIT)).astype(BF16)


def _diff(lam_params, qd, kd, vd, kd_c, vd_c, subln, *, tq, tk):
    b, t, w = qd.shape
    c = kd_c.shape[1]
    return pl.pallas_call(
        functools.partial(_diff_kernel, tk=tk),
        grid=(b, DIFF_HEADS, t // tq),
        in_specs=[pl.BlockSpec((4, DIFF_QK), lambda i, h, j: (0, 0)),
                  pl.BlockSpec((1, tq, LANES), lambda i, h, j: (i, j, h)),
                  pl.BlockSpec((1, t, LANES), lambda i, h, j: (i, 0, h)),
                  pl.BlockSpec((1, t, LANES), lambda i, h, j: (i, 0, h)),
                  pl.BlockSpec((1, c, LANES), lambda i, h, j: (i, 0, h)),
                  pl.BlockSpec((1, c, LANES), lambda i, h, j: (i, 0, h)),
                  pl.BlockSpec((1, DIFF_V), lambda i, h, j: (0, 0))],
        out_specs=pl.BlockSpec((1, tq, LANES), lambda i, h, j: (i, j, h)),
        out_shape=jax.ShapeDtypeStruct((b, t, w), BF16),
        compiler_params=pltpu.CompilerParams(vmem_limit_bytes=VMEM_LIMIT),
        name="diff_attn",
    )(lam_params, qd, kd, vd, kd_c, vd_c, subln)


def _out_kernel(oa_ref, od_ref, x_ref, gt1_ref, sh2_ref, sc2_ref, g1_ref, g2_ref,
                wout_ref, wq_ref, keys_ref, x1_ref, h2_ref, st_ref):
    wa = NA_WIDTH
    y = (jnp.dot(oa_ref[0], wout_ref[0:wa, :], preferred_element_type=F32)
         + jnp.dot(od_ref[0], wout_ref[wa:, :], preferred_element_type=F32))
    x1 = x_ref[0] + gt1_ref[0] * _rms(y, g1_ref[...])
    x1_ref[0] = x1
    h2 = _rms(x1, g2_ref[...]) * (1.0 + sc2_ref[0]) + sh2_ref[0]
    h2_ref[0] = h2
    qb = jnp.dot(h2.astype(BF16), wq_ref[...], preferred_element_type=F32).astype(BF16)
    for hp in range(2 * PEER_HEADS):
        st_ref[0, hp] = _dot_nt(keys_ref[hp], qb[:, hp * LANES:(hp + 1) * LANES])


def _out(oa, od, x, gt1, sh2, sc2, g1, g2, wout, wq, keys, *, tm):
    b, t, d = x.shape
    nq = wq.shape[1]
    nhp = keys.shape[0]
    mod = pl.BlockSpec((1, 1, d), lambda i, j: (i, 0, 0))
    gsp = pl.BlockSpec((1, d), lambda i, j: (0, 0))
    tok = pl.BlockSpec((1, tm, d), lambda i, j: (i, j, 0))
    half = pl.BlockSpec((1, tm, NA_WIDTH), lambda i, j: (i, j, 0))
    return pl.pallas_call(
        _out_kernel,
        grid=(b, t // tm),
        in_specs=[half, half, tok, mod, mod, mod, gsp, gsp,
                  pl.BlockSpec((d, d), lambda i, j: (0, 0)),
                  pl.BlockSpec((d, nq), lambda i, j: (0, 0)),
                  pl.BlockSpec((nhp, PEER_NKEYS, LANES), lambda i, j: (0, 0, 0))],
        out_specs=[tok, tok, pl.BlockSpec((1, nhp, PEER_NKEYS, tm), lambda i, j: (i, 0, 0, j))],
        out_shape=[jax.ShapeDtypeStruct((b, t, d), F32), jax.ShapeDtypeStruct((b, t, d), F32),
                   jax.ShapeDtypeStruct((b, nhp, PEER_NKEYS, t), F32)],
        compiler_params=pltpu.CompilerParams(vmem_limit_bytes=VMEM_LIMIT),
        name="out_proj_peer_scores",
    )(oa, od, x, gt1, sh2, sc2, g1, g2, wout, wq, keys)


def _top1(s, iota):
    m = jnp.max(s, axis=0, keepdims=True)
    idx = jnp.min(jnp.where(s == m, iota, s.shape[0]), axis=0, keepdims=True)
    return m, idx, jnp.where(iota == idx, -jnp.inf, s)


def _topk_kernel(st_ref, idx_ref, gate_ref):
    tl = st_ref.shape[-1]
    k = PEER_TOPK
    iota_n = lax.broadcasted_iota(I32, (PEER_NKEYS, tl), 0)
    iota_c = lax.broadcasted_iota(I32, (k * k, tl), 0)
    for h in range(PEER_HEADS):
        tops = []
        for p in range(2):
            s = st_ref[0, 2 * h + p]
            vals, idxs = [], []
            for _ in range(k):
                m, i, s = _top1(s, iota_n)
                vals.append(m)
                idxs.append(i)
            tops.append((jnp.concatenate(vals, axis=0), jnp.concatenate(idxs, axis=0)))
        (s0, i0), (s1, i1) = tops
        cand = jnp.concatenate([s0[a:a + 1] + s1 for a in range(k)], axis=0)
        cidx = jnp.concatenate([i0[a:a + 1] * PEER_NKEYS + i1 for a in range(k)], axis=0)
        best, experts = [], []
        for _ in range(k):
            m, pos, cand = _top1(cand, iota_c)
            best.append(m)
            experts.append(jnp.sum(jnp.where(iota_c == pos, cidx, 0), axis=0, keepdims=True))
        best = jnp.concatenate(best, axis=0)
        e = jnp.exp(best - best[0:1])
        gate_ref[0, h * k:(h + 1) * k, :] = e / jnp.sum(e, axis=0, keepdims=True)
        idx_ref[0, h * k:(h + 1) * k, :] = jnp.concatenate(experts, axis=0)


def _topk(st, *, tl):
    b, nhp, nkeys, t = st.shape
    nj = PEER_NJ
    out_spec = pl.BlockSpec((1, nj, tl), lambda i, j: (i, 0, j))
    return pl.pallas_call(
        _topk_kernel,
        grid=(b, t // tl),
        in_specs=[pl.BlockSpec((1, nhp, nkeys, tl), lambda i, j: (i, 0, 0, j))],
        out_specs=[out_spec, out_spec],
        out_shape=[jax.ShapeDtypeStruct((b, nj, t), I32), jax.ShapeDtypeStruct((b, nj, t), F32)],
        compiler_params=pltpu.CompilerParams(vmem_limit_bytes=VMEM_LIMIT),
        name="peer_topk",
    )(st)


def _pack_table(tab):
    e, d = tab.shape
    tb = tab.astype(BF16)
    lo = lax.bitcast_convert_type(tb[:, :d // 2], jnp.uint16).astype(jnp.uint32)
    hi = lax.bitcast_convert_type(tb[:, d // 2:], jnp.uint16).astype(jnp.uint32)
    words = lax.bitcast_convert_type((hi << 16) | lo, I32)
    return words.reshape(e * (d // 2 // LANES), LANES)


def _unpack(slab):
    lo = pltpu.bitcast(slab << 16, F32)
    hi = pltpu.bitcast(slab & jnp.int32(-65536), F32)
    return lo, hi


def _peer_u_kernel(idx_ref, x_ref, gate_ref, u_ref, w_ref, tile_ref):
    nj = PEER_NJ
    s = TILE_STRIDE
    ones = jnp.ones((8, LANES), F32)

    def token(t, carry):
        xlo = x_ref[t, 0:4, :]
        xhi = x_ref[t, 4:8, :]
        for j in range(nj):
            row = pl.multiple_of(idx_ref[t * nj + j], 4)
            lo, hi = _unpack(u_ref[pl.ds(row, 4), :])
            tile_ref[pl.ds(j, 4, stride=s), :] = lo * xlo + hi * xhi
        r = ((tile_ref[0:nj, :] + tile_ref[s:s + nj, :])
             + (tile_ref[2 * s:2 * s + nj, :] + tile_ref[3 * s:3 * s + nj, :]))
        act = lax.dot_general(ones, r, (((1,), (1,)), ((), ())), precision=lax.Precision.HIGHEST,
                              preferred_element_type=F32)[0:1, :]
        gelu = 0.5 * act * (1.0 + lax.erf(act * (2.0 ** -0.5)))
        w_ref[pl.ds(t, 1), :] = gate_ref[pl.ds(t, 1), :] * gelu
        return carry

    lax.fori_loop(0, x_ref.shape[0], token, 0)


def _peer_u(idx4, x3, gate, table):
    n = x3.shape[0]
    nj = PEER_NJ
    tm = PEER_TM
    return pl.pallas_call(
        _peer_u_kernel,
        grid=(n // tm,),
        in_specs=[pl.BlockSpec((tm * nj,), lambda i: (i,), memory_space=pltpu.SMEM),
                  pl.BlockSpec((tm, 8, LANES), lambda i: (i, 0, 0)),
                  pl.BlockSpec((tm, nj), lambda i: (i, 0)),
                  pl.BlockSpec(memory_space=pltpu.VMEM)],
        out_specs=pl.BlockSpec((tm, nj), lambda i: (i, 0)),
        out_shape=jax.ShapeDtypeStruct((n, nj), F32),
        scratch_shapes=[pltpu.VMEM((4 * TILE_STRIDE + 4, LANES), F32)],
        compiler_params=pltpu.CompilerParams(vmem_limit_bytes=VMEM_LIMIT),
        name="peer_u",
    )(idx4, x3, gate, table)


def _peer_v_kernel(idx_ref, w_ref, v_ref, o_ref):
    nj = PEER_NJ
    nacc = 4

    def token(t, carry):
        acc_lo = [jnp.zeros((4, LANES), F32) for _ in range(nacc)]
        acc_hi = [jnp.zeros((4, LANES), F32) for _ in range(nacc)]
        for j in range(nj):
            row = pl.multiple_of(idx_ref[t * nj + j], 4)
            wj = w_ref[t * nj + j]
            lo, hi = _unpack(v_ref[pl.ds(row, 4), :])
            acc_lo[j % nacc] = acc_lo[j % nacc] + wj * lo
            acc_hi[j % nacc] = acc_hi[j % nacc] + wj * hi
        o_ref[t, 0:4, :] = (acc_lo[0] + acc_lo[1]) + (acc_lo[2] + acc_lo[3])
        o_ref[t, 4:8, :] = (acc_hi[0] + acc_hi[1]) + (acc_hi[2] + acc_hi[3])
        return carry

    lax.fori_loop(0, o_ref.shape[0], token, 0)


def _peer_v(idx4, w, table):
    n = w.shape[0] // PEER_NJ
    nj = PEER_NJ
    tm = PEER_TM
    return pl.pallas_call(
        _peer_v_kernel,
        grid=(n // tm,),
        in_specs=[pl.BlockSpec((tm * nj,), lambda i: (i,), memory_space=pltpu.SMEM),
                  pl.BlockSpec((tm * nj,), lambda i: (i,), memory_space=pltpu.SMEM),
                  pl.BlockSpec(memory_space=pltpu.VMEM)],
        out_specs=pl.BlockSpec((tm, 8, LANES), lambda i: (i, 0, 0)),
        out_shape=jax.ShapeDtypeStruct((n, 8, LANES), F32),
        compiler_params=pltpu.CompilerParams(vmem_limit_bytes=VMEM_LIMIT),
        name="peer_v",
    )(idx4, w, table)


def _final_kernel(x1_ref, y_ref, gt2_ref, g_ref, o_ref):
    o_ref[0] = x1_ref[0] + gt2_ref[0] * _rms(y_ref[0], g_ref[...])


def _final(x1, y, gt2, g, *, tm):
    b, t, d = x1.shape
    tok = pl.BlockSpec((1, tm, d), lambda i, j: (i, j, 0))
    return pl.pallas_call(
        _final_kernel,
        grid=(b, t // tm),
        in_specs=[tok, tok, pl.BlockSpec((1, 1, d), lambda i, j: (i, 0, 0)),
                  pl.BlockSpec((1, d), lambda i, j: (0, 0))],
        out_specs=tok,
        out_shape=jax.ShapeDtypeStruct((b, t, d), F32),
        compiler_params=pltpu.CompilerParams(vmem_limit_bytes=VMEM_LIMIT),
        name="final_residual",
    )(x1, y, gt2, g)


def _rope_tables(t):
    tok = jnp.arange(t)
    row = (tok // GRID_W).astype(F32)
    col = (tok % GRID_W).astype(F32)
    n_freq = DIFF_QK // 4
    inv = ROPE_THETA ** (-jnp.arange(n_freq, dtype=F32) / n_freq)
    ang = jnp.concatenate([row[:, None] * inv, col[:, None] * inv], axis=-1)
    ang2 = jnp.tile(jnp.repeat(ang, 2, axis=1), (1, LANES // DIFF_QK))
    sign = jnp.where(jnp.arange(LANES) % 2 == 0, -1.0, 1.0).astype(F32)
    return jnp.cos(ang2), jnp.sin(ang2) * sign


def _head_major_cols(w):
    d = w.shape[0]
    return w.reshape(d, 2, DIFF_HEADS, DIFF_QK).transpose(0, 2, 1, 3).reshape(d, -1)


def kernel(x, c, ctx, c_ctx, w_ada, b_ada, g_norm, w_in, na_rel_bias, diff_lambda, diff_subln,
           w_out, peer_wq, peer_keys, peer_u, peer_v):
    assert w_ada.shape[0] == 1, "single-layer kernel"
    b, t, d = x.shape
    n_ctx = ctx.shape[1]
    rows = t // GRID_W
    assert t % (NA_QROWS * GRID_W) == 0 and rows >= 3 * NA_QROWS

    pad = (-(b + 1)) % 8
    cc = jnp.concatenate([c, c_ctx[None], jnp.zeros((pad, d), F32)], axis=0)
    mod = _ada(cc, w_ada[0], b_ada[0][None]).reshape(b + 1 + pad, N_MOD, d)
    lat = lambda i: mod[:b, i][:, None, :]
    cxm = lambda i: mod[b:b + 1, i][:, None, :]

    gw = NA_WIDTH
    w_in0 = w_in[0]
    w_perm = jnp.concatenate([w_in0[:, :3 * gw], _head_major_cols(w_in0[:, 3 * gw:4 * gw]),
                              _head_major_cols(w_in0[:, 4 * gw:5 * gw]), w_in0[:, 5 * gw:]], axis=1).astype(BF16)
    cos2, sin2 = _rope_tables(t)
    g0 = g_norm[0, 0][None]
    qa, ka, va, qd, kd, vd = _qkv(x, lat(0), lat(1), g0, w_perm, cos2, sin2,
                                  rope=True, per_batch_mod=True, tm=512)
    _, ka_c, va_c, _, kd_c, vd_c = _qkv(ctx, cxm(0), cxm(1), g0, w_perm, cos2[:n_ctx], sin2[:n_ctx],
                                        rope=False, per_batch_mod=False, tm=n_ctx)

    out_a = _na(qa, ka, va, ka_c, va_c, _na_bias_table(na_rel_bias[0], rows))
    out_d = _diff(diff_lambda[0], qd, kd, vd, kd_c, vd_c, diff_subln[0][None], tq=256, tk=512)

    keys = peer_keys[0].reshape(2 * PEER_HEADS, PEER_NKEYS, -1).astype(BF16)
    x1, h2, st = _out(out_a, out_d, x, lat(2), lat(3), lat(4), g_norm[0, 1][None], g_norm[0, 2][None],
                      w_out[0].astype(BF16), peer_wq[0].astype(BF16), keys, tm=256)

    idx_t, gate_t = _topk(st, tl=256)
    idx4 = (idx_t.transpose(0, 2, 1) * 4).reshape(-1)
    gate = gate_t.transpose(0, 2, 1).reshape(b * t, PEER_NJ)

    w = _peer_u(idx4, h2.reshape(b * t, 8, LANES), gate, _pack_table(peer_u[0]))
    y = _peer_v(idx4, w.reshape(-1), _pack_table(peer_v[0])).reshape(b, t, d)
    return _final(x1, y, lat(5), g_norm[0, 3][None], tm=512)
```

```python
import functools
import math

import numpy as np
import jax
import jax.numpy as jnp
from jax import lax
from jax.experimental import pallas as pl
from jax.experimental.pallas import tpu as pltpu

F32 = jnp.float32
BF16 = jnp.bfloat16
I32 = jnp.int32

GRID_W = 64
HEAD_DIM = 64
NA_HEADS = 8
NA_WIDTH = NA_HEADS * HEAD_DIM
NA_WIN_R = 8
NA_WIN_C = 16
DIFF_HEADS = 4
DIFF_QK = HEAD_DIM
DIFF_V = 2 * DIFF_QK
ROPE_THETA = 10000.0
PEER_HEADS = 8
PEER_NKEYS = 128
PEER_TOPK = 16
N_MOD = 6
EPS = 1e-6
LAMBDA_INIT = 0.8 - 0.6 * math.exp(-0.3 * 0)

LANES = 128
VMEM_LIMIT = 56 * 1024 * 1024
NEG = -1e30

NA_QROWS = 8
NA_KROWS = 16
NA_QSUB = 128
PEER_TM = 32
PEER_NJ = PEER_HEADS * PEER_TOPK
TILE_STRIDE = PEER_NJ + 1
PEER_UNROLL = 8


def _dot_nt(a, b):
    return lax.dot_general(a, b, (((1,), (1,)), ((), ())), preferred_element_type=F32)


def _rms(x, g):
    return x * lax.rsqrt(jnp.mean(x * x, axis=-1, keepdims=True) + EPS) * g


def _ada_kernel(c_ref, w_ref, b_ref, o_ref):
    c = c_ref[...]
    s = c / (1.0 + jnp.exp(-c))
    o_ref[...] = jnp.dot(s, w_ref[...], precision=lax.Precision.HIGHEST,
                         preferred_element_type=F32) + b_ref[...]


def _ada(cc, w, b):
    m, d = cc.shape
    n = w.shape[1]
    tn = 1024
    return pl.pallas_call(
        _ada_kernel,
        grid=(n // tn,),
        in_specs=[pl.BlockSpec((m, d), lambda j: (0, 0)),
                  pl.BlockSpec((d, tn), lambda j: (0, j)),
                  pl.BlockSpec((1, tn), lambda j: (0, j))],
        out_specs=pl.BlockSpec((m, tn), lambda j: (0, j)),
        out_shape=jax.ShapeDtypeStruct((m, n), F32),
        compiler_params=pltpu.CompilerParams(vmem_limit_bytes=VMEM_LIMIT),
        name="ada",
    )(cc, w, b)


def _qkv_kernel(x_ref, sh_ref, sc_ref, g_ref, w_ref, cos_ref, sin_ref,
                qa_ref, ka_ref, va_ref, qd_ref, kd_ref, vd_ref, *, rope):
    x = x_ref[0]
    h = _rms(x, g_ref[...]) * (1.0 + sc_ref[0]) + sh_ref[0]
    hb = h.astype(BF16)
    gw = NA_WIDTH
    scale = HEAD_DIM ** -0.5

    def proj(g):
        return jnp.dot(hb, w_ref[:, g * gw:(g + 1) * gw], preferred_element_type=F32)

    def roped(p):
        if not rope:
            return p
        cos2 = cos_ref[...]
        sin2 = sin_ref[...]
        even = (lax.broadcasted_iota(I32, cos2.shape, 1) % 2) == 0
        outs = []
        for c in range(gw // LANES):
            v = p[:, c * LANES:(c + 1) * LANES]
            nxt = pltpu.roll(v, LANES - 1, axis=1)
            prv = pltpu.roll(v, 1, axis=1)
            outs.append(v * cos2 + jnp.where(even, nxt, prv) * sin2)
        return jnp.concatenate(outs, axis=1)

    qa_ref[0] = (proj(0) * scale).astype(BF16)
    ka_ref[0] = proj(1).astype(BF16)
    va_ref[0] = proj(2).astype(BF16)
    qd_ref[0] = (roped(proj(3)) * scale).astype(BF16)
    kd_ref[0] = roped(proj(4)).astype(BF16)
    vd_ref[0] = proj(5).astype(BF16)


def _qkv(x, sh, sc, g, w, cos2, sin2, *, rope, per_batch_mod, tm):
    b, l, d = x.shape
    n = w.shape[1]
    gw = NA_WIDTH
    mod_map = (lambda i, j: (i, 0, 0)) if per_batch_mod else (lambda i, j: (0, 0, 0))
    out_sds = jax.ShapeDtypeStruct((b, l, gw), BF16)
    out_spec = pl.BlockSpec((1, tm, gw), lambda i, j: (i, j, 0))
    return pl.pallas_call(
        functools.partial(_qkv_kernel, rope=rope),
        grid=(b, l // tm),
        in_specs=[pl.BlockSpec((1, tm, d), lambda i, j: (i, j, 0)),
                  pl.BlockSpec((1, 1, d), mod_map),
                  pl.BlockSpec((1, 1, d), mod_map),
                  pl.BlockSpec((1, d), lambda i, j: (0, 0)),
                  pl.BlockSpec((d, n), lambda i, j: (0, 0)),
                  pl.BlockSpec((tm, LANES), lambda i, j: (j, 0)),
                  pl.BlockSpec((tm, LANES), lambda i, j: (j, 0))],
        out_specs=[out_spec] * 6,
        out_shape=[out_sds] * 6,
        compiler_params=pltpu.CompilerParams(vmem_limit_bytes=VMEM_LIMIT),
        name="qkv_rope" if rope else "qkv_ctx",
    )(x, sh, sc, g, w, cos2, sin2)


def _na_kernel(q_ref, k_ref, v_ref, kc_ref, vc_ref, bias_ref, o_ref, *, rows):
    rb = pl.program_id(2)
    ks = jnp.clip(rb * NA_QROWS - NA_WIN_R // 2, 0, rows - NA_KROWS)
    start = pl.multiple_of(ks * GRID_W, GRID_W)
    nk = NA_KROWS * GRID_W
    kw = k_ref[0, pl.ds(start, nk), :]
    vw = v_ref[0, pl.ds(start, nk), :]
    kc = kc_ref[0]
    vc = vc_ref[0]
    lane = lax.broadcasted_iota(I32, (1, LANES), 1)
    for s in range(NA_QROWS * GRID_W // NA_QSUB):
        sl = slice(s * NA_QSUB, (s + 1) * NA_QSUB)
        q = q_ref[0, sl, :]
        outs = []
        for hh in range(2):
            qh = jnp.where((lane // HEAD_DIM) == hh, q, jnp.zeros_like(q))
            s_loc = _dot_nt(qh, kw) + bias_ref[0, hh, sl, :]
            s_ctx = _dot_nt(qh, kc)
            mx = jnp.maximum(jnp.max(s_loc, axis=-1, keepdims=True),
                             jnp.max(s_ctx, axis=-1, keepdims=True))
            p_loc = jnp.exp(s_loc - mx)
            p_ctx = jnp.exp(s_ctx - mx)
            l = jnp.sum(p_loc, axis=-1, keepdims=True) + jnp.sum(p_ctx, axis=-1, keepdims=True)
            o = (jnp.dot(p_loc.astype(BF16), vw, preferred_element_type=F32)
                 + jnp.dot(p_ctx.astype(BF16), vc, preferred_element_type=F32))
            outs.append(o / l)
        o_ref[0, sl, :] = jnp.where(lane < HEAD_DIM, outs[0], outs[1]).astype(BF16)


def _na_bias_table(rel_bias, rows):
    nrb = rows // NA_QROWS
    n_ro = 2 * NA_WIN_R - 1
    qc = np.arange(GRID_W)[:, None]
    kc = np.arange(GRID_W)[None, :]
    c_start = np.clip(qc - NA_WIN_C // 2, 0, GRID_W - NA_WIN_C)
    valid_c = (kc >= c_start) & (kc < c_start + NA_WIN_C)
    co = np.clip(kc - qc + NA_WIN_C - 1, 0, 2 * NA_WIN_C - 2)
    tz = jnp.take(rel_bias, jnp.asarray(co.reshape(-1), I32), axis=2).reshape(NA_HEADS, n_ro, GRID_W, GRID_W)
    tz = jnp.where(jnp.asarray(valid_c)[None, None], tz, NEG)
    tz = jnp.concatenate([tz, jnp.full((NA_HEADS, 1, GRID_W, GRID_W), NEG, F32)], axis=1)
    sel = []
    for rb in (0, 1, nrb - 1):
        r0 = rb * NA_QROWS
        ks = int(np.clip(r0 - NA_WIN_R // 2, 0, rows - NA_KROWS))
        qr = r0 + np.arange(NA_QROWS)[:, None]
        kr = ks + np.arange(NA_KROWS)[None, :]
        r_start = np.clip(qr - NA_WIN_R // 2, 0, rows - NA_WIN_R)
        valid_r = (kr >= r_start) & (kr < r_start + NA_WIN_R)
        sel.append(np.where(valid_r, kr - qr + NA_WIN_R - 1, n_ro))
    sel = np.stack(sel).reshape(-1)
    blocks = jnp.take(tz, jnp.asarray(sel, I32), axis=1)
    blocks = blocks.reshape(NA_HEADS, 3, NA_QROWS, NA_KROWS, GRID_W, GRID_W)
    return blocks.transpose(1, 0, 2, 4, 3, 5).reshape(3, NA_HEADS, NA_QROWS * GRID_W, NA_KROWS * GRID_W)


def _na(qa, ka, va, ka_c, va_c, bias):
    b, t, _ = qa.shape
    c = ka_c.shape[1]
    rows = t // GRID_W
    nrb = rows // NA_QROWS
    tq = NA_QROWS * GRID_W
    nk = NA_KROWS * GRID_W

    def bias_map(i, hp, rb):
        pat = jnp.where(rb == 0, 0, jnp.where(rb == nrb - 1, 2, 1))
        return (pat, hp, 0, 0)

    return pl.pallas_call(
        functools.partial(_na_kernel, rows=rows),
        grid=(b, NA_HEADS // 2, nrb),
        in_specs=[pl.BlockSpec((1, tq, LANES), lambda i, hp, rb: (i, rb, hp)),
                  pl.BlockSpec((1, t, LANES), lambda i, hp, rb: (i, 0, hp)),
                  pl.BlockSpec((1, t, LANES), lambda i, hp, rb: (i, 0, hp)),
                  pl.BlockSpec((1, c, LANES), lambda i, hp, rb: (i, 0, hp)),
                  pl.BlockSpec((1, c, LANES), lambda i, hp, rb: (i, 0, hp)),
                  pl.BlockSpec((1, 2, tq, nk), bias_map)],
        out_specs=pl.BlockSpec((1, tq, LANES), lambda i, hp, rb: (i, rb, hp)),
        out_shape=jax.ShapeDtypeStruct((b, t, NA_WIDTH), BF16),
        compiler_params=pltpu.CompilerParams(vmem_limit_bytes=VMEM_LIMIT),
        name="na_attn",
    )(qa, ka, va, ka_c, va_c, bias)


def _diff_kernel(lam_ref, q_ref, k_ref, v_ref, kc_ref, vc_ref, g_ref, o_ref, *, tk):
    q = q_ref[0]
    tq = q.shape[0]
    t = k_ref.shape[1]
    lane = lax.broadcasted_iota(I32, (1, LANES), 1)
    zero = jnp.zeros_like(q)
    qs = (jnp.where(lane < DIFF_QK, q, zero), jnp.where(lane >= DIFF_QK, q, zero))

    def update(kb, vb, carry):
        new = []
        for mi in range(2):
            m, l, a = carry[mi]
            s = _dot_nt(qs[mi], kb)
            mn = jnp.maximum(m, jnp.max(s, axis=-1, keepdims=True))
            alpha = jnp.exp(m - mn)
            p = jnp.exp(s - mn)
            l = alpha * l + jnp.sum(p, axis=-1, keepdims=True)
            a = alpha * a + jnp.dot(p.astype(BF16), vb, preferred_element_type=F32)
            new.append((mn, l, a))
        return tuple(new)

    init = tuple((jnp.full((tq, 1), NEG, F32), jnp.zeros((tq, 1), F32), jnp.zeros((tq, DIFF_V), F32))
                 for _ in range(2))
    carry = update(kc_ref[0], vc_ref[0], init)

    def body(i, carry):
        st = pl.multiple_of(i * tk, tk)
        return update(k_ref[0, pl.ds(st, tk), :], v_ref[0, pl.ds(st, tk), :], carry)

    carry = lax.fori_loop(0, t // tk, body, carry)
    lp = lam_ref[...]
    lam = (jnp.exp(jnp.sum(lp[0:1] * lp[1:2], axis=-1, keepdims=True))
           - jnp.exp(jnp.sum(lp[2:3] * lp[3:4], axis=-1, keepdims=True)) + LAMBDA_INIT)
    (_, l0, a0), (_, l1, a1) = carry
    out = a0 / l0 - lam * (a1 / l1)
    o_ref[0] = (_rms(out, g_ref[...]) * (1.0 - LAMBDA_INIT)).astype(BF16)


def _diff(lam_params, qd, kd, vd, kd_c, vd_c, subln, *, tq, tk):
    b, t, w = qd.shape
    c = kd_c.shape[1]
    return pl.pallas_call(
        functools.partial(_diff_kernel, tk=tk),
        grid=(b, DIFF_HEADS, t // tq),
        in_specs=[pl.BlockSpec((4, DIFF_QK), lambda i, h, j: (0, 0)),
                  pl.BlockSpec((1, tq, LANES), lambda i, h, j: (i, j, h)),
                  pl.BlockSpec((1, t, LANES), lambda i, h, j: (i, 0, h)),
                  pl.BlockSpec((1, t, LANES), lambda i, h, j: (i, 0, h)),
                  pl.BlockSpec((1, c, LANES), lambda i, h, j: (i, 0, h)),
                  pl.BlockSpec((1, c, LANES), lambda i, h, j: (i, 0, h)),
                  pl.BlockSpec((1, DIFF_V), lambda i, h, j: (0, 0))],
        out_specs=pl.BlockSpec((1, tq, LANES), lambda i, h, j: (i, j, h)),
        out_shape=jax.ShapeDtypeStruct((b, t, w), BF16),
        compiler_params=pltpu.CompilerParams(vmem_limit_bytes=VMEM_LIMIT),
        name="diff_attn",
    )(lam_params, qd, kd, vd, kd_c, vd_c, subln)


def _out_kernel(oa_ref, od_ref, x_ref, gt1_ref, sh2_ref, sc2_ref, g1_ref, g2_ref,
                wout_ref, wq_ref, keys_ref, x1_ref, h2_ref, st_ref):
    wa = NA_WIDTH
    y = (jnp.dot(oa_ref[0], wout_ref[0:wa, :], preferred_element_type=F32)
         + jnp.dot(od_ref[0], wout_ref[wa:, :], preferred_element_type=F32))
    x1 = x_ref[0] + gt1_ref[0] * _rms(y, g1_ref[...])
    x1_ref[0] = x1
    h2 = _rms(x1, g2_ref[...]) * (1.0 + sc2_ref[0]) + sh2_ref[0]
    h2_ref[0] = h2
    qb = jnp.dot(h2.astype(BF16), wq_ref[...], preferred_element_type=F32).astype(BF16)
    for hp in range(2 * PEER_HEADS):
        st_ref[0, hp] = _dot_nt(keys_ref[hp], qb[:, hp * LANES:(hp + 1) * LANES])


def _out(oa, od, x, gt1, sh2, sc2, g1, g2, wout, wq, keys, *, tm):
    b, t, d = x.shape
    nq = wq.shape[1]
    nhp = keys.shape[0]
    mod = pl.BlockSpec((1, 1, d), lambda i, j: (i, 0, 0))
    gsp = pl.BlockSpec((1, d), lambda i, j: (0, 0))
    tok = pl.BlockSpec((1, tm, d), lambda i, j: (i, j, 0))
    half = pl.BlockSpec((1, tm, NA_WIDTH), lambda i, j: (i, j, 0))
    return pl.pallas_call(
        _out_kernel,
        grid=(b, t // tm),
        in_specs=[half, half, tok, mod, mod, mod, gsp, gsp,
                  pl.BlockSpec((d, d), lambda i, j: (0, 0)),
                  pl.BlockSpec((d, nq), lambda i, j: (0, 0)),
                  pl.BlockSpec((nhp, PEER_NKEYS, LANES), lambda i, j: (0, 0, 0))],
        out_specs=[tok, tok, pl.BlockSpec((1, nhp, PEER_NKEYS, tm), lambda i, j: (i, 0, 0, j))],
        out_shape=[jax.ShapeDtypeStruct((b, t, d), F32), jax.ShapeDtypeStruct((b, t, d), F32),
                   jax.ShapeDtypeStruct((b, nhp, PEER_NKEYS, t), F32)],
        compiler_params=pltpu.CompilerParams(vmem_limit_bytes=VMEM_LIMIT),
        name="out_proj_peer_scores",
    )(oa, od, x, gt1, sh2, sc2, g1, g2, wout, wq, keys)


def _top1(s, iota):
    m = jnp.max(s, axis=0, keepdims=True)
    idx = jnp.min(jnp.where(s == m, iota, s.shape[0]), axis=0, keepdims=True)
    return m, idx, jnp.where(iota == idx, -jnp.inf, s)


def _topk_kernel(st_ref, idx_ref, gate_ref):
    tl = st_ref.shape[-1]
    k = PEER_TOPK
    iota_n = lax.broadcasted_iota(I32, (PEER_NKEYS, tl), 0)
    iota_c = lax.broadcasted_iota(I32, (k * k, tl), 0)
    for h in range(PEER_HEADS):
        tops = []
        for p in range(2):
            s = st_ref[0, 2 * h + p]
            vals, idxs = [], []
            for _ in range(k):
                m, i, s = _top1(s, iota_n)
                vals.append(m)
                idxs.append(i)
            tops.append((jnp.concatenate(vals, axis=0), jnp.concatenate(idxs, axis=0)))
        (s0, i0), (s1, i1) = tops
        cand = jnp.concatenate([s0[a:a + 1] + s1 for a in range(k)], axis=0)
        cidx = jnp.concatenate([i0[a:a + 1] * PEER_NKEYS + i1 for a in range(k)], axis=0)
        best, experts = [], []
        for _ in range(k):
            m, pos, cand = _top1(cand, iota_c)
            best.append(m)
            experts.append(jnp.sum(jnp.where(iota_c == pos, cidx, 0), axis=0, keepdims=True))
        best = jnp.concatenate(best, axis=0)
        e = jnp.exp(best - best[0:1])
        gate_ref[0, h * k:(h + 1) * k, :] = e / jnp.sum(e, axis=0, keepdims=True)
        idx_ref[0, h * k:(h + 1) * k, :] = jnp.concatenate(experts, axis=0)


def _topk(st, *, tl):
    b, nhp, nkeys, t = st.shape
    nj = PEER_NJ
    out_spec = pl.BlockSpec((1, nj, tl), lambda i, j: (i, 0, j))
    return pl.pallas_call(
        _topk_kernel,
        grid=(b, t // tl),
        in_specs=[pl.BlockSpec((1, nhp, nkeys, tl), lambda i, j: (i, 0, 0, j))],
        out_specs=[out_spec, out_spec],
        out_shape=[jax.ShapeDtypeStruct((b, nj, t), I32), jax.ShapeDtypeStruct((b, nj, t), F32)],
        compiler_params=pltpu.CompilerParams(vmem_limit_bytes=VMEM_LIMIT),
        name="peer_topk",
    )(st)


def _pack_table(tab):
    e, d = tab.shape
    tb = tab.astype(BF16)
    lo = lax.bitcast_convert_type(tb[:, :d // 2], jnp.uint16).astype(jnp.uint32)
    hi = lax.bitcast_convert_type(tb[:, d // 2:], jnp.uint16).astype(jnp.uint32)
    words = lax.bitcast_convert_type((hi << 16) | lo, I32)
    return words.reshape(e * (d // 2 // LANES), LANES)


def _gather_rows(idx_ref, base, table_ref, tile_ref):
    for j in range(PEER_NJ):
        row = pl.multiple_of(idx_ref[base + j], 4)
        tile_ref[pl.ds(j, 4, stride=TILE_STRIDE), :] = table_ref[pl.ds(row, 4), :]


def _tile_chunk(tile_ref, c):
    return pltpu.bitcast(tile_ref[c * TILE_STRIDE:c * TILE_STRIDE + PEER_NJ, :], BF16)


def _split_bf16(v):
    head = v.astype(BF16).astype(F32)
    return jnp.concatenate([head, v - head], axis=0)


def _peer_u_kernel(idx_ref, x_ref, gate_ref, u_ref, w_ref, *tile_refs):
    nj = PEER_NJ
    nu = len(tile_refs)
    row16 = lax.broadcasted_iota(I32, (16, 1), 0) % 8
    even = (lax.broadcasted_iota(I32, (1, 2 * nj), 1) % 2) == 0

    def token(t, tile_ref):
        _gather_rows(idx_ref, pl.multiple_of(t * nj, nj), u_ref, tile_ref)
        x16 = _split_bf16(x_ref[t])
        acc = jnp.zeros((16, 2 * nj), F32)
        for c in range(4):
            lhs = jnp.where((row16 == c) | (row16 == 4 + c), x16, 0.0).astype(BF16)
            acc = acc + _dot_nt(lhs, _tile_chunk(tile_ref, c))
        a = acc[0:8] + acc[8:16]
        return jnp.sum(jnp.where(even, a, pltpu.roll(a, 4, axis=0))[0:4], axis=0, keepdims=True)

    def group(g, carry):
        t0 = pl.multiple_of(g * nu, nu)
        z = jnp.concatenate([token(t0 + i, tile_refs[i]) for i in range(nu)], axis=0)
        act = z + jnp.where(even, pltpu.roll(z, 2 * nj - 1, axis=1), pltpu.roll(z, 1, axis=1))
        gelu = 0.5 * act * (1.0 + lax.erf(act * (2.0 ** -0.5)))
        w_ref[pl.ds(t0, nu), :] = gate_ref[pl.ds(t0, nu), :] * gelu
        return carry

    lax.fori_loop(0, x_ref.shape[0] // nu, group, 0)


def _peer_u(idx4, x3, gate2, table):
    n = x3.shape[0]
    nj = PEER_NJ
    tm = PEER_TM
    return pl.pallas_call(
        _peer_u_kernel,
        grid=(n // tm,),
        in_specs=[pl.BlockSpec((tm * nj,), lambda i: (i,), memory_space=pltpu.SMEM),
                  pl.BlockSpec((tm, 8, LANES), lambda i: (i, 0, 0)),
                  pl.BlockSpec((tm, 2 * nj), lambda i: (i, 0)),
                  pl.BlockSpec(memory_space=pltpu.VMEM)],
        out_specs=pl.BlockSpec((tm, 2 * nj), lambda i: (i, 0)),
        out_shape=jax.ShapeDtypeStruct((n, 2 * nj), F32),
        scratch_shapes=[pltpu.VMEM((4 * TILE_STRIDE + 4, LANES), I32) for _ in range(PEER_UNROLL)],
        compiler_params=pltpu.CompilerParams(vmem_limit_bytes=VMEM_LIMIT),
        name="peer_u",
    )(idx4, x3, gate2, table)


def _peer_v_kernel(idx_ref, w_ref, v_ref, o_ref, *tile_refs):
    nj = PEER_NJ
    nu = len(tile_refs)
    row16 = lax.broadcasted_iota(I32, (16, 1), 0) % 8
    even = (lax.broadcasted_iota(I32, (1, 2 * nj), 1) % 2) == 0

    def token(t, tile_ref):
        _gather_rows(idx_ref, pl.multiple_of(t * nj, nj), v_ref, tile_ref)
        w16 = _split_bf16(jnp.broadcast_to(w_ref[pl.ds(t, 1), :], (8, 2 * nj)))
        acc = jnp.zeros((16, LANES), F32)
        for c in range(4):
            keep = ((row16 == c) & even) | ((row16 == 4 + c) & jnp.logical_not(even))
            lhs = jnp.where(keep, w16, 0.0).astype(BF16)
            acc = acc + jnp.dot(lhs, _tile_chunk(tile_ref, c), preferred_element_type=F32)
        o_ref[t] = acc[0:8] + acc[8:16]

    def group(g, carry):
        t0 = pl.multiple_of(g * nu, nu)
        for i in range(nu):
            token(t0 + i, tile_refs[i])
        return carry

    lax.fori_loop(0, o_ref.shape[0] // nu, group, 0)


def _peer_v(idx4, w2, table):
    n = w2.shape[0]
    nj = PEER_NJ
    tm = PEER_TM
    return pl.pallas_call(
        _peer_v_kernel,
        grid=(n // tm,),
        in_specs=[pl.BlockSpec((tm * nj,), lambda i: (i,), memory_space=pltpu.SMEM),
                  pl.BlockSpec((tm, 2 * nj), lambda i: (i, 0)),
                  pl.BlockSpec(memory_space=pltpu.VMEM)],
        out_specs=pl.BlockSpec((tm, 8, LANES), lambda i: (i, 0, 0)),
        out_shape=jax.ShapeDtypeStruct((n, 8, LANES), F32),
        scratch_shapes=[pltpu.VMEM((4 * TILE_STRIDE + 4, LANES), I32) for _ in range(PEER_UNROLL)],
        compiler_params=pltpu.CompilerParams(vmem_limit_bytes=VMEM_LIMIT),
        name="peer_v",
    )(idx4, w2, table)


def _final_kernel(x1_ref, y_ref, gt2_ref, g_ref, o_ref):
    o_ref[0] = x1_ref[0] + gt2_ref[0] * _rms(y_ref[0], g_ref[...])


def _final(x1, y, gt2, g, *, tm):
    b, t, d = x1.shape
    tok = pl.BlockSpec((1, tm, d), lambda i, j: (i, j, 0))
    return pl.pallas_call(
        _final_kernel,
        grid=(b, t // tm),
        in_specs=[tok, tok, pl.BlockSpec((1, 1, d), lambda i, j: (i, 0, 0)),
                  pl.BlockSpec((1, d), lambda i, j: (0, 0))],
        out_specs=tok,
        out_shape=jax.ShapeDtypeStruct((b, t, d), F32),
        compiler_params=pltpu.CompilerParams(vmem_limit_bytes=VMEM_LIMIT),
        name="final_residual",
    )(x1, y, gt2, g)


def _rope_tables(t):
    tok = jnp.arange(t)
    row = (tok // GRID_W).astype(F32)
    col = (tok % GRID_W).astype(F32)
    n_freq = DIFF_QK // 4
    inv = ROPE_THETA ** (-jnp.arange(n_freq, dtype=F32) / n_freq)
    ang = jnp.concatenate([row[:, None] * inv, col[:, None] * inv], axis=-1)
    ang2 = jnp.tile(jnp.repeat(ang, 2, axis=1), (1, LANES // DIFF_QK))
    sign = jnp.where(jnp.arange(LANES) % 2 == 0, -1.0, 1.0).astype(F32)
    return jnp.cos(ang2), jnp.sin(ang2) * sign


def _head_major_cols(w):
    d = w.shape[0]
    return w.reshape(d, 2, DIFF_HEADS, DIFF_QK).transpose(0, 2, 1, 3).reshape(d, -1)


def kernel(x, c, ctx, c_ctx, w_ada, b_ada, g_norm, w_in, na_rel_bias, diff_lambda, diff_subln,
           w_out, peer_wq, peer_keys, peer_u, peer_v):
    assert w_ada.shape[0] == 1, "single-layer kernel"
    b, t, d = x.shape
    n_ctx = ctx.shape[1]
    rows = t // GRID_W
    assert t % (NA_QROWS * GRID_W) == 0 and rows >= 3 * NA_QROWS

    pad = (-(b + 1)) % 8
    cc = jnp.concatenate([c, c_ctx[None], jnp.zeros((pad, d), F32)], axis=0)
    mod = _ada(cc, w_ada[0], b_ada[0][None]).reshape(b + 1 + pad, N_MOD, d)
    lat = lambda i: mod[:b, i][:, None, :]
    cxm = lambda i: mod[b:b + 1, i][:, None, :]

    gw = NA_WIDTH
    w_in0 = w_in[0]
    w_perm = jnp.concatenate([w_in0[:, :3 * gw], _head_major_cols(w_in0[:, 3 * gw:4 * gw]),
                              _head_major_cols(w_in0[:, 4 * gw:5 * gw]), w_in0[:, 5 * gw:]], axis=1).astype(BF16)
    cos2, sin2 = _rope_tables(t)
    g0 = g_norm[0, 0][None]
    qa, ka, va, qd, kd, vd = _qkv(x, lat(0), lat(1), g0, w_perm, cos2, sin2,
                                  rope=True, per_batch_mod=True, tm=512)
    _, ka_c, va_c, _, kd_c, vd_c = _qkv(ctx, cxm(0), cxm(1), g0, w_perm, cos2[:n_ctx], sin2[:n_ctx],
                                        rope=False, per_batch_mod=False, tm=n_ctx)

    out_a = _na(qa, ka, va, ka_c, va_c, _na_bias_table(na_rel_bias[0], rows))
    out_d = _diff(diff_lambda[0], qd, kd, vd, kd_c, vd_c, diff_subln[0][None], tq=256, tk=512)

    keys = peer_keys[0].reshape(2 * PEER_HEADS, PEER_NKEYS, -1).astype(BF16)
    x1, h2, st = _out(out_a, out_d, x, lat(2), lat(3), lat(4), g_norm[0, 1][None], g_norm[0, 2][None],
                      w_out[0].astype(BF16), peer_wq[0].astype(BF16), keys, tm=256)

    idx_t, gate_t = _topk(st, tl=256)
    idx4 = (idx_t.transpose(0, 2, 1) * 4).reshape(-1)
    gate2 = jnp.repeat(gate_t.transpose(0, 2, 1).reshape(b * t, PEER_NJ), 2, axis=1)

    w2 = _peer_u(idx4, h2.reshape(b * t, 8, LANES), gate2, _pack_table(peer_u[0]))
    y = _peer_v(idx4, w2, _pack_table(peer_v[0])).reshape(b, t, d)
    return _final(x1, y, lat(5), g_norm[0, 3][None], tm=512)
```

```python
import functools
import math

import numpy as np
import jax
import jax.numpy as jnp
from jax import lax
from jax.experimental import pallas as pl
from jax.experimental.pallas import tpu as pltpu

F32 = jnp.float32
BF16 = jnp.bfloat16
I32 = jnp.int32

GRID_W = 64
HEAD_DIM = 64
NA_HEADS = 8
NA_WIDTH = NA_HEADS * HEAD_DIM
NA_WIN_R = 8
NA_WIN_C = 16
DIFF_HEADS = 4
DIFF_QK = HEAD_DIM
DIFF_V = 2 * DIFF_QK
ROPE_THETA = 10000.0
PEER_HEADS = 8
PEER_NKEYS = 128
PEER_TOPK = 16
N_MOD = 6
EPS = 1e-6
LAMBDA_INIT = 0.8 - 0.6 * math.exp(-0.3 * 0)
LOG2E = math.log2(math.e)
DIFF_TQ = 256
DIFF_TK = 512

LANES = 128
VMEM_LIMIT = 56 * 1024 * 1024
NEG = -1e30

NA_QROWS = 8
NA_KROWS = 16
NA_QSUB = 128
PEER_TM = 64
PEER_NJ = PEER_HEADS * PEER_TOPK
TILE_STRIDE = PEER_NJ + 1
PEER_UNROLL = 8


def _dot_nt(a, b):
    return lax.dot_general(a, b, (((1,), (1,)), ((), ())), preferred_element_type=F32)


def _rms(x, g):
    return x * lax.rsqrt(jnp.mean(x * x, axis=-1, keepdims=True) + EPS) * g


def _ada_kernel(c_ref, w_ref, b_ref, o_ref):
    c = c_ref[...]
    s = c / (1.0 + jnp.exp(-c))
    o_ref[...] = jnp.dot(s, w_ref[...], precision=lax.Precision.HIGHEST,
                         preferred_element_type=F32) + b_ref[...]


def _ada(cc, w, b):
    m, d = cc.shape
    n = w.shape[1]
    tn = 1024
    return pl.pallas_call(
        _ada_kernel,
        grid=(n // tn,),
        in_specs=[pl.BlockSpec((m, d), lambda j: (0, 0)),
                  pl.BlockSpec((d, tn), lambda j: (0, j)),
                  pl.BlockSpec((1, tn), lambda j: (0, j))],
        out_specs=pl.BlockSpec((m, tn), lambda j: (0, j)),
        out_shape=jax.ShapeDtypeStruct((m, n), F32),
        compiler_params=pltpu.CompilerParams(vmem_limit_bytes=VMEM_LIMIT),
        name="ada",
    )(cc, w, b)


def _qkv_kernel(x_ref, sh_ref, sc_ref, g_ref, w_ref, cos_ref, sin_ref,
                qa_ref, ka_ref, va_ref, qd_ref, kd_ref, vd_ref, *, rope):
    x = x_ref[0]
    h = _rms(x, g_ref[...]) * (1.0 + sc_ref[0]) + sh_ref[0]
    hb = h.astype(BF16)
    gw = NA_WIDTH
    scale = HEAD_DIM ** -0.5

    def proj(g):
        return jnp.dot(hb, w_ref[:, g * gw:(g + 1) * gw], preferred_element_type=F32)

    def roped(p):
        if not rope:
            return p
        cos2 = cos_ref[...]
        sin2 = sin_ref[...]
        even = (lax.broadcasted_iota(I32, cos2.shape, 1) % 2) == 0
        outs = []
        for c in range(gw // LANES):
            v = p[:, c * LANES:(c + 1) * LANES]
            nxt = pltpu.roll(v, LANES - 1, axis=1)
            prv = pltpu.roll(v, 1, axis=1)
            outs.append(v * cos2 + jnp.where(even, nxt, prv) * sin2)
        return jnp.concatenate(outs, axis=1)

    qa_ref[0] = (proj(0) * scale).astype(BF16)
    ka_ref[0] = proj(1).astype(BF16)
    va_ref[0] = proj(2).astype(BF16)
    qd_ref[0] = (roped(proj(3)) * (scale * LOG2E)).astype(BF16)
    kd_ref[0] = roped(proj(4)).astype(BF16)
    vd = proj(5).astype(BF16)
    ones = jnp.ones((vd.shape[0], DIFF_V), BF16)
    pieces = []
    for hd in range(DIFF_HEADS):
        pieces += [vd[:, hd * DIFF_V:(hd + 1) * DIFF_V], ones]
    vd_ref[0] = jnp.concatenate(pieces, axis=1)


def _qkv(x, sh, sc, g, w, cos2, sin2, *, rope, per_batch_mod, tm):
    b, l, d = x.shape
    n = w.shape[1]
    gw = NA_WIDTH
    mod_map = (lambda i, j: (i, 0, 0)) if per_batch_mod else (lambda i, j: (0, 0, 0))
    out_sds = jax.ShapeDtypeStruct((b, l, gw), BF16)
    out_spec = pl.BlockSpec((1, tm, gw), lambda i, j: (i, j, 0))
    return pl.pallas_call(
        functools.partial(_qkv_kernel, rope=rope),
        grid=(b, l // tm),
        in_specs=[pl.BlockSpec((1, tm, d), lambda i, j: (i, j, 0)),
                  pl.BlockSpec((1, 1, d), mod_map),
                  pl.BlockSpec((1, 1, d), mod_map),
                  pl.BlockSpec((1, d), lambda i, j: (0, 0)),
                  pl.BlockSpec((d, n), lambda i, j: (0, 0)),
                  pl.BlockSpec((tm, LANES), lambda i, j: (j, 0)),
                  pl.BlockSpec((tm, LANES), lambda i, j: (j, 0))],
        out_specs=[out_spec] * 5 + [pl.BlockSpec((1, tm, 2 * gw), lambda i, j: (i, j, 0))],
        out_shape=[out_sds] * 5 + [jax.ShapeDtypeStruct((b, l, 2 * gw), BF16)],
        compiler_params=pltpu.CompilerParams(vmem_limit_bytes=VMEM_LIMIT),
        name="qkv_rope" if rope else "qkv_ctx",
    )(x, sh, sc, g, w, cos2, sin2)


def _na_kernel(q_ref, k_ref, v_ref, kc_ref, vc_ref, bias_ref, o_ref, *, rows):
    rb = pl.program_id(2)
    ks = jnp.clip(rb * NA_QROWS - NA_WIN_R // 2, 0, rows - NA_KROWS)
    start = pl.multiple_of(ks * GRID_W, GRID_W)
    nk = NA_KROWS * GRID_W
    kw = k_ref[0, pl.ds(start, nk), :]
    vw = v_ref[0, pl.ds(start, nk), :]
    kc = kc_ref[0]
    vc = vc_ref[0]
    lane = lax.broadcasted_iota(I32, (1, LANES), 1)
    for s in range(NA_QROWS * GRID_W // NA_QSUB):
        sl = slice(s * NA_QSUB, (s + 1) * NA_QSUB)
        q = q_ref[0, sl, :]
        outs = []
        for hh in range(2):
            qh = jnp.where((lane // HEAD_DIM) == hh, q, jnp.zeros_like(q))
            s_loc = _dot_nt(qh, kw) + bias_ref[0, hh, sl, :]
            s_ctx = _dot_nt(qh, kc)
            mx = jnp.maximum(jnp.max(s_loc, axis=-1, keepdims=True),
                             jnp.max(s_ctx, axis=-1, keepdims=True))
            p_loc = jnp.exp(s_loc - mx)
            p_ctx = jnp.exp(s_ctx - mx)
            l = jnp.sum(p_loc, axis=-1, keepdims=True) + jnp.sum(p_ctx, axis=-1, keepdims=True)
            o = (jnp.dot(p_loc.astype(BF16), vw, preferred_element_type=F32)
                 + jnp.dot(p_ctx.astype(BF16), vc, preferred_element_type=F32))
            outs.append(o / l)
        o_ref[0, sl, :] = jnp.where(lane < HEAD_DIM, outs[0], outs[1]).astype(BF16)


def _na_bias_table(rel_bias, rows):
    nrb = rows // NA_QROWS
    n_ro = 2 * NA_WIN_R - 1
    qc = np.arange(GRID_W)[:, None]
    kc = np.arange(GRID_W)[None, :]
    c_start = np.clip(qc - NA_WIN_C // 2, 0, GRID_W - NA_WIN_C)
    valid_c = (kc >= c_start) & (kc < c_start + NA_WIN_C)
    co = np.clip(kc - qc + NA_WIN_C - 1, 0, 2 * NA_WIN_C - 2)
    tz = jnp.take(rel_bias, jnp.asarray(co.reshape(-1), I32), axis=2).reshape(NA_HEADS, n_ro, GRID_W, GRID_W)
    tz = jnp.where(jnp.asarray(valid_c)[None, None], tz, NEG)
    tz = jnp.concatenate([tz, jnp.full((NA_HEADS, 1, GRID_W, GRID_W), NEG, F32)], axis=1)
    sel = []
    for rb in (0, 1, nrb - 1):
        r0 = rb * NA_QROWS
        ks = int(np.clip(r0 - NA_WIN_R // 2, 0, rows - NA_KROWS))
        qr = r0 + np.arange(NA_QROWS)[:, None]
        kr = ks + np.arange(NA_KROWS)[None, :]
        r_start = np.clip(qr - NA_WIN_R // 2, 0, rows - NA_WIN_R)
        valid_r = (kr >= r_start) & (kr < r_start + NA_WIN_R)
        sel.append(np.where(valid_r, kr - qr + NA_WIN_R - 1, n_ro))
    sel = np.stack(sel).reshape(-1)
    blocks = jnp.take(tz, jnp.asarray(sel, I32), axis=1)
    blocks = blocks.reshape(NA_HEADS, 3, NA_QROWS, NA_KROWS, GRID_W, GRID_W)
    return blocks.transpose(1, 0, 2, 4, 3, 5).reshape(3, NA_HEADS, NA_QROWS * GRID_W, NA_KROWS * GRID_W)


def _na(qa, ka, va, ka_c, va_c, bias):
    b, t, _ = qa.shape
    c = ka_c.shape[1]
    rows = t // GRID_W
    nrb = rows // NA_QROWS
    tq = NA_QROWS * GRID_W
    nk = NA_KROWS * GRID_W

    def bias_map(i, hp, rb):
        pat = jnp.where(rb == 0, 0, jnp.where(rb == nrb - 1, 2, 1))
        return (pat, hp, 0, 0)

    return pl.pallas_call(
        functools.partial(_na_kernel, rows=rows),
        grid=(b, NA_HEADS // 2, nrb),
        in_specs=[pl.BlockSpec((1, tq, LANES), lambda i, hp, rb: (i, rb, hp)),
                  pl.BlockSpec((1, t, LANES), lambda i, hp, rb: (i, 0, hp)),
                  pl.BlockSpec((1, t, LANES), lambda i, hp, rb: (i, 0, hp)),
                  pl.BlockSpec((1, c, LANES), lambda i, hp, rb: (i, 0, hp)),
                  pl.BlockSpec((1, c, LANES), lambda i, hp, rb: (i, 0, hp)),
                  pl.BlockSpec((1, 2, tq, nk), bias_map)],
        out_specs=pl.BlockSpec((1, tq, LANES), lambda i, hp, rb: (i, rb, hp)),
        out_shape=jax.ShapeDtypeStruct((b, t, NA_WIDTH), BF16),
        compiler_params=pltpu.CompilerParams(vmem_limit_bytes=VMEM_LIMIT),
        name="na_attn",
    )(qa, ka, va, ka_c, va_c, bias)


def _diff_kernel(lam_ref, q_ref, k_ref, v_ref, kc_ref, vc_ref, g_ref, o_ref, *, tk):
    q = q_ref[0]
    tq = q.shape[0]
    t = k_ref.shape[1]
    lane = lax.broadcasted_iota(I32, (1, LANES), 1)
    zero = jnp.zeros_like(q)
    qs = (jnp.where(lane < DIFF_QK, q, zero), jnp.where(lane >= DIFF_QK, q, zero))

    def update(kb, vb, carry):
        new = []
        for mi in range(2):
            m, a = carry[mi]
            s = _dot_nt(qs[mi], kb)
            mn = jnp.maximum(m, jnp.max(s, axis=-1, keepdims=True))
            p = jnp.exp2(s - mn)
            a = jnp.exp2(m - mn) * a + jnp.dot(p.astype(BF16), vb, preferred_element_type=F32)
            new.append((mn, a))
        return tuple(new)

    init = tuple((jnp.full((tq, 1), NEG, F32), jnp.zeros((tq, 2 * DIFF_V), F32)) for _ in range(2))
    carry = update(kc_ref[0], vc_ref[0], init)
    for i in range(t // tk):
        carry = update(k_ref[0, i * tk:(i + 1) * tk, :], v_ref[0, i * tk:(i + 1) * tk, :], carry)
    lp = lam_ref[...]
    lam = (jnp.exp(jnp.sum(lp[0:1] * lp[1:2], axis=-1, keepdims=True))
           - jnp.exp(jnp.sum(lp[2:3] * lp[3:4], axis=-1, keepdims=True)) + LAMBDA_INIT)
    (_, a0), (_, a1) = carry
    out = (a0[:, :DIFF_V] / a0[:, DIFF_V:DIFF_V + 1]
           - lam * (a1[:, :DIFF_V] / a1[:, DIFF_V:DIFF_V + 1]))
    o_ref[0] = (_rms(out, g_ref[...]) * (1.0 - LAMBDA_INIT)).astype(BF16)


def _diff(lam_params, qd, kd, vd, kd_c, vd_c, subln, *, tq, tk):
    b, t, w = qd.shape
    c = kd_c.shape[1]
    return pl.pallas_call(
        functools.partial(_diff_kernel, tk=tk),
        grid=(b, DIFF_HEADS, t // tq),
        in_specs=[pl.BlockSpec((4, DIFF_QK), lambda i, h, j: (0, 0)),
                  pl.BlockSpec((1, tq, LANES), lambda i, h, j: (i, j, h)),
                  pl.BlockSpec((1, t, LANES), lambda i, h, j: (i, 0, h)),
                  pl.BlockSpec((1, t, 2 * DIFF_V), lambda i, h, j: (i, 0, h)),
                  pl.BlockSpec((1, c, LANES), lambda i, h, j: (i, 0, h)),
                  pl.BlockSpec((1, c, 2 * DIFF_V), lambda i, h, j: (i, 0, h)),
                  pl.BlockSpec((1, DIFF_V), lambda i, h, j: (0, 0))],
        out_specs=pl.BlockSpec((1, tq, LANES), lambda i, h, j: (i, j, h)),
        out_shape=jax.ShapeDtypeStruct((b, t, w), BF16),
        compiler_params=pltpu.CompilerParams(vmem_limit_bytes=VMEM_LIMIT),
        name="diff_attn",
    )(lam_params, qd, kd, vd, kd_c, vd_c, subln)


def _out_kernel(oa_ref, od_ref, x_ref, gt1_ref, sh2_ref, sc2_ref, g1_ref, g2_ref,
                wout_ref, wq_ref, keys_ref, x1_ref, h2_ref, st_ref):
    wa = NA_WIDTH
    y = (jnp.dot(oa_ref[0], wout_ref[0:wa, :], preferred_element_type=F32)
         + jnp.dot(od_ref[0], wout_ref[wa:, :], preferred_element_type=F32))
    x1 = x_ref[0] + gt1_ref[0] * _rms(y, g1_ref[...])
    x1_ref[0] = x1
    h2 = _rms(x1, g2_ref[...]) * (1.0 + sc2_ref[0]) + sh2_ref[0]
    h2_ref[0] = h2
    qb = jnp.dot(h2.astype(BF16), wq_ref[...], preferred_element_type=F32).astype(BF16)
    for hp in range(2 * PEER_HEADS):
        st_ref[0, hp] = _dot_nt(keys_ref[hp], qb[:, hp * LANES:(hp + 1) * LANES])


def _out(oa, od, x, gt1, sh2, sc2, g1, g2, wout, wq, keys, *, tm):
    b, t, d = x.shape
    nq = wq.shape[1]
    nhp = keys.shape[0]
    mod = pl.BlockSpec((1, 1, d), lambda i, j: (i, 0, 0))
    gsp = pl.BlockSpec((1, d), lambda i, j: (0, 0))
    tok = pl.BlockSpec((1, tm, d), lambda i, j: (i, j, 0))
    half = pl.BlockSpec((1, tm, NA_WIDTH), lambda i, j: (i, j, 0))
    return pl.pallas_call(
        _out_kernel,
        grid=(b, t // tm),
        in_specs=[half, half, tok, mod, mod, mod, gsp, gsp,
                  pl.BlockSpec((d, d), lambda i, j: (0, 0)),
                  pl.BlockSpec((d, nq), lambda i, j: (0, 0)),
                  pl.BlockSpec((nhp, PEER_NKEYS, LANES), lambda i, j: (0, 0, 0))],
        out_specs=[tok, tok, pl.BlockSpec((1, nhp, PEER_NKEYS, tm), lambda i, j: (i, 0, 0, j))],
        out_shape=[jax.ShapeDtypeStruct((b, t, d), F32), jax.ShapeDtypeStruct((b, t, d), F32),
                   jax.ShapeDtypeStruct((b, nhp, PEER_NKEYS, t), F32)],
        compiler_params=pltpu.CompilerParams(vmem_limit_bytes=VMEM_LIMIT),
        name="out_proj_peer_scores",
    )(oa, od, x, gt1, sh2, sc2, g1, g2, wout, wq, keys)


def _top1(s, iota):
    m = jnp.max(s, axis=0, keepdims=True)
    idx = jnp.min(jnp.where(s == m, iota, s.shape[0]), axis=0, keepdims=True)
    return m, idx, jnp.where(iota == idx, -jnp.inf, s)


def _topk_kernel(st_ref, idx_ref, gate_ref):
    tl = st_ref.shape[-1]
    k = PEER_TOPK
    iota_n = lax.broadcasted_iota(I32, (PEER_NKEYS, tl), 0)
    widths = [k // (a + 1) for a in range(k)]
    n_cand = -(-sum(widths) // 8) * 8
    iota_c = lax.broadcasted_iota(I32, (n_cand, tl), 0)
    pad = n_cand - sum(widths)
    for h in range(PEER_HEADS):
        tops = []
        for p in range(2):
            s = st_ref[0, 2 * h + p]
            vals, idxs = [], []
            for _ in range(k):
                m, i, s = _top1(s, iota_n)
                vals.append(m)
                idxs.append(i)
            tops.append((jnp.concatenate(vals, axis=0), jnp.concatenate(idxs, axis=0)))
        (s0, i0), (s1, i1) = tops
        cand = jnp.concatenate([s0[a:a + 1] + s1[0:widths[a]] for a in range(k)]
                               + [jnp.full((pad, tl), -jnp.inf, F32)], axis=0)
        cidx = jnp.concatenate([i0[a:a + 1] * PEER_NKEYS + i1[0:widths[a]] for a in range(k)]
                               + [jnp.zeros((pad, tl), I32)], axis=0)
        best, experts = [], []
        for _ in range(k):
            m, pos, cand = _top1(cand, iota_c)
            best.append(m)
            experts.append(jnp.sum(jnp.where(iota_c == pos, cidx, 0), axis=0, keepdims=True))
        best = jnp.concatenate(best, axis=0)
        e = jnp.exp(best - best[0:1])
        gate_ref[0, h * k:(h + 1) * k, :] = e / jnp.sum(e, axis=0, keepdims=True)
        idx_ref[0, h * k:(h + 1) * k, :] = jnp.concatenate(experts, axis=0)


def _topk(st, *, tl):
    b, nhp, nkeys, t = st.shape
    nj = PEER_NJ
    out_spec = pl.BlockSpec((1, nj, tl), lambda i, j: (i, 0, j))
    return pl.pallas_call(
        _topk_kernel,
        grid=(b, t // tl),
        in_specs=[pl.BlockSpec((1, nhp, nkeys, tl), lambda i, j: (i, 0, 0, j))],
        out_specs=[out_spec, out_spec],
        out_shape=[jax.ShapeDtypeStruct((b, nj, t), I32), jax.ShapeDtypeStruct((b, nj, t), F32)],
        compiler_params=pltpu.CompilerParams(vmem_limit_bytes=VMEM_LIMIT),
        name="peer_topk",
    )(st)


def _pack_table(tab):
    e, d = tab.shape
    tb = tab.astype(BF16)
    lo = lax.bitcast_convert_type(tb[:, :d // 2], jnp.uint16).astype(jnp.uint32)
    hi = lax.bitcast_convert_type(tb[:, d // 2:], jnp.uint16).astype(jnp.uint32)
    words = lax.bitcast_convert_type((hi << 16) | lo, I32)
    return words.reshape(e * (d // 2 // LANES), LANES)


def _gather_rows(idx_ref, base, table_ref, tile_ref):
    for j in range(PEER_NJ):
        row = pl.multiple_of(idx_ref[base + j], 4)
        tile_ref[pl.ds(j, 4, stride=TILE_STRIDE), :] = table_ref[pl.ds(row, 4), :]


def _tile_chunk(tile_ref, c):
    return pltpu.bitcast(tile_ref[c * TILE_STRIDE:c * TILE_STRIDE + PEER_NJ, :], BF16)


def _split_bf16(v):
    head = v.astype(BF16).astype(F32)
    return jnp.concatenate([head, v - head], axis=0)


def _peer_u_kernel(idx_ref, x_ref, gate_ref, u_ref, w_ref, *tile_refs):
    nj = PEER_NJ
    nu = len(tile_refs)
    row16 = lax.broadcasted_iota(I32, (16, 1), 0) % 8
    even = (lax.broadcasted_iota(I32, (1, 2 * nj), 1) % 2) == 0

    def token(t, tile_ref):
        _gather_rows(idx_ref, pl.multiple_of(t * nj, nj), u_ref, tile_ref)
        x16 = _split_bf16(x_ref[t])
        acc = jnp.zeros((16, 2 * nj), F32)
        for c in range(4):
            lhs = jnp.where((row16 == c) | (row16 == 4 + c), x16, 0.0).astype(BF16)
            acc = acc + _dot_nt(lhs, _tile_chunk(tile_ref, c))
        a = acc[0:8] + acc[8:16]
        return jnp.sum(jnp.where(even, a, pltpu.roll(a, 4, axis=0))[0:4], axis=0, keepdims=True)

    def group(g, carry):
        t0 = pl.multiple_of(g * nu, nu)
        z = jnp.concatenate([token(t0 + i, tile_refs[i]) for i in range(nu)], axis=0)
        act = z + jnp.where(even, pltpu.roll(z, 2 * nj - 1, axis=1), pltpu.roll(z, 1, axis=1))
        gelu = 0.5 * act * (1.0 + lax.erf(act * (2.0 ** -0.5)))
        w_ref[pl.ds(t0, nu), :] = gate_ref[pl.ds(t0, nu), :] * gelu
        return carry

    lax.fori_loop(0, x_ref.shape[0] // nu, group, 0)


def _peer_u(idx4, x3, gate2, table):
    n = x3.shape[0]
    nj = PEER_NJ
    tm = PEER_TM
    return pl.pallas_call(
        _peer_u_kernel,
        grid=(n // tm,),
        in_specs=[pl.BlockSpec((tm * nj,), lambda i: (i,), memory_space=pltpu.SMEM),
                  pl.BlockSpec((tm, 8, LANES), lambda i: (i, 0, 0)),
                  pl.BlockSpec((tm, 2 * nj), lambda i: (i, 0)),
                  pl.BlockSpec(memory_space=pltpu.VMEM)],
        out_specs=pl.BlockSpec((tm, 2 * nj), lambda i: (i, 0)),
        out_shape=jax.ShapeDtypeStruct((n, 2 * nj), F32),
        scratch_shapes=[pltpu.VMEM((4 * TILE_STRIDE + 4, LANES), I32) for _ in range(PEER_UNROLL)],
        compiler_params=pltpu.CompilerParams(vmem_limit_bytes=VMEM_LIMIT),
        name="peer_u",
    )(idx4, x3, gate2, table)


def _peer_v_kernel(idx_ref, w_ref, v_ref, o_ref, *tile_refs):
    nj = PEER_NJ
    nu = len(tile_refs)
    row16 = lax.broadcasted_iota(I32, (16, 1), 0) % 8
    even = (lax.broadcasted_iota(I32, (1, 2 * nj), 1) % 2) == 0

    def token(t, tile_ref):
        _gather_rows(idx_ref, pl.multiple_of(t * nj, nj), v_ref, tile_ref)
        w16 = _split_bf16(jnp.broadcast_to(w_ref[pl.ds(t, 1), :], (8, 2 * nj)))
        acc = jnp.zeros((16, LANES), F32)
        for c in range(4):
            keep = ((row16 == c) & even) | ((row16 == 4 + c) & jnp.logical_not(even))
            lhs = jnp.where(keep, w16, 0.0).astype(BF16)
            acc = acc + jnp.dot(lhs, _tile_chunk(tile_ref, c), preferred_element_type=F32)
        o_ref[t] = acc[0:8] + acc[8:16]

    def group(g, carry):
        t0 = pl.multiple_of(g * nu, nu)
        for i in range(nu):
            token(t0 + i, tile_refs[i])
        return carry

    lax.fori_loop(0, o_ref.shape[0] // nu, group, 0)


def _peer_v(idx4, w2, table):
    n = w2.shape[0]
    nj = PEER_NJ
    tm = PEER_TM
    return pl.pallas_call(
        _peer_v_kernel,
        grid=(n // tm,),
        in_specs=[pl.BlockSpec((tm * nj,), lambda i: (i,), memory_space=pltpu.SMEM),
                  pl.BlockSpec((tm, 2 * nj), lambda i: (i, 0)),
                  pl.BlockSpec(memory_space=pltpu.VMEM)],
        out_specs=pl.BlockSpec((tm, 8, LANES), lambda i: (i, 0, 0)),
        out_shape=jax.ShapeDtypeStruct((n, 8, LANES), F32),
        scratch_shapes=[pltpu.VMEM((4 * TILE_STRIDE + 4, LANES), I32) for _ in range(PEER_UNROLL)],
        compiler_params=pltpu.CompilerParams(vmem_limit_bytes=VMEM_LIMIT),
        name="peer_v",
    )(idx4, w2, table)


def _final_kernel(x1_ref, y_ref, gt2_ref, g_ref, o_ref):
    o_ref[0] = x1_ref[0] + gt2_ref[0] * _rms(y_ref[0], g_ref[...])


def _final(x1, y, gt2, g, *, tm):
    b, t, d = x1.shape
    tok = pl.BlockSpec((1, tm, d), lambda i, j: (i, j, 0))
    return pl.pallas_call(
        _final_kernel,
        grid=(b, t // tm),
        in_specs=[tok, tok, pl.BlockSpec((1, 1, d), lambda i, j: (i, 0, 0)),
                  pl.BlockSpec((1, d), lambda i, j: (0, 0))],
        out_specs=tok,
        out_shape=jax.ShapeDtypeStruct((b, t, d), F32),
        compiler_params=pltpu.CompilerParams(vmem_limit_bytes=VMEM_LIMIT),
        name="final_residual",
    )(x1, y, gt2, g)


def _rope_tables(t):
    tok = jnp.arange(t)
    row = (tok // GRID_W).astype(F32)
    col = (tok % GRID_W).astype(F32)
    n_freq = DIFF_QK // 4
    inv = ROPE_THETA ** (-jnp.arange(n_freq, dtype=F32) / n_freq)
    ang = jnp.concatenate([row[:, None] * inv, col[:, None] * inv], axis=-1)
    ang2 = jnp.tile(jnp.repeat(ang, 2, axis=1), (1, LANES // DIFF_QK))
    sign = jnp.where(jnp.arange(LANES) % 2 == 0, -1.0, 1.0).astype(F32)
    return jnp.cos(ang2), jnp.sin(ang2) * sign


def _head_major_cols(w):
    d = w.shape[0]
    return w.reshape(d, 2, DIFF_HEADS, DIFF_QK).transpose(0, 2, 1, 3).reshape(d, -1)


def kernel(x, c, ctx, c_ctx, w_ada, b_ada, g_norm, w_in, na_rel_bias, diff_lambda, diff_subln,
           w_out, peer_wq, peer_keys, peer_u, peer_v):
    assert w_ada.shape[0] == 1, "single-layer kernel"
    b, t, d = x.shape
    n_ctx = ctx.shape[1]
    rows = t // GRID_W
    assert t % (NA_QROWS * GRID_W) == 0 and rows >= 3 * NA_QROWS

    pad = (-(b + 1)) % 8
    cc = jnp.concatenate([c, c_ctx[None], jnp.zeros((pad, d), F32)], axis=0)
    mod = _ada(cc, w_ada[0], b_ada[0][None]).reshape(b + 1 + pad, N_MOD, d)
    lat = lambda i: mod[:b, i][:, None, :]
    cxm = lambda i: mod[b:b + 1, i][:, None, :]

    gw = NA_WIDTH
    w_in0 = w_in[0]
    w_perm = jnp.concatenate([w_in0[:, :3 * gw], _head_major_cols(w_in0[:, 3 * gw:4 * gw]),
                              _head_major_cols(w_in0[:, 4 * gw:5 * gw]), w_in0[:, 5 * gw:]], axis=1).astype(BF16)
    cos2, sin2 = _rope_tables(t)
    g0 = g_norm[0, 0][None]
    qa, ka, va, qd, kd, vd = _qkv(x, lat(0), lat(1), g0, w_perm, cos2, sin2,
                                  rope=True, per_batch_mod=True, tm=512)
    _, ka_c, va_c, _, kd_c, vd_c = _qkv(ctx, cxm(0), cxm(1), g0, w_perm, cos2[:n_ctx], sin2[:n_ctx],
                                        rope=False, per_batch_mod=False, tm=n_ctx)

    out_a = _na(qa, ka, va, ka_c, va_c, _na_bias_table(na_rel_bias[0], rows))
    out_d = _diff(diff_lambda[0], qd, kd, vd, kd_c, vd_c, diff_subln[0][None], tq=DIFF_TQ, tk=DIFF_TK)

    keys = peer_keys[0].reshape(2 * PEER_HEADS, PEER_NKEYS, -1).astype(BF16)
    x1, h2, st = _out(out_a, out_d, x, lat(2), lat(3), lat(4), g_norm[0, 1][None], g_norm[0, 2][None],
                      w_out[0].astype(BF16), peer_wq[0].astype(BF16), keys, tm=256)

    idx_t, gate_t = _topk(st, tl=256)
    idx4 = (idx_t.transpose(0, 2, 1) * 4).reshape(-1)
    gate2 = jnp.repeat(gate_t.transpose(0, 2, 1).reshape(b * t, PEER_NJ), 2, axis=1)

    w2 = _peer_u(idx4, h2.reshape(b * t, 8, LANES), gate2, _pack_table(peer_u[0]))
    y = _peer_v(idx4, w2, _pack_table(peer_v[0])).reshape(b, t, d)
    return _final(x1, y, lat(5), g_norm[0, 3][None], tm=512)
```

```python
import functools
import math

import numpy as np
import jax
import jax.numpy as jnp
from jax import lax
from jax.experimental import pallas as pl
from jax.experimental.pallas import tpu as pltpu

F32 = jnp.float32
BF16 = jnp.bfloat16
I32 = jnp.int32

GRID_W = 64
HEAD_DIM = 64
NA_HEADS = 8
NA_WIDTH = NA_HEADS * HEAD_DIM
NA_WIN_R = 8
NA_WIN_C = 16
DIFF_HEADS = 4
DIFF_QK = HEAD_DIM
DIFF_V = 2 * DIFF_QK
ROPE_THETA = 10000.0
PEER_HEADS = 8
PEER_NKEYS = 128
PEER_TOPK = 16
N_MOD = 6
EPS = 1e-6
LAMBDA_INIT = 0.8 - 0.6 * math.exp(-0.3 * 0)
LOG2E = math.log2(math.e)
DIFF_TQ = 256
DIFF_TK = 512

LANES = 128
VMEM_LIMIT = 56 * 1024 * 1024
NEG = -1e30

NA_QROWS = 8
NA_KROWS = 16
NA_QSUB = 128
PEER_TM = 64
PEER_NJ = PEER_HEADS * PEER_TOPK
TILE_STRIDE = PEER_NJ + 1
PEER_UNROLL = 8


def _dot_nt(a, b):
    return lax.dot_general(a, b, (((1,), (1,)), ((), ())), preferred_element_type=F32)


def _rms(x, g):
    return x * lax.rsqrt(jnp.mean(x * x, axis=-1, keepdims=True) + EPS) * g


def _ada_kernel(c_ref, w_ref, b_ref, o_ref):
    c = c_ref[...]
    s = c / (1.0 + jnp.exp(-c))
    o_ref[...] = jnp.dot(s, w_ref[...], precision=lax.Precision.HIGHEST,
                         preferred_element_type=F32) + b_ref[...]


def _ada(cc, w, b):
    m, d = cc.shape
    n = w.shape[1]
    tn = 1024
    return pl.pallas_call(
        _ada_kernel,
        grid=(n // tn,),
        in_specs=[pl.BlockSpec((m, d), lambda j: (0, 0)),
                  pl.BlockSpec((d, tn), lambda j: (0, j)),
                  pl.BlockSpec((1, tn), lambda j: (0, j))],
        out_specs=pl.BlockSpec((m, tn), lambda j: (0, j)),
        out_shape=jax.ShapeDtypeStruct((m, n), F32),
        compiler_params=pltpu.CompilerParams(vmem_limit_bytes=VMEM_LIMIT),
        name="ada",
    )(cc, w, b)


def _qkv_kernel(x_ref, sh_ref, sc_ref, g_ref, w_ref, cos_ref, sin_ref,
                qa_ref, ka_ref, va_ref, qd_ref, kd_ref, vd_ref, *, rope):
    x = x_ref[0]
    h = _rms(x, g_ref[...]) * (1.0 + sc_ref[0]) + sh_ref[0]
    hb = h.astype(BF16)
    gw = NA_WIDTH
    scale = HEAD_DIM ** -0.5

    def proj(g):
        return jnp.dot(hb, w_ref[:, g * gw:(g + 1) * gw], preferred_element_type=F32)

    def roped(p):
        if not rope:
            return p
        cos2 = cos_ref[...]
        sin2 = sin_ref[...]
        even = (lax.broadcasted_iota(I32, cos2.shape, 1) % 2) == 0
        outs = []
        for c in range(gw // LANES):
            v = p[:, c * LANES:(c + 1) * LANES]
            nxt = pltpu.roll(v, LANES - 1, axis=1)
            prv = pltpu.roll(v, 1, axis=1)
            outs.append(v * cos2 + jnp.where(even, nxt, prv) * sin2)
        return jnp.concatenate(outs, axis=1)

    qa_ref[0] = (proj(0) * scale).astype(BF16)
    ka_ref[0] = proj(1).astype(BF16)
    va_ref[0] = proj(2).astype(BF16)
    qd_ref[0] = (roped(proj(3)) * (scale * LOG2E)).astype(BF16)
    kd_ref[0] = roped(proj(4)).astype(BF16)
    vd = proj(5).astype(BF16)
    ones = jnp.ones((vd.shape[0], DIFF_V), BF16)
    pieces = []
    for hd in range(DIFF_HEADS):
        pieces += [vd[:, hd * DIFF_V:(hd + 1) * DIFF_V], ones]
    vd_ref[0] = jnp.concatenate(pieces, axis=1)


def _qkv(x, sh, sc, g, w, cos2, sin2, *, rope, per_batch_mod, tm):
    b, l, d = x.shape
    n = w.shape[1]
    gw = NA_WIDTH
    mod_map = (lambda i, j: (i, 0, 0)) if per_batch_mod else (lambda i, j: (0, 0, 0))
    out_sds = jax.ShapeDtypeStruct((b, l, gw), BF16)
    out_spec = pl.BlockSpec((1, tm, gw), lambda i, j: (i, j, 0))
    return pl.pallas_call(
        functools.partial(_qkv_kernel, rope=rope),
        grid=(b, l // tm),
        in_specs=[pl.BlockSpec((1, tm, d), lambda i, j: (i, j, 0)),
                  pl.BlockSpec((1, 1, d), mod_map),
                  pl.BlockSpec((1, 1, d), mod_map),
                  pl.BlockSpec((1, d), lambda i, j: (0, 0)),
                  pl.BlockSpec((d, n), lambda i, j: (0, 0)),
                  pl.BlockSpec((tm, LANES), lambda i, j: (j, 0)),
                  pl.BlockSpec((tm, LANES), lambda i, j: (j, 0))],
        out_specs=[out_spec] * 5 + [pl.BlockSpec((1, tm, 2 * gw), lambda i, j: (i, j, 0))],
        out_shape=[out_sds] * 5 + [jax.ShapeDtypeStruct((b, l, 2 * gw), BF16)],
        compiler_params=pltpu.CompilerParams(vmem_limit_bytes=VMEM_LIMIT),
        name="qkv_rope" if rope else "qkv_ctx",
    )(x, sh, sc, g, w, cos2, sin2)


def _na_kernel(q_ref, k_ref, v_ref, kc_ref, vc_ref, bias_ref, o_ref, *, rows):
    rb = pl.program_id(2)
    ks = jnp.clip(rb * NA_QROWS - NA_WIN_R // 2, 0, rows - NA_KROWS)
    start = pl.multiple_of(ks * GRID_W, GRID_W)
    nk = NA_KROWS * GRID_W
    kw = k_ref[0, pl.ds(start, nk), :]
    vw = v_ref[0, pl.ds(start, nk), :]
    kc = kc_ref[0]
    vc = vc_ref[0]
    lane = lax.broadcasted_iota(I32, (1, LANES), 1)
    for s in range(NA_QROWS * GRID_W // NA_QSUB):
        sl = slice(s * NA_QSUB, (s + 1) * NA_QSUB)
        q = q_ref[0, sl, :]
        outs = []
        for hh in range(2):
            qh = jnp.where((lane // HEAD_DIM) == hh, q, jnp.zeros_like(q))
            s_loc = _dot_nt(qh, kw) + bias_ref[0, hh, sl, :]
            s_ctx = _dot_nt(qh, kc)
            mx = jnp.maximum(jnp.max(s_loc, axis=-1, keepdims=True),
                             jnp.max(s_ctx, axis=-1, keepdims=True))
            p_loc = jnp.exp(s_loc - mx)
            p_ctx = jnp.exp(s_ctx - mx)
            l = jnp.sum(p_loc, axis=-1, keepdims=True) + jnp.sum(p_ctx, axis=-1, keepdims=True)
            o = (jnp.dot(p_loc.astype(BF16), vw, preferred_element_type=F32)
                 + jnp.dot(p_ctx.astype(BF16), vc, preferred_element_type=F32))
            outs.append(o / l)
        o_ref[0, sl, :] = jnp.where(lane < HEAD_DIM, outs[0], outs[1]).astype(BF16)


def _na_bias_table(rel_bias, rows):
    nrb = rows // NA_QROWS
    n_ro = 2 * NA_WIN_R - 1
    qc = np.arange(GRID_W)[:, None]
    kc = np.arange(GRID_W)[None, :]
    c_start = np.clip(qc - NA_WIN_C // 2, 0, GRID_W - NA_WIN_C)
    valid_c = (kc >= c_start) & (kc < c_start + NA_WIN_C)
    co = np.clip(kc - qc + NA_WIN_C - 1, 0, 2 * NA_WIN_C - 2)
    tz = jnp.take(rel_bias, jnp.asarray(co.reshape(-1), I32), axis=2).reshape(NA_HEADS, n_ro, GRID_W, GRID_W)
    tz = jnp.where(jnp.asarray(valid_c)[None, None], tz, NEG)
    tz = jnp.concatenate([tz, jnp.full((NA_HEADS, 1, GRID_W, GRID_W), NEG, F32)], axis=1)
    sel = []
    for rb in (0, 1, nrb - 1):
        r0 = rb * NA_QROWS
        ks = int(np.clip(r0 - NA_WIN_R // 2, 0, rows - NA_KROWS))
        qr = r0 + np.arange(NA_QROWS)[:, None]
        kr = ks + np.arange(NA_KROWS)[None, :]
        r_start = np.clip(qr - NA_WIN_R // 2, 0, rows - NA_WIN_R)
        valid_r = (kr >= r_start) & (kr < r_start + NA_WIN_R)
        sel.append(np.where(valid_r, kr - qr + NA_WIN_R - 1, n_ro))
    sel = np.stack(sel).reshape(-1)
    blocks = jnp.take(tz, jnp.asarray(sel, I32), axis=1)
    blocks = blocks.reshape(NA_HEADS, 3, NA_QROWS, NA_KROWS, GRID_W, GRID_W)
    return blocks.transpose(1, 0, 2, 4, 3, 5).reshape(3, NA_HEADS, NA_QROWS * GRID_W, NA_KROWS * GRID_W)


def _na(qa, ka, va, ka_c, va_c, bias):
    b, t, _ = qa.shape
    c = ka_c.shape[1]
    rows = t // GRID_W
    nrb = rows // NA_QROWS
    tq = NA_QROWS * GRID_W
    nk = NA_KROWS * GRID_W

    def bias_map(i, hp, rb):
        pat = jnp.where(rb == 0, 0, jnp.where(rb == nrb - 1, 2, 1))
        return (pat, hp, 0, 0)

    return pl.pallas_call(
        functools.partial(_na_kernel, rows=rows),
        grid=(b, NA_HEADS // 2, nrb),
        in_specs=[pl.BlockSpec((1, tq, LANES), lambda i, hp, rb: (i, rb, hp)),
                  pl.BlockSpec((1, t, LANES), lambda i, hp, rb: (i, 0, hp)),
                  pl.BlockSpec((1, t, LANES), lambda i, hp, rb: (i, 0, hp)),
                  pl.BlockSpec((1, c, LANES), lambda i, hp, rb: (i, 0, hp)),
                  pl.BlockSpec((1, c, LANES), lambda i, hp, rb: (i, 0, hp)),
                  pl.BlockSpec((1, 2, tq, nk), bias_map)],
        out_specs=pl.BlockSpec((1, tq, LANES), lambda i, hp, rb: (i, rb, hp)),
        out_shape=jax.ShapeDtypeStruct((b, t, NA_WIDTH), BF16),
        compiler_params=pltpu.CompilerParams(vmem_limit_bytes=VMEM_LIMIT),
        name="na_attn",
    )(qa, ka, va, ka_c, va_c, bias)


def _diff_kernel(lam_ref, q_ref, k_ref, v_ref, kc_ref, vc_ref, g_ref, o_ref, *, tk):
    q = q_ref[0]
    tq = q.shape[0]
    t = k_ref.shape[1]
    lane = lax.broadcasted_iota(I32, (1, LANES), 1)
    zero = jnp.zeros_like(q)
    qs = (jnp.where(lane < DIFF_QK, q, zero), jnp.where(lane >= DIFF_QK, q, zero))

    def update(kb, vb, carry):
        new = []
        for mi in range(2):
            m, a = carry[mi]
            s = _dot_nt(qs[mi], kb)
            mn = jnp.maximum(m, jnp.max(s, axis=-1, keepdims=True))
            p = jnp.exp2(s - mn)
            a = jnp.exp2(m - mn) * a + jnp.dot(p.astype(BF16), vb, preferred_element_type=F32)
            new.append((mn, a))
        return tuple(new)

    init = tuple((jnp.full((tq, 1), NEG, F32), jnp.zeros((tq, 2 * DIFF_V), F32)) for _ in range(2))
    carry = update(kc_ref[0], vc_ref[0], init)
    for i in range(t // tk):
        carry = update(k_ref[0, i * tk:(i + 1) * tk, :], v_ref[0, i * tk:(i + 1) * tk, :], carry)
    lp = lam_ref[...]
    lam = (jnp.exp(jnp.sum(lp[0:1] * lp[1:2], axis=-1, keepdims=True))
           - jnp.exp(jnp.sum(lp[2:3] * lp[3:4], axis=-1, keepdims=True)) + LAMBDA_INIT)
    (_, a0), (_, a1) = carry
    out = (a0[:, :DIFF_V] / a0[:, DIFF_V:DIFF_V + 1]
           - lam * (a1[:, :DIFF_V] / a1[:, DIFF_V:DIFF_V + 1]))
    o_ref[0] = (_rms(out, g_ref[...]) * (1.0 - LAMBDA_INIT)).astype(BF16)


def _diff(lam_params, qd, kd, vd, kd_c, vd_c, subln, *, tq, tk):
    b, t, w = qd.shape
    c = kd_c.shape[1]
    return pl.pallas_call(
        functools.partial(_diff_kernel, tk=tk),
        grid=(b, DIFF_HEADS, t // tq),
        in_specs=[pl.BlockSpec((4, DIFF_QK), lambda i, h, j: (0, 0)),
                  pl.BlockSpec((1, tq, LANES), lambda i, h, j: (i, j, h)),
                  pl.BlockSpec((1, t, LANES), lambda i, h, j: (i, 0, h)),
                  pl.BlockSpec((1, t, 2 * DIFF_V), lambda i, h, j: (i, 0, h)),
                  pl.BlockSpec((1, c, LANES), lambda i, h, j: (i, 0, h)),
                  pl.BlockSpec((1, c, 2 * DIFF_V), lambda i, h, j: (i, 0, h)),
                  pl.BlockSpec((1, DIFF_V), lambda i, h, j: (0, 0))],
        out_specs=pl.BlockSpec((1, tq, LANES), lambda i, h, j: (i, j, h)),
        out_shape=jax.ShapeDtypeStruct((b, t, w), BF16),
        compiler_params=pltpu.CompilerParams(vmem_limit_bytes=VMEM_LIMIT),
        name="diff_attn",
    )(lam_params, qd, kd, vd, kd_c, vd_c, subln)


def _out_kernel(oa_ref, od_ref, x_ref, gt1_ref, sh2_ref, sc2_ref, g1_ref, g2_ref,
                wout_ref, wq_ref, keys_ref, x1_ref, h2_ref, st_ref):
    wa = NA_WIDTH
    y = (jnp.dot(oa_ref[0], wout_ref[0:wa, :], preferred_element_type=F32)
         + jnp.dot(od_ref[0], wout_ref[wa:, :], preferred_element_type=F32))
    x1 = x_ref[0] + gt1_ref[0] * _rms(y, g1_ref[...])
    x1_ref[0] = x1
    h2 = _rms(x1, g2_ref[...]) * (1.0 + sc2_ref[0]) + sh2_ref[0]
    h2_ref[0] = h2
    qb = jnp.dot(h2.astype(BF16), wq_ref[...], preferred_element_type=F32).astype(BF16)
    for hp in range(2 * PEER_HEADS):
        st_ref[0, hp] = _dot_nt(keys_ref[hp], qb[:, hp * LANES:(hp + 1) * LANES])


def _out(oa, od, x, gt1, sh2, sc2, g1, g2, wout, wq, keys, *, tm):
    b, t, d = x.shape
    nq = wq.shape[1]
    nhp = keys.shape[0]
    mod = pl.BlockSpec((1, 1, d), lambda i, j: (i, 0, 0))
    gsp = pl.BlockSpec((1, d), lambda i, j: (0, 0))
    tok = pl.BlockSpec((1, tm, d), lambda i, j: (i, j, 0))
    half = pl.BlockSpec((1, tm, NA_WIDTH), lambda i, j: (i, j, 0))
    return pl.pallas_call(
        _out_kernel,
        grid=(b, t // tm),
        in_specs=[half, half, tok, mod, mod, mod, gsp, gsp,
                  pl.BlockSpec((d, d), lambda i, j: (0, 0)),
                  pl.BlockSpec((d, nq), lambda i, j: (0, 0)),
                  pl.BlockSpec((nhp, PEER_NKEYS, LANES), lambda i, j: (0, 0, 0))],
        out_specs=[tok, tok, pl.BlockSpec((1, nhp, PEER_NKEYS, tm), lambda i, j: (i, 0, 0, j))],
        out_shape=[jax.ShapeDtypeStruct((b, t, d), F32), jax.ShapeDtypeStruct((b, t, d), F32),
                   jax.ShapeDtypeStruct((b, nhp, PEER_NKEYS, t), F32)],
        compiler_params=pltpu.CompilerParams(vmem_limit_bytes=VMEM_LIMIT),
        name="out_proj_peer_scores",
    )(oa, od, x, gt1, sh2, sc2, g1, g2, wout, wq, keys)


def _top1(s, iota):
    m = jnp.max(s, axis=0, keepdims=True)
    idx = jnp.min(jnp.where(s == m, iota, s.shape[0]), axis=0, keepdims=True)
    return m, idx, jnp.where(iota == idx, -jnp.inf, s)


def _topk_kernel(st_ref, idx_ref, gate_ref):
    tl = st_ref.shape[-1]
    k = PEER_TOPK
    iota_n = lax.broadcasted_iota(I32, (PEER_NKEYS, tl), 0)
    widths = [k // (a + 1) for a in range(k)]
    n_cand = -(-sum(widths) // 8) * 8
    iota_c = lax.broadcasted_iota(I32, (n_cand, tl), 0)
    pad = n_cand - sum(widths)
    for h in range(PEER_HEADS):
        tops = []
        for p in range(2):
            s = st_ref[0, 2 * h + p]
            vals, idxs = [], []
            for _ in range(k):
                m, i, s = _top1(s, iota_n)
                vals.append(m)
                idxs.append(i)
            tops.append((jnp.concatenate(vals, axis=0), jnp.concatenate(idxs, axis=0)))
        (s0, i0), (s1, i1) = tops
        cand = jnp.concatenate([s0[a:a + 1] + s1[0:widths[a]] for a in range(k)]
                               + [jnp.full((pad, tl), -jnp.inf, F32)], axis=0)
        cidx = jnp.concatenate([i0[a:a + 1] * PEER_NKEYS + i1[0:widths[a]] for a in range(k)]
                               + [jnp.zeros((pad, tl), I32)], axis=0)
        best, experts = [], []
        for _ in range(k):
            m, pos, cand = _top1(cand, iota_c)
            best.append(m)
            experts.append(jnp.sum(jnp.where(iota_c == pos, cidx, 0), axis=0, keepdims=True))
        best = jnp.concatenate(best, axis=0)
        e = jnp.exp(best - best[0:1])
        gate_ref[0, h * k:(h + 1) * k, :] = e / jnp.sum(e, axis=0, keepdims=True)
        idx_ref[0, h * k:(h + 1) * k, :] = jnp.concatenate(experts, axis=0)


def _topk(st, *, tl):
    b, nhp, nkeys, t = st.shape
    nj = PEER_NJ
    out_spec = pl.BlockSpec((1, nj, tl), lambda i, j: (i, 0, j))
    return pl.pallas_call(
        _topk_kernel,
        grid=(b, t // tl),
        in_specs=[pl.BlockSpec((1, nhp, nkeys, tl), lambda i, j: (i, 0, 0, j))],
        out_specs=[out_spec, out_spec],
        out_shape=[jax.ShapeDtypeStruct((b, nj, t), I32), jax.ShapeDtypeStruct((b, nj, t), F32)],
        compiler_params=pltpu.CompilerParams(vmem_limit_bytes=VMEM_LIMIT),
        name="peer_topk",
    )(st)


def _pack_table(tab):
    e, d = tab.shape
    tb = tab.astype(BF16)
    lo = lax.bitcast_convert_type(tb[:, :d // 2], jnp.uint16).astype(jnp.uint32)
    hi = lax.bitcast_convert_type(tb[:, d // 2:], jnp.uint16).astype(jnp.uint32)
    words = lax.bitcast_convert_type((hi << 16) | lo, I32)
    return words.reshape(e * (d // 2 // LANES), LANES)


def _gather_rows(idx_ref, base, table_ref, tile_ref):
    for j in range(PEER_NJ):
        row = pl.multiple_of(idx_ref[base + j], 4)
        tile_ref[pl.ds(j, 4, stride=TILE_STRIDE), :] = table_ref[pl.ds(row, 4), :]


def _tile_chunk(tile_ref, c):
    return pltpu.bitcast(tile_ref[c * TILE_STRIDE:c * TILE_STRIDE + PEER_NJ, :], BF16)


def _split_bf16(v):
    head = v.astype(BF16).astype(F32)
    return jnp.concatenate([head, v - head], axis=0)


def _idx_copy(idx_hbm, first_token, buf, sem):
    n = buf.shape[0]
    src = idx_hbm.at[pl.ds(pl.multiple_of(first_token * PEER_NJ, n), n)]
    return pltpu.make_async_copy(src, buf, sem)


def _for_each_group(idx_hbm, ibufs, sems, group):
    step = pl.program_id(0)
    half = PEER_TM // 2
    nu = PEER_UNROLL

    @pl.when(step == 0)
    def _():
        for h in range(2):
            _idx_copy(idx_hbm, h * half, ibufs[h], sems.at[h]).start()

    for h in range(2):
        tok0 = step * PEER_TM + h * half
        _idx_copy(idx_hbm, tok0, ibufs[h], sems.at[h]).wait()
        for g in range(half // nu):
            group(ibufs[h], g * nu * PEER_NJ, h * half + g * nu)

        @pl.when(step + 1 < pl.num_programs(0))
        def _():
            _idx_copy(idx_hbm, tok0 + PEER_TM, ibufs[h], sems.at[h]).start()


def _peer_scratch():
    return ([pltpu.VMEM((4 * TILE_STRIDE + 4, LANES), I32) for _ in range(PEER_UNROLL)]
            + [pltpu.SMEM((PEER_TM // 2 * PEER_NJ,), I32) for _ in range(2)]
            + [pltpu.SemaphoreType.DMA((2,))])


def _peer_u_kernel(idx_hbm, x_ref, gate_ref, u_ref, w_ref, *scratch):
    nj = PEER_NJ
    nu = PEER_UNROLL
    tile_refs, ibufs, sems = scratch[:nu], scratch[nu:nu + 2], scratch[nu + 2]
    row16 = lax.broadcasted_iota(I32, (16, 1), 0) % 8
    even = (lax.broadcasted_iota(I32, (1, 2 * nj), 1) % 2) == 0

    def token(ibuf, off, t, tile_ref):
        _gather_rows(ibuf, off, u_ref, tile_ref)
        x16 = _split_bf16(x_ref[t])
        acc = jnp.zeros((16, 2 * nj), F32)
        for c in range(4):
            lhs = jnp.where((row16 == c) | (row16 == 4 + c), x16, 0.0).astype(BF16)
            acc = acc + _dot_nt(lhs, _tile_chunk(tile_ref, c))
        a = acc[0:8] + acc[8:16]
        return jnp.sum(jnp.where(even, a, pltpu.roll(a, 4, axis=0))[0:4], axis=0, keepdims=True)

    def group(ibuf, off, t0):
        z = jnp.concatenate([token(ibuf, off + i * nj, t0 + i, tile_refs[i]) for i in range(nu)], axis=0)
        act = z + jnp.where(even, pltpu.roll(z, 2 * nj - 1, axis=1), pltpu.roll(z, 1, axis=1))
        gelu = 0.5 * act * (1.0 + lax.erf(act * (2.0 ** -0.5)))
        w_ref[t0:t0 + nu, :] = gate_ref[t0:t0 + nu, :] * gelu

    _for_each_group(idx_hbm, ibufs, sems, group)


def _peer_u(idx4, x3, gate2, table):
    n = x3.shape[0]
    nj = PEER_NJ
    tm = PEER_TM
    return pl.pallas_call(
        _peer_u_kernel,
        grid=(n // tm,),
        in_specs=[pl.BlockSpec(memory_space=pl.ANY),
                  pl.BlockSpec((tm, 8, LANES), lambda i: (i, 0, 0)),
                  pl.BlockSpec((tm, 2 * nj), lambda i: (i, 0)),
                  pl.BlockSpec(memory_space=pltpu.VMEM)],
        out_specs=pl.BlockSpec((tm, 2 * nj), lambda i: (i, 0)),
        out_shape=jax.ShapeDtypeStruct((n, 2 * nj), F32),
        scratch_shapes=_peer_scratch(),
        compiler_params=pltpu.CompilerParams(vmem_limit_bytes=VMEM_LIMIT, dimension_semantics=("arbitrary",)),
        name="peer_u",
    )(idx4, x3, gate2, table)


def _peer_v_kernel(idx_hbm, w_ref, v_ref, o_ref, *scratch):
    nj = PEER_NJ
    nu = PEER_UNROLL
    tile_refs, ibufs, sems = scratch[:nu], scratch[nu:nu + 2], scratch[nu + 2]
    row16 = lax.broadcasted_iota(I32, (16, 1), 0) % 8
    even = (lax.broadcasted_iota(I32, (1, 2 * nj), 1) % 2) == 0

    def token(ibuf, off, t, tile_ref):
        _gather_rows(ibuf, off, v_ref, tile_ref)
        w16 = _split_bf16(jnp.broadcast_to(w_ref[t:t + 1, :], (8, 2 * nj)))
        acc = jnp.zeros((16, LANES), F32)
        for c in range(4):
            keep = ((row16 == c) & even) | ((row16 == 4 + c) & jnp.logical_not(even))
            lhs = jnp.where(keep, w16, 0.0).astype(BF16)
            acc = acc + jnp.dot(lhs, _tile_chunk(tile_ref, c), preferred_element_type=F32)
        o_ref[t] = acc[0:8] + acc[8:16]

    def group(ibuf, off, t0):
        for i in range(nu):
            token(ibuf, off + i * nj, t0 + i, tile_refs[i])

    _for_each_group(idx_hbm, ibufs, sems, group)


def _peer_v(idx4, w2, table):
    n = w2.shape[0]
    nj = PEER_NJ
    tm = PEER_TM
    return pl.pallas_call(
        _peer_v_kernel,
        grid=(n // tm,),
        in_specs=[pl.BlockSpec(memory_space=pl.ANY),
                  pl.BlockSpec((tm, 2 * nj), lambda i: (i, 0)),
                  pl.BlockSpec(memory_space=pltpu.VMEM)],
        out_specs=pl.BlockSpec((tm, 8, LANES), lambda i: (i, 0, 0)),
        out_shape=jax.ShapeDtypeStruct((n, 8, LANES), F32),
        scratch_shapes=_peer_scratch(),
        compiler_params=pltpu.CompilerParams(vmem_limit_bytes=VMEM_LIMIT, dimension_semantics=("arbitrary",)),
        name="peer_v",
    )(idx4, w2, table)


def _final_kernel(x1_ref, y_ref, gt2_ref, g_ref, o_ref):
    o_ref[0] = x1_ref[0] + gt2_ref[0] * _rms(y_ref[0], g_ref[...])


def _final(x1, y, gt2, g, *, tm):
    b, t, d = x1.shape
    tok = pl.BlockSpec((1, tm, d), lambda i, j: (i, j, 0))
    return pl.pallas_call(
        _final_kernel,
        grid=(b, t // tm),
        in_specs=[tok, tok, pl.BlockSpec((1, 1, d), lambda i, j: (i, 0, 0)),
                  pl.BlockSpec((1, d), lambda i, j: (0, 0))],
        out_specs=tok,
        out_shape=jax.ShapeDtypeStruct((b, t, d), F32),
        compiler_params=pltpu.CompilerParams(vmem_limit_bytes=VMEM_LIMIT),
        name="final_residual",
    )(x1, y, gt2, g)


def _rope_tables(t):
    tok = jnp.arange(t)
    row = (tok // GRID_W).astype(F32)
    col = (tok % GRID_W).astype(F32)
    n_freq = DIFF_QK // 4
    inv = ROPE_THETA ** (-jnp.arange(n_freq, dtype=F32) / n_freq)
    ang = jnp.concatenate([row[:, None] * inv, col[:, None] * inv], axis=-1)
    ang2 = jnp.tile(jnp.repeat(ang, 2, axis=1), (1, LANES // DIFF_QK))
    sign = jnp.where(jnp.arange(LANES) % 2 == 0, -1.0, 1.0).astype(F32)
    return jnp.cos(ang2), jnp.sin(ang2) * sign


def _head_major_cols(w):
    d = w.shape[0]
    return w.reshape(d, 2, DIFF_HEADS, DIFF_QK).transpose(0, 2, 1, 3).reshape(d, -1)


def kernel(x, c, ctx, c_ctx, w_ada, b_ada, g_norm, w_in, na_rel_bias, diff_lambda, diff_subln,
           w_out, peer_wq, peer_keys, peer_u, peer_v):
    assert w_ada.shape[0] == 1, "single-layer kernel"
    b, t, d = x.shape
    n_ctx = ctx.shape[1]
    rows = t // GRID_W
    assert t % (NA_QROWS * GRID_W) == 0 and rows >= 3 * NA_QROWS

    pad = (-(b + 1)) % 8
    cc = jnp.concatenate([c, c_ctx[None], jnp.zeros((pad, d), F32)], axis=0)
    mod = _ada(cc, w_ada[0], b_ada[0][None]).reshape(b + 1 + pad, N_MOD, d)
    lat = lambda i: mod[:b, i][:, None, :]
    cxm = lambda i: mod[b:b + 1, i][:, None, :]

    gw = NA_WIDTH
    w_in0 = w_in[0]
    w_perm = jnp.concatenate([w_in0[:, :3 * gw], _head_major_cols(w_in0[:, 3 * gw:4 * gw]),
                              _head_major_cols(w_in0[:, 4 * gw:5 * gw]), w_in0[:, 5 * gw:]], axis=1).astype(BF16)
    cos2, sin2 = _rope_tables(t)
    g0 = g_norm[0, 0][None]
    qa, ka, va, qd, kd, vd = _qkv(x, lat(0), lat(1), g0, w_perm, cos2, sin2,
                                  rope=True, per_batch_mod=True, tm=512)
    _, ka_c, va_c, _, kd_c, vd_c = _qkv(ctx, cxm(0), cxm(1), g0, w_perm, cos2[:n_ctx], sin2[:n_ctx],
                                        rope=False, per_batch_mod=False, tm=n_ctx)

    out_a = _na(qa, ka, va, ka_c, va_c, _na_bias_table(na_rel_bias[0], rows))
    out_d = _diff(diff_lambda[0], qd, kd, vd, kd_c, vd_c, diff_subln[0][None], tq=DIFF_TQ, tk=DIFF_TK)

    keys = peer_keys[0].reshape(2 * PEER_HEADS, PEER_NKEYS, -1).astype(BF16)
    x1, h2, st = _out(out_a, out_d, x, lat(2), lat(3), lat(4), g_norm[0, 1][None], g_norm[0, 2][None],
                      w_out[0].astype(BF16), peer_wq[0].astype(BF16), keys, tm=256)

    idx_t, gate_t = _topk(st, tl=256)
    idx4 = (idx_t.transpose(0, 2, 1) * 4).reshape(-1)
    gate2 = jnp.repeat(gate_t.transpose(0, 2, 1).reshape(b * t, PEER_NJ), 2, axis=1)

    w2 = _peer_u(idx4, h2.reshape(b * t, 8, LANES), gate2, _pack_table(peer_u[0]))
    y = _peer_v(idx4, w2, _pack_table(peer_v[0])).reshape(b, t, d)
    return _final(x1, y, lat(5), g_norm[0, 3][None], tm=512)
```

```python
import functools
import math

import numpy as np
import jax
import jax.numpy as jnp
from jax import lax
from jax.experimental import pallas as pl
from jax.experimental.pallas import tpu as pltpu

F32 = jnp.float32
BF16 = jnp.bfloat16
I32 = jnp.int32

GRID_W = 64
HEAD_DIM = 64
NA_HEADS = 8
NA_WIDTH = NA_HEADS * HEAD_DIM
NA_WIN_R = 8
NA_WIN_C = 16
DIFF_HEADS = 4
DIFF_QK = HEAD_DIM
DIFF_V = 2 * DIFF_QK
ROPE_THETA = 10000.0
PEER_HEADS = 8
PEER_NKEYS = 128
PEER_TOPK = 16
N_MOD = 6
EPS = 1e-6
LAMBDA_INIT = 0.8 - 0.6 * math.exp(-0.3 * 0)
LOG2E = math.log2(math.e)
DIFF_TQ = 512
DIFF_TK = 512

LANES = 128
VMEM_LIMIT = 56 * 1024 * 1024
NEG = -1e30

NA_QROWS = 8
NA_KROWS = 16
NA_QSUB = 128
PEER_TM = 64
PEER_NJ = PEER_HEADS * PEER_TOPK
TILE_STRIDE = PEER_NJ + 1
PEER_UNROLL = 8


def _dot_nt(a, b):
    return lax.dot_general(a, b, (((1,), (1,)), ((), ())), preferred_element_type=F32)


def _rms(x, g):
    return x * lax.rsqrt(jnp.mean(x * x, axis=-1, keepdims=True) + EPS) * g


def _ada_kernel(c_ref, w_ref, b_ref, o_ref):
    c = c_ref[...]
    s = c / (1.0 + jnp.exp(-c))
    o_ref[...] = jnp.dot(s, w_ref[...], precision=lax.Precision.HIGHEST,
                         preferred_element_type=F32) + b_ref[...]


def _ada(cc, w, b):
    m, d = cc.shape
    n = w.shape[1]
    tn = 1024
    return pl.pallas_call(
        _ada_kernel,
        grid=(n // tn,),
        in_specs=[pl.BlockSpec((m, d), lambda j: (0, 0)),
                  pl.BlockSpec((d, tn), lambda j: (0, j)),
                  pl.BlockSpec((1, tn), lambda j: (0, j))],
        out_specs=pl.BlockSpec((m, tn), lambda j: (0, j)),
        out_shape=jax.ShapeDtypeStruct((m, n), F32),
        compiler_params=pltpu.CompilerParams(vmem_limit_bytes=VMEM_LIMIT),
        name="ada",
    )(cc, w, b)


def _qkv_kernel(x_ref, sh_ref, sc_ref, g_ref, w_ref, cos_ref, sin_ref,
                qa_ref, ka_ref, va_ref, qd_ref, kd_ref, vd_ref, *, rope):
    x = x_ref[0]
    h = _rms(x, g_ref[...]) * (1.0 + sc_ref[0]) + sh_ref[0]
    hb = h.astype(BF16)
    gw = NA_WIDTH
    scale = HEAD_DIM ** -0.5

    def proj(g):
        return jnp.dot(hb, w_ref[:, g * gw:(g + 1) * gw], preferred_element_type=F32)

    def roped(p):
        if not rope:
            return p
        cos2 = cos_ref[...]
        sin2 = sin_ref[...]
        even = (lax.broadcasted_iota(I32, cos2.shape, 1) % 2) == 0
        outs = []
        for c in range(gw // LANES):
            v = p[:, c * LANES:(c + 1) * LANES]
            nxt = pltpu.roll(v, LANES - 1, axis=1)
            prv = pltpu.roll(v, 1, axis=1)
            outs.append(v * cos2 + jnp.where(even, nxt, prv) * sin2)
        return jnp.concatenate(outs, axis=1)

    qa_ref[0] = (proj(0) * scale).astype(BF16)
    ka_ref[0] = proj(1).astype(BF16)
    va_ref[0] = proj(2).astype(BF16)
    qd_ref[0] = (roped(proj(3)) * (scale * LOG2E)).astype(BF16)
    kd_ref[0] = roped(proj(4)).astype(BF16)
    vd = proj(5).astype(BF16)
    ones = jnp.ones((vd.shape[0], DIFF_V), BF16)
    pieces = []
    for hd in range(DIFF_HEADS):
        pieces += [vd[:, hd * DIFF_V:(hd + 1) * DIFF_V], ones]
    vd_ref[0] = jnp.concatenate(pieces, axis=1)


def _qkv(x, sh, sc, g, w, cos2, sin2, *, rope, per_batch_mod, tm):
    b, l, d = x.shape
    n = w.shape[1]
    gw = NA_WIDTH
    mod_map = (lambda i, j: (i, 0, 0)) if per_batch_mod else (lambda i, j: (0, 0, 0))
    out_sds = jax.ShapeDtypeStruct((b, l, gw), BF16)
    out_spec = pl.BlockSpec((1, tm, gw), lambda i, j: (i, j, 0))
    return pl.pallas_call(
        functools.partial(_qkv_kernel, rope=rope),
        grid=(b, l // tm),
        in_specs=[pl.BlockSpec((1, tm, d), lambda i, j: (i, j, 0)),
                  pl.BlockSpec((1, 1, d), mod_map),
                  pl.BlockSpec((1, 1, d), mod_map),
                  pl.BlockSpec((1, d), lambda i, j: (0, 0)),
                  pl.BlockSpec((d, n), lambda i, j: (0, 0)),
                  pl.BlockSpec((tm, LANES), lambda i, j: (j, 0)),
                  pl.BlockSpec((tm, LANES), lambda i, j: (j, 0))],
        out_specs=[out_spec] * 5 + [pl.BlockSpec((1, tm, 2 * gw), lambda i, j: (i, j, 0))],
        out_shape=[out_sds] * 5 + [jax.ShapeDtypeStruct((b, l, 2 * gw), BF16)],
        compiler_params=pltpu.CompilerParams(vmem_limit_bytes=VMEM_LIMIT),
        name="qkv_rope" if rope else "qkv_ctx",
    )(x, sh, sc, g, w, cos2, sin2)


def _na_kernel(q_ref, k_ref, v_ref, kc_ref, vc_ref, bias_ref, o_ref, *, rows):
    rb = pl.program_id(2)
    ks = jnp.clip(rb * NA_QROWS - NA_WIN_R // 2, 0, rows - NA_KROWS)
    start = pl.multiple_of(ks * GRID_W, GRID_W)
    nk = NA_KROWS * GRID_W
    kw = k_ref[0, pl.ds(start, nk), :]
    vw = v_ref[0, pl.ds(start, nk), :]
    kc = kc_ref[0]
    vc = vc_ref[0]
    lane = lax.broadcasted_iota(I32, (1, LANES), 1)
    for s in range(NA_QROWS * GRID_W // NA_QSUB):
        sl = slice(s * NA_QSUB, (s + 1) * NA_QSUB)
        q = q_ref[0, sl, :]
        outs = []
        for hh in range(2):
            qh = jnp.where((lane // HEAD_DIM) == hh, q, jnp.zeros_like(q))
            s_loc = _dot_nt(qh, kw) + bias_ref[0, hh, sl, :]
            s_ctx = _dot_nt(qh, kc)
            mx = jnp.maximum(jnp.max(s_loc, axis=-1, keepdims=True),
                             jnp.max(s_ctx, axis=-1, keepdims=True))
            p_loc = jnp.exp(s_loc - mx)
            p_ctx = jnp.exp(s_ctx - mx)
            l = jnp.sum(p_loc, axis=-1, keepdims=True) + jnp.sum(p_ctx, axis=-1, keepdims=True)
            o = (jnp.dot(p_loc.astype(BF16), vw, preferred_element_type=F32)
                 + jnp.dot(p_ctx.astype(BF16), vc, preferred_element_type=F32))
            outs.append(o / l)
        o_ref[0, sl, :] = jnp.where(lane < HEAD_DIM, outs[0], outs[1]).astype(BF16)


def _na_bias_table(rel_bias, rows):
    nrb = rows // NA_QROWS
    n_ro = 2 * NA_WIN_R - 1
    qc = np.arange(GRID_W)[:, None]
    kc = np.arange(GRID_W)[None, :]
    c_start = np.clip(qc - NA_WIN_C // 2, 0, GRID_W - NA_WIN_C)
    valid_c = (kc >= c_start) & (kc < c_start + NA_WIN_C)
    co = np.clip(kc - qc + NA_WIN_C - 1, 0, 2 * NA_WIN_C - 2)
    tz = jnp.take(rel_bias, jnp.asarray(co.reshape(-1), I32), axis=2).reshape(NA_HEADS, n_ro, GRID_W, GRID_W)
    tz = jnp.where(jnp.asarray(valid_c)[None, None], tz, NEG)
    tz = jnp.concatenate([tz, jnp.full((NA_HEADS, 1, GRID_W, GRID_W), NEG, F32)], axis=1)
    sel = []
    for rb in (0, 1, nrb - 1):
        r0 = rb * NA_QROWS
        ks = int(np.clip(r0 - NA_WIN_R // 2, 0, rows - NA_KROWS))
        qr = r0 + np.arange(NA_QROWS)[:, None]
        kr = ks + np.arange(NA_KROWS)[None, :]
        r_start = np.clip(qr - NA_WIN_R // 2, 0, rows - NA_WIN_R)
        valid_r = (kr >= r_start) & (kr < r_start + NA_WIN_R)
        sel.append(np.where(valid_r, kr - qr + NA_WIN_R - 1, n_ro))
    sel = np.stack(sel).reshape(-1)
    blocks = jnp.take(tz, jnp.asarray(sel, I32), axis=1)
    blocks = blocks.reshape(NA_HEADS, 3, NA_QROWS, NA_KROWS, GRID_W, GRID_W)
    return blocks.transpose(1, 0, 2, 4, 3, 5).reshape(3, NA_HEADS, NA_QROWS * GRID_W, NA_KROWS * GRID_W)


def _na(qa, ka, va, ka_c, va_c, bias):
    b, t, _ = qa.shape
    c = ka_c.shape[1]
    rows = t // GRID_W
    nrb = rows // NA_QROWS
    tq = NA_QROWS * GRID_W
    nk = NA_KROWS * GRID_W

    def bias_map(i, hp, rb):
        pat = jnp.where(rb == 0, 0, jnp.where(rb == nrb - 1, 2, 1))
        return (pat, hp, 0, 0)

    return pl.pallas_call(
        functools.partial(_na_kernel, rows=rows),
        grid=(b, NA_HEADS // 2, nrb),
        in_specs=[pl.BlockSpec((1, tq, LANES), lambda i, hp, rb: (i, rb, hp)),
                  pl.BlockSpec((1, t, LANES), lambda i, hp, rb: (i, 0, hp)),
                  pl.BlockSpec((1, t, LANES), lambda i, hp, rb: (i, 0, hp)),
                  pl.BlockSpec((1, c, LANES), lambda i, hp, rb: (i, 0, hp)),
                  pl.BlockSpec((1, c, LANES), lambda i, hp, rb: (i, 0, hp)),
                  pl.BlockSpec((1, 2, tq, nk), bias_map)],
        out_specs=pl.BlockSpec((1, tq, LANES), lambda i, hp, rb: (i, rb, hp)),
        out_shape=jax.ShapeDtypeStruct((b, t, NA_WIDTH), BF16),
        compiler_params=pltpu.CompilerParams(vmem_limit_bytes=VMEM_LIMIT),
        name="na_attn",
    )(qa, ka, va, ka_c, va_c, bias)


def _diff_kernel(lam_ref, q_ref, k_ref, v_ref, kc_ref, vc_ref, g_ref, o_ref, *, tk):
    q = q_ref[0]
    tq = q.shape[0]
    t = k_ref.shape[1]
    lane = lax.broadcasted_iota(I32, (1, LANES), 1)
    zero = jnp.zeros_like(q)
    qs = (jnp.where(lane < DIFF_QK, q, zero), jnp.where(lane >= DIFF_QK, q, zero))

    def update(kb, vb, carry):
        new = []
        for mi in range(2):
            m, a = carry[mi]
            s = _dot_nt(qs[mi], kb)
            mn = jnp.maximum(m, jnp.max(s, axis=-1, keepdims=True))
            p = jnp.exp2(s - mn)
            a = jnp.exp2(m - mn) * a + jnp.dot(p.astype(BF16), vb, preferred_element_type=F32)
            new.append((mn, a))
        return tuple(new)

    init = tuple((jnp.full((tq, 1), NEG, F32), jnp.zeros((tq, 2 * DIFF_V), F32)) for _ in range(2))
    carry = update(kc_ref[0], vc_ref[0], init)
    for i in range(t // tk):
        carry = update(k_ref[0, i * tk:(i + 1) * tk, :], v_ref[0, i * tk:(i + 1) * tk, :], carry)
    lp = lam_ref[...]
    lam = (jnp.exp(jnp.sum(lp[0:1] * lp[1:2], axis=-1, keepdims=True))
           - jnp.exp(jnp.sum(lp[2:3] * lp[3:4], axis=-1, keepdims=True)) + LAMBDA_INIT)
    (_, a0), (_, a1) = carry
    out = (a0[:, :DIFF_V] / a0[:, DIFF_V:DIFF_V + 1]
           - lam * (a1[:, :DIFF_V] / a1[:, DIFF_V:DIFF_V + 1]))
    o_ref[0] = (_rms(out, g_ref[...]) * (1.0 - LAMBDA_INIT)).astype(BF16)


def _diff(lam_params, qd, kd, vd, kd_c, vd_c, subln, *, tq, tk):
    b, t, w = qd.shape
    c = kd_c.shape[1]
    return pl.pallas_call(
        functools.partial(_diff_kernel, tk=tk),
        grid=(b, DIFF_HEADS, t // tq),
        in_specs=[pl.BlockSpec((4, DIFF_QK), lambda i, h, j: (0, 0)),
                  pl.BlockSpec((1, tq, LANES), lambda i, h, j: (i, j, h)),
                  pl.BlockSpec((1, t, LANES), lambda i, h, j: (i, 0, h)),
                  pl.BlockSpec((1, t, 2 * DIFF_V), lambda i, h, j: (i, 0, h)),
                  pl.BlockSpec((1, c, LANES), lambda i, h, j: (i, 0, h)),
                  pl.BlockSpec((1, c, 2 * DIFF_V), lambda i, h, j: (i, 0, h)),
                  pl.BlockSpec((1, DIFF_V), lambda i, h, j: (0, 0))],
        out_specs=pl.BlockSpec((1, tq, LANES), lambda i, h, j: (i, j, h)),
        out_shape=jax.ShapeDtypeStruct((b, t, w), BF16),
        compiler_params=pltpu.CompilerParams(vmem_limit_bytes=VMEM_LIMIT),
        name="diff_attn",
    )(lam_params, qd, kd, vd, kd_c, vd_c, subln)


def _out_kernel(oa_ref, od_ref, x_ref, gt1_ref, sh2_ref, sc2_ref, g1_ref, g2_ref,
                wout_ref, wq_ref, keys_ref, x1_ref, h2_ref, st_ref):
    wa = NA_WIDTH
    y = (jnp.dot(oa_ref[0], wout_ref[0:wa, :], preferred_element_type=F32)
         + jnp.dot(od_ref[0], wout_ref[wa:, :], preferred_element_type=F32))
    x1 = x_ref[0] + gt1_ref[0] * _rms(y, g1_ref[...])
    x1_ref[0] = x1
    h2 = _rms(x1, g2_ref[...]) * (1.0 + sc2_ref[0]) + sh2_ref[0]
    h2_ref[0] = h2
    qb = jnp.dot(h2.astype(BF16), wq_ref[...], preferred_element_type=F32).astype(BF16)
    for hp in range(2 * PEER_HEADS):
        st_ref[0, hp] = _dot_nt(keys_ref[hp], qb[:, hp * LANES:(hp + 1) * LANES])


def _out(oa, od, x, gt1, sh2, sc2, g1, g2, wout, wq, keys, *, tm):
    b, t, d = x.shape
    nq = wq.shape[1]
    nhp = keys.shape[0]
    mod = pl.BlockSpec((1, 1, d), lambda i, j: (i, 0, 0))
    gsp = pl.BlockSpec((1, d), lambda i, j: (0, 0))
    tok = pl.BlockSpec((1, tm, d), lambda i, j: (i, j, 0))
    half = pl.BlockSpec((1, tm, NA_WIDTH), lambda i, j: (i, j, 0))
    return pl.pallas_call(
        _out_kernel,
        grid=(b, t // tm),
        in_specs=[half, half, tok, mod, mod, mod, gsp, gsp,
                  pl.BlockSpec((d, d), lambda i, j: (0, 0)),
                  pl.BlockSpec((d, nq), lambda i, j: (0, 0)),
                  pl.BlockSpec((nhp, PEER_NKEYS, LANES), lambda i, j: (0, 0, 0))],
        out_specs=[tok, tok, pl.BlockSpec((1, nhp, PEER_NKEYS, tm), lambda i, j: (i, 0, 0, j))],
        out_shape=[jax.ShapeDtypeStruct((b, t, d), F32), jax.ShapeDtypeStruct((b, t, d), F32),
                   jax.ShapeDtypeStruct((b, nhp, PEER_NKEYS, t), F32)],
        compiler_params=pltpu.CompilerParams(vmem_limit_bytes=VMEM_LIMIT),
        name="out_proj_peer_scores",
    )(oa, od, x, gt1, sh2, sc2, g1, g2, wout, wq, keys)


def _top1(s, iota):
    m = jnp.max(s, axis=0, keepdims=True)
    idx = jnp.min(jnp.where(s == m, iota, s.shape[0]), axis=0, keepdims=True)
    return m, idx, jnp.where(iota == idx, -jnp.inf, s)


def _topk_kernel(st_ref, idx_ref, gate_ref):
    tl = st_ref.shape[-1]
    k = PEER_TOPK
    iota_n = lax.broadcasted_iota(I32, (PEER_NKEYS, tl), 0)
    widths = [k // (a + 1) for a in range(k)]
    n_cand = -(-sum(widths) // 8) * 8
    iota_c = lax.broadcasted_iota(I32, (n_cand, tl), 0)
    pad = n_cand - sum(widths)
    gates, rows = [], []
    for h in range(PEER_HEADS):
        tops = []
        for p in range(2):
            s = st_ref[0, 2 * h + p]
            vals, idxs = [], []
            for _ in range(k):
                m, i, s = _top1(s, iota_n)
                vals.append(m)
                idxs.append(i)
            tops.append((jnp.concatenate(vals, axis=0), jnp.concatenate(idxs, axis=0)))
        (s0, i0), (s1, i1) = tops
        cand = jnp.concatenate([s0[a:a + 1] + s1[0:widths[a]] for a in range(k)]
                               + [jnp.full((pad, tl), -jnp.inf, F32)], axis=0)
        cidx = jnp.concatenate([i0[a:a + 1] * PEER_NKEYS + i1[0:widths[a]] for a in range(k)]
                               + [jnp.zeros((pad, tl), I32)], axis=0)
        best, experts = [], []
        for _ in range(k):
            m, pos, cand = _top1(cand, iota_c)
            best.append(m)
            experts.append(jnp.sum(jnp.where(iota_c == pos, cidx, 0), axis=0, keepdims=True))
        best = jnp.concatenate(best, axis=0)
        e = jnp.exp(best - best[0:1])
        gates.append(e / jnp.sum(e, axis=0, keepdims=True))
        rows += experts
    idx_ref[0] = (jnp.concatenate(rows, axis=0) * 4).T
    nj = PEER_NJ
    dup = (lax.broadcasted_iota(I32, (nj, 2 * nj), 1) // 2 == lax.broadcasted_iota(I32, (nj, 2 * nj), 0))
    gate_ref[0] = jnp.dot(jnp.concatenate(gates, axis=0).T, dup.astype(F32),
                          precision=lax.Precision.HIGHEST, preferred_element_type=F32)


def _topk(st, *, tl):
    b, nhp, nkeys, t = st.shape
    nj = PEER_NJ
    return pl.pallas_call(
        _topk_kernel,
        grid=(b, t // tl),
        in_specs=[pl.BlockSpec((1, nhp, nkeys, tl), lambda i, j: (i, 0, 0, j))],
        out_specs=[pl.BlockSpec((1, tl, nj), lambda i, j: (i, j, 0)),
                   pl.BlockSpec((1, tl, 2 * nj), lambda i, j: (i, j, 0))],
        out_shape=[jax.ShapeDtypeStruct((b, t, nj), I32), jax.ShapeDtypeStruct((b, t, 2 * nj), F32)],
        compiler_params=pltpu.CompilerParams(vmem_limit_bytes=VMEM_LIMIT),
        name="peer_topk",
    )(st)


def _pack_table(tab):
    e, d = tab.shape
    tb = tab.astype(BF16)
    lo = lax.bitcast_convert_type(tb[:, :d // 2], jnp.uint16).astype(jnp.uint32)
    hi = lax.bitcast_convert_type(tb[:, d // 2:], jnp.uint16).astype(jnp.uint32)
    words = lax.bitcast_convert_type((hi << 16) | lo, I32)
    return words.reshape(e * (d // 2 // LANES), LANES)


def _gather_rows(idx_ref, base, table_ref, tile_ref):
    ahead = 4
    rows = [idx_ref[base + j] for j in range(ahead)]
    for j in range(PEER_NJ):
        if j + ahead < PEER_NJ:
            rows.append(idx_ref[base + j + ahead])
        row = pl.multiple_of(rows[j], 4)
        tile_ref[pl.ds(j, 4, stride=TILE_STRIDE), :] = table_ref[pl.ds(row, 4), :]


def _tile_chunk(tile_ref, c):
    return pltpu.bitcast(tile_ref[c * TILE_STRIDE:c * TILE_STRIDE + PEER_NJ, :], BF16)


def _split_bf16(v):
    head = v.astype(BF16).astype(F32)
    return jnp.concatenate([head, v - head], axis=0)


def _idx_copy(idx_hbm, first_token, buf, sem):
    n = buf.shape[0]
    src = idx_hbm.at[pl.ds(pl.multiple_of(first_token * PEER_NJ, n), n)]
    return pltpu.make_async_copy(src, buf, sem)


def _for_each_group(idx_hbm, ibufs, sems, group):
    step = pl.program_id(0)
    half = PEER_TM // 2
    nu = PEER_UNROLL

    @pl.when(step == 0)
    def _():
        for h in range(2):
            _idx_copy(idx_hbm, h * half, ibufs[h], sems.at[h]).start()

    for h in range(2):
        tok0 = step * PEER_TM + h * half
        _idx_copy(idx_hbm, tok0, ibufs[h], sems.at[h]).wait()
        for g in range(half // nu):
            group(ibufs[h], g * nu * PEER_NJ, h * half + g * nu)

        @pl.when(step + 1 < pl.num_programs(0))
        def _():
            _idx_copy(idx_hbm, tok0 + PEER_TM, ibufs[h], sems.at[h]).start()


def _peer_scratch():
    return ([pltpu.VMEM((4 * TILE_STRIDE + 4, LANES), I32) for _ in range(PEER_UNROLL)]
            + [pltpu.SMEM((PEER_TM // 2 * PEER_NJ,), I32) for _ in range(2)]
            + [pltpu.SemaphoreType.DMA((2,))])


def _peer_u_kernel(idx_hbm, x_ref, gate_ref, u_ref, w_ref, *scratch):
    nj = PEER_NJ
    nu = PEER_UNROLL
    tile_refs, ibufs, sems = scratch[:nu], scratch[nu:nu + 2], scratch[nu + 2]
    row16 = lax.broadcasted_iota(I32, (16, 1), 0) % 8
    even = (lax.broadcasted_iota(I32, (1, 2 * nj), 1) % 2) == 0

    def token(ibuf, off, t, tile_ref):
        _gather_rows(ibuf, off, u_ref, tile_ref)
        x16 = _split_bf16(x_ref[t])
        acc = jnp.zeros((16, 2 * nj), F32)
        for c in range(4):
            lhs = jnp.where((row16 == c) | (row16 == 4 + c), x16, 0.0).astype(BF16)
            acc = acc + _dot_nt(lhs, _tile_chunk(tile_ref, c))
        a = acc[0:8] + acc[8:16]
        return jnp.sum(jnp.where(even, a, pltpu.roll(a, 4, axis=0))[0:4], axis=0, keepdims=True)

    def group(ibuf, off, t0):
        z = jnp.concatenate([token(ibuf, off + i * nj, t0 + i, tile_refs[i]) for i in range(nu)], axis=0)
        act = z + jnp.where(even, pltpu.roll(z, 2 * nj - 1, axis=1), pltpu.roll(z, 1, axis=1))
        gelu = 0.5 * act * (1.0 + lax.erf(act * (2.0 ** -0.5)))
        w_ref[t0:t0 + nu, :] = gate_ref[t0:t0 + nu, :] * gelu

    _for_each_group(idx_hbm, ibufs, sems, group)


def _peer_u(idx4, x3, gate2, table):
    n = x3.shape[0]
    nj = PEER_NJ
    tm = PEER_TM
    return pl.pallas_call(
        _peer_u_kernel,
        grid=(n // tm,),
        in_specs=[pl.BlockSpec(memory_space=pl.ANY),
                  pl.BlockSpec((tm, 8, LANES), lambda i: (i, 0, 0)),
                  pl.BlockSpec((tm, 2 * nj), lambda i: (i, 0)),
                  pl.BlockSpec(memory_space=pltpu.VMEM)],
        out_specs=pl.BlockSpec((tm, 2 * nj), lambda i: (i, 0)),
        out_shape=jax.ShapeDtypeStruct((n, 2 * nj), F32),
        scratch_shapes=_peer_scratch(),
        compiler_params=pltpu.CompilerParams(vmem_limit_bytes=VMEM_LIMIT, dimension_semantics=("arbitrary",)),
        name="peer_u",
    )(idx4, x3, gate2, table)


def _peer_v_kernel(idx_hbm, w_ref, v_ref, o_ref, *scratch):
    nj = PEER_NJ
    nu = PEER_UNROLL
    tile_refs, ibufs, sems = scratch[:nu], scratch[nu:nu + 2], scratch[nu + 2]
    row16 = lax.broadcasted_iota(I32, (16, 1), 0) % 8
    even = (lax.broadcasted_iota(I32, (1, 2 * nj), 1) % 2) == 0

    def token(ibuf, off, t, tile_ref):
        _gather_rows(ibuf, off, v_ref, tile_ref)
        w16 = _split_bf16(jnp.broadcast_to(w_ref[t:t + 1, :], (8, 2 * nj)))
        acc = jnp.zeros((16, LANES), F32)
        for c in range(4):
            keep = ((row16 == c) & even) | ((row16 == 4 + c) & jnp.logical_not(even))
            lhs = jnp.where(keep, w16, 0.0).astype(BF16)
            acc = acc + jnp.dot(lhs, _tile_chunk(tile_ref, c), preferred_element_type=F32)
        o_ref[t] = acc[0:8] + acc[8:16]

    def group(ibuf, off, t0):
        for i in range(nu):
            token(ibuf, off + i * nj, t0 + i, tile_refs[i])

    _for_each_group(idx_hbm, ibufs, sems, group)


def _peer_v(idx4, w2, table):
    n = w2.shape[0]
    nj = PEER_NJ
    tm = PEER_TM
    return pl.pallas_call(
        _peer_v_kernel,
        grid=(n // tm,),
        in_specs=[pl.BlockSpec(memory_space=pl.ANY),
                  pl.BlockSpec((tm, 2 * nj), lambda i: (i, 0)),
                  pl.BlockSpec(memory_space=pltpu.VMEM)],
        out_specs=pl.BlockSpec((tm, 8, LANES), lambda i: (i, 0, 0)),
        out_shape=jax.ShapeDtypeStruct((n, 8, LANES), F32),
        scratch_shapes=_peer_scratch(),
        compiler_params=pltpu.CompilerParams(vmem_limit_bytes=VMEM_LIMIT, dimension_semantics=("arbitrary",)),
        name="peer_v",
    )(idx4, w2, table)


def _final_kernel(x1_ref, y_ref, gt2_ref, g_ref, o_ref):
    o_ref[0] = x1_ref[0] + gt2_ref[0] * _rms(y_ref[0], g_ref[...])


def _final(x1, y, gt2, g, *, tm):
    b, t, d = x1.shape
    tok = pl.BlockSpec((1, tm, d), lambda i, j: (i, j, 0))
    return pl.pallas_call(
        _final_kernel,
        grid=(b, t // tm),
        in_specs=[tok, tok, pl.BlockSpec((1, 1, d), lambda i, j: (i, 0, 0)),
                  pl.BlockSpec((1, d), lambda i, j: (0, 0))],
        out_specs=tok,
        out_shape=jax.ShapeDtypeStruct((b, t, d), F32),
        compiler_params=pltpu.CompilerParams(vmem_limit_bytes=VMEM_LIMIT),
        name="final_residual",
    )(x1, y, gt2, g)


def _rope_tables(t):
    tok = jnp.arange(t)
    row = (tok // GRID_W).astype(F32)
    col = (tok % GRID_W).astype(F32)
    n_freq = DIFF_QK // 4
    inv = ROPE_THETA ** (-jnp.arange(n_freq, dtype=F32) / n_freq)
    ang = jnp.concatenate([row[:, None] * inv, col[:, None] * inv], axis=-1)
    ang2 = jnp.tile(jnp.repeat(ang, 2, axis=1), (1, LANES // DIFF_QK))
    sign = jnp.where(jnp.arange(LANES) % 2 == 0, -1.0, 1.0).astype(F32)
    return jnp.cos(ang2), jnp.sin(ang2) * sign


def _head_major_cols(w):
    d = w.shape[0]
    return w.reshape(d, 2, DIFF_HEADS, DIFF_QK).transpose(0, 2, 1, 3).reshape(d, -1)


def kernel(x, c, ctx, c_ctx, w_ada, b_ada, g_norm, w_in, na_rel_bias, diff_lambda, diff_subln,
           w_out, peer_wq, peer_keys, peer_u, peer_v):
    assert w_ada.shape[0] == 1, "single-layer kernel"
    b, t, d = x.shape
    n_ctx = ctx.shape[1]
    rows = t // GRID_W
    assert t % (NA_QROWS * GRID_W) == 0 and rows >= 3 * NA_QROWS

    pad = (-(b + 1)) % 8
    cc = jnp.concatenate([c, c_ctx[None], jnp.zeros((pad, d), F32)], axis=0)
    mod = _ada(cc, w_ada[0], b_ada[0][None]).reshape(b + 1 + pad, N_MOD, d)
    lat = lambda i: mod[:b, i][:, None, :]
    cxm = lambda i: mod[b:b + 1, i][:, None, :]

    gw = NA_WIDTH
    w_in0 = w_in[0]
    w_perm = jnp.concatenate([w_in0[:, :3 * gw], _head_major_cols(w_in0[:, 3 * gw:4 * gw]),
                              _head_major_cols(w_in0[:, 4 * gw:5 * gw]), w_in0[:, 5 * gw:]], axis=1).astype(BF16)
    cos2, sin2 = _rope_tables(t)
    g0 = g_norm[0, 0][None]
    qa, ka, va, qd, kd, vd = _qkv(x, lat(0), lat(1), g0, w_perm, cos2, sin2,
                                  rope=True, per_batch_mod=True, tm=512)
    _, ka_c, va_c, _, kd_c, vd_c = _qkv(ctx, cxm(0), cxm(1), g0, w_perm, cos2[:n_ctx], sin2[:n_ctx],
                                        rope=False, per_batch_mod=False, tm=n_ctx)

    out_a = _na(qa, ka, va, ka_c, va_c, _na_bias_table(na_rel_bias[0], rows))
    out_d = _diff(diff_lambda[0], qd, kd, vd, kd_c, vd_c, diff_subln[0][None], tq=DIFF_TQ, tk=DIFF_TK)

    keys = peer_keys[0].reshape(2 * PEER_HEADS, PEER_NKEYS, -1).astype(BF16)
    x1, h2, st = _out(out_a, out_d, x, lat(2), lat(3), lat(4), g_norm[0, 1][None], g_norm[0, 2][None],
                      w_out[0].astype(BF16), peer_wq[0].astype(BF16), keys, tm=256)

    idx4, gate2 = _topk(st, tl=256)
    idx4 = idx4.reshape(-1)
    gate2 = gate2.reshape(b * t, 2 * PEER_NJ)

    w2 = _peer_u(idx4, h2.reshape(b * t, 8, LANES), gate2, _pack_table(peer_u[0]))
    y = _peer_v(idx4, w2, _pack_table(peer_v[0])).reshape(b, t, d)
    return _final(x1, y, lat(5), g_norm[0, 3][None], tm=512)
```

```python
import functools
import math

import numpy as np
import jax
import jax.numpy as jnp
from jax import lax
from jax.experimental import pallas as pl
from jax.experimental.pallas import tpu as pltpu

F32 = jnp.float32
BF16 = jnp.bfloat16
I32 = jnp.int32

GRID_W = 64
HEAD_DIM = 64
NA_HEADS = 8
NA_WIDTH = NA_HEADS * HEAD_DIM
NA_WIN_R = 8
NA_WIN_C = 16
DIFF_HEADS = 4
DIFF_QK = HEAD_DIM
DIFF_V = 2 * DIFF_QK
ROPE_THETA = 10000.0
PEER_HEADS = 8
PEER_NKEYS = 128
PEER_TOPK = 16
N_MOD = 6
EPS = 1e-6
LAMBDA_INIT = 0.8 - 0.6 * math.exp(-0.3 * 0)
LOG2E = math.log2(math.e)
DIFF_TQ = 512
DIFF_TK = 512

LANES = 128
VMEM_LIMIT = 56 * 1024 * 1024
NEG = -1e30

NA_QROWS = 8
NA_KROWS = 16
NA_QSUB = 128
PEER_TM = 64
PEER_NJ = PEER_HEADS * PEER_TOPK
TILE_STRIDE = PEER_NJ + 1
PEER_UNROLL = 8


def _dot_nt(a, b):
    return lax.dot_general(a, b, (((1,), (1,)), ((), ())), preferred_element_type=F32)


def _rms(x, g):
    return x * lax.rsqrt(jnp.mean(x * x, axis=-1, keepdims=True) + EPS) * g


def _ada_kernel(c_ref, w_ref, b_ref, o_ref):
    c = c_ref[...]
    s = c / (1.0 + jnp.exp(-c))
    o_ref[...] = jnp.dot(s, w_ref[...], precision=lax.Precision.HIGHEST,
                         preferred_element_type=F32) + b_ref[...]


def _ada(cc, w, b):
    m, d = cc.shape
    n = w.shape[1]
    tn = 1024
    return pl.pallas_call(
        _ada_kernel,
        grid=(n // tn,),
        in_specs=[pl.BlockSpec((m, d), lambda j: (0, 0)),
                  pl.BlockSpec((d, tn), lambda j: (0, j)),
                  pl.BlockSpec((1, tn), lambda j: (0, j))],
        out_specs=pl.BlockSpec((m, tn), lambda j: (0, j)),
        out_shape=jax.ShapeDtypeStruct((m, n), F32),
        compiler_params=pltpu.CompilerParams(vmem_limit_bytes=VMEM_LIMIT),
        name="ada",
    )(cc, w, b)


def _qkv_kernel(x_ref, sh_ref, sc_ref, g_ref, w_ref, cos_ref, sin_ref,
                qa_ref, ka_ref, va_ref, qd_ref, kd_ref, vd_ref, *, rope):
    x = x_ref[0]
    h = _rms(x, g_ref[...]) * (1.0 + sc_ref[0]) + sh_ref[0]
    hb = h.astype(BF16)
    gw = NA_WIDTH
    scale = HEAD_DIM ** -0.5

    def proj(g):
        return jnp.dot(hb, w_ref[:, g * gw:(g + 1) * gw], preferred_element_type=F32)

    def roped(p):
        if not rope:
            return p
        cos2 = cos_ref[...]
        sin2 = sin_ref[...]
        even = (lax.broadcasted_iota(I32, cos2.shape, 1) % 2) == 0
        outs = []
        for c in range(gw // LANES):
            v = p[:, c * LANES:(c + 1) * LANES]
            nxt = pltpu.roll(v, LANES - 1, axis=1)
            prv = pltpu.roll(v, 1, axis=1)
            outs.append(v * cos2 + jnp.where(even, nxt, prv) * sin2)
        return jnp.concatenate(outs, axis=1)

    qa_ref[0] = (proj(0) * scale).astype(BF16)
    ka_ref[0] = proj(1).astype(BF16)
    va_ref[0] = proj(2).astype(BF16)
    qd_ref[0] = (roped(proj(3)) * (scale * LOG2E)).astype(BF16)
    kd_ref[0] = roped(proj(4)).astype(BF16)
    vd = proj(5).astype(BF16)
    ones = jnp.ones((vd.shape[0], DIFF_V), BF16)
    pieces = []
    for hd in range(DIFF_HEADS):
        pieces += [vd[:, hd * DIFF_V:(hd + 1) * DIFF_V], ones]
    vd_ref[0] = jnp.concatenate(pieces, axis=1)


def _qkv(x, sh, sc, g, w, cos2, sin2, *, rope, per_batch_mod, tm):
    b, l, d = x.shape
    n = w.shape[1]
    gw = NA_WIDTH
    mod_map = (lambda i, j: (i, 0, 0)) if per_batch_mod else (lambda i, j: (0, 0, 0))
    out_sds = jax.ShapeDtypeStruct((b, l, gw), BF16)
    out_spec = pl.BlockSpec((1, tm, gw), lambda i, j: (i, j, 0))
    return pl.pallas_call(
        functools.partial(_qkv_kernel, rope=rope),
        grid=(b, l // tm),
        in_specs=[pl.BlockSpec((1, tm, d), lambda i, j: (i, j, 0)),
                  pl.BlockSpec((1, 1, d), mod_map),
                  pl.BlockSpec((1, 1, d), mod_map),
                  pl.BlockSpec((1, d), lambda i, j: (0, 0)),
                  pl.BlockSpec((d, n), lambda i, j: (0, 0)),
                  pl.BlockSpec((tm, LANES), lambda i, j: (j, 0)),
                  pl.BlockSpec((tm, LANES), lambda i, j: (j, 0))],
        out_specs=[out_spec] * 5 + [pl.BlockSpec((1, tm, 2 * gw), lambda i, j: (i, j, 0))],
        out_shape=[out_sds] * 5 + [jax.ShapeDtypeStruct((b, l, 2 * gw), BF16)],
        compiler_params=pltpu.CompilerParams(vmem_limit_bytes=VMEM_LIMIT),
        name="qkv_rope" if rope else "qkv_ctx",
    )(x, sh, sc, g, w, cos2, sin2)


def _na_kernel(q_ref, k_ref, v_ref, kc_ref, vc_ref, bias_ref, o_ref, *, rows):
    rb = pl.program_id(2)
    ks = jnp.clip(rb * NA_QROWS - NA_WIN_R // 2, 0, rows - NA_KROWS)
    start = pl.multiple_of(ks * GRID_W, GRID_W)
    nk = NA_KROWS * GRID_W
    kw = k_ref[0, pl.ds(start, nk), :]
    vw = v_ref[0, pl.ds(start, nk), :]
    kc = kc_ref[0]
    vc = vc_ref[0]
    lane = lax.broadcasted_iota(I32, (1, LANES), 1)
    for s in range(NA_QROWS * GRID_W // NA_QSUB):
        sl = slice(s * NA_QSUB, (s + 1) * NA_QSUB)
        q = q_ref[0, sl, :]
        outs = []
        for hh in range(2):
            qh = jnp.where((lane // HEAD_DIM) == hh, q, jnp.zeros_like(q))
            s_loc = _dot_nt(qh, kw) + bias_ref[0, hh, sl, :]
            s_ctx = _dot_nt(qh, kc)
            mx = jnp.maximum(jnp.max(s_loc, axis=-1, keepdims=True),
                             jnp.max(s_ctx, axis=-1, keepdims=True))
            p_loc = jnp.exp(s_loc - mx)
            p_ctx = jnp.exp(s_ctx - mx)
            l = jnp.sum(p_loc, axis=-1, keepdims=True) + jnp.sum(p_ctx, axis=-1, keepdims=True)
            o = (jnp.dot(p_loc.astype(BF16), vw, preferred_element_type=F32)
                 + jnp.dot(p_ctx.astype(BF16), vc, preferred_element_type=F32))
            outs.append(o / l)
        o_ref[0, sl, :] = jnp.where(lane < HEAD_DIM, outs[0], outs[1]).astype(BF16)


def _na_bias_table(rel_bias, rows):
    nrb = rows // NA_QROWS
    n_ro = 2 * NA_WIN_R - 1
    qc = np.arange(GRID_W)[:, None]
    kc = np.arange(GRID_W)[None, :]
    c_start = np.clip(qc - NA_WIN_C // 2, 0, GRID_W - NA_WIN_C)
    valid_c = (kc >= c_start) & (kc < c_start + NA_WIN_C)
    co = np.clip(kc - qc + NA_WIN_C - 1, 0, 2 * NA_WIN_C - 2)
    tz = jnp.take(rel_bias, jnp.asarray(co.reshape(-1), I32), axis=2).reshape(NA_HEADS, n_ro, GRID_W, GRID_W)
    tz = jnp.where(jnp.asarray(valid_c)[None, None], tz, NEG)
    tz = jnp.concatenate([tz, jnp.full((NA_HEADS, 1, GRID_W, GRID_W), NEG, F32)], axis=1)
    sel = []
    for rb in (0, 1, nrb - 1):
        r0 = rb * NA_QROWS
        ks = int(np.clip(r0 - NA_WIN_R // 2, 0, rows - NA_KROWS))
        qr = r0 + np.arange(NA_QROWS)[:, None]
        kr = ks + np.arange(NA_KROWS)[None, :]
        r_start = np.clip(qr - NA_WIN_R // 2, 0, rows - NA_WIN_R)
        valid_r = (kr >= r_start) & (kr < r_start + NA_WIN_R)
        sel.append(np.where(valid_r, kr - qr + NA_WIN_R - 1, n_ro))
    sel = np.stack(sel).reshape(-1)
    blocks = jnp.take(tz, jnp.asarray(sel, I32), axis=1)
    blocks = blocks.reshape(NA_HEADS, 3, NA_QROWS, NA_KROWS, GRID_W, GRID_W)
    return blocks.transpose(1, 0, 2, 4, 3, 5).reshape(3, NA_HEADS, NA_QROWS * GRID_W, NA_KROWS * GRID_W)


def _na(qa, ka, va, ka_c, va_c, bias):
    b, t, _ = qa.shape
    c = ka_c.shape[1]
    rows = t // GRID_W
    nrb = rows // NA_QROWS
    tq = NA_QROWS * GRID_W
    nk = NA_KROWS * GRID_W

    def bias_map(i, hp, rb):
        pat = jnp.where(rb == 0, 0, jnp.where(rb == nrb - 1, 2, 1))
        return (pat, hp, 0, 0)

    return pl.pallas_call(
        functools.partial(_na_kernel, rows=rows),
        grid=(b, NA_HEADS // 2, nrb),
        in_specs=[pl.BlockSpec((1, tq, LANES), lambda i, hp, rb: (i, rb, hp)),
                  pl.BlockSpec((1, t, LANES), lambda i, hp, rb: (i, 0, hp)),
                  pl.BlockSpec((1, t, LANES), lambda i, hp, rb: (i, 0, hp)),
                  pl.BlockSpec((1, c, LANES), lambda i, hp, rb: (i, 0, hp)),
                  pl.BlockSpec((1, c, LANES), lambda i, hp, rb: (i, 0, hp)),
                  pl.BlockSpec((1, 2, tq, nk), bias_map)],
        out_specs=pl.BlockSpec((1, tq, LANES), lambda i, hp, rb: (i, rb, hp)),
        out_shape=jax.ShapeDtypeStruct((b, t, NA_WIDTH), BF16),
        compiler_params=pltpu.CompilerParams(vmem_limit_bytes=VMEM_LIMIT),
        name="na_attn",
    )(qa, ka, va, ka_c, va_c, bias)


def _diff_kernel(lam_ref, q_ref, k_ref, v_ref, kc_ref, vc_ref, g_ref, o_ref, *, tk):
    q = q_ref[0]
    tq = q.shape[0]
    t = k_ref.shape[1]
    lane = lax.broadcasted_iota(I32, (1, LANES), 1)
    zero = jnp.zeros_like(q)
    qs = (jnp.where(lane < DIFF_QK, q, zero), jnp.where(lane >= DIFF_QK, q, zero))

    def update(kb, vb, carry):
        new = []
        for mi in range(2):
            m, a = carry[mi]
            s = _dot_nt(qs[mi], kb)
            mn = jnp.maximum(m, jnp.max(s, axis=-1, keepdims=True))
            p = jnp.exp2(s - mn)
            a = jnp.exp2(m - mn) * a + jnp.dot(p.astype(BF16), vb, preferred_element_type=F32)
            new.append((mn, a))
        return tuple(new)

    init = tuple((jnp.full((tq, 1), NEG, F32), jnp.zeros((tq, 2 * DIFF_V), F32)) for _ in range(2))
    carry = update(kc_ref[0], vc_ref[0], init)
    for i in range(t // tk):
        carry = update(k_ref[0, i * tk:(i + 1) * tk, :], v_ref[0, i * tk:(i + 1) * tk, :], carry)
    lp = lam_ref[...]
    lam = (jnp.exp(jnp.sum(lp[0:1] * lp[1:2], axis=-1, keepdims=True))
           - jnp.exp(jnp.sum(lp[2:3] * lp[3:4], axis=-1, keepdims=True)) + LAMBDA_INIT)
    (_, a0), (_, a1) = carry
    out = (a0[:, :DIFF_V] / a0[:, DIFF_V:DIFF_V + 1]
           - lam * (a1[:, :DIFF_V] / a1[:, DIFF_V:DIFF_V + 1]))
    o_ref[0] = (_rms(out, g_ref[...]) * (1.0 - LAMBDA_INIT)).astype(BF16)


def _diff(lam_params, qd, kd, vd, kd_c, vd_c, subln, *, tq, tk):
    b, t, w = qd.shape
    c = kd_c.shape[1]
    return pl.pallas_call(
        functools.partial(_diff_kernel, tk=tk),
        grid=(b, DIFF_HEADS, t // tq),
        in_specs=[pl.BlockSpec((4, DIFF_QK), lambda i, h, j: (0, 0)),
                  pl.BlockSpec((1, tq, LANES), lambda i, h, j: (i, j, h)),
                  pl.BlockSpec((1, t, LANES), lambda i, h, j: (i, 0, h)),
                  pl.BlockSpec((1, t, 2 * DIFF_V), lambda i, h, j: (i, 0, h)),
                  pl.BlockSpec((1, c, LANES), lambda i, h, j: (i, 0, h)),
                  pl.BlockSpec((1, c, 2 * DIFF_V), lambda i, h, j: (i, 0, h)),
                  pl.BlockSpec((1, DIFF_V), lambda i, h, j: (0, 0))],
        out_specs=pl.BlockSpec((1, tq, LANES), lambda i, h, j: (i, j, h)),
        out_shape=jax.ShapeDtypeStruct((b, t, w), BF16),
        compiler_params=pltpu.CompilerParams(vmem_limit_bytes=VMEM_LIMIT),
        name="diff_attn",
    )(lam_params, qd, kd, vd, kd_c, vd_c, subln)


def _out_kernel(oa_ref, od_ref, x_ref, gt1_ref, sh2_ref, sc2_ref, g1_ref, g2_ref,
                wout_ref, wq_ref, keys_ref, x1_ref, h2_ref, st_ref):
    wa = NA_WIDTH
    y = (jnp.dot(oa_ref[0], wout_ref[0:wa, :], preferred_element_type=F32)
         + jnp.dot(od_ref[0], wout_ref[wa:, :], preferred_element_type=F32))
    x1 = x_ref[0] + gt1_ref[0] * _rms(y, g1_ref[...])
    x1_ref[0] = x1
    h2 = _rms(x1, g2_ref[...]) * (1.0 + sc2_ref[0]) + sh2_ref[0]
    h2_ref[0] = h2
    qb = jnp.dot(h2.astype(BF16), wq_ref[...], preferred_element_type=F32).astype(BF16)
    for hp in range(2 * PEER_HEADS):
        st_ref[0, hp] = _dot_nt(keys_ref[hp], qb[:, hp * LANES:(hp + 1) * LANES])


def _out(oa, od, x, gt1, sh2, sc2, g1, g2, wout, wq, keys, *, tm):
    b, t, d = x.shape
    nq = wq.shape[1]
    nhp = keys.shape[0]
    mod = pl.BlockSpec((1, 1, d), lambda i, j: (i, 0, 0))
    gsp = pl.BlockSpec((1, d), lambda i, j: (0, 0))
    tok = pl.BlockSpec((1, tm, d), lambda i, j: (i, j, 0))
    half = pl.BlockSpec((1, tm, NA_WIDTH), lambda i, j: (i, j, 0))
    return pl.pallas_call(
        _out_kernel,
        grid=(b, t // tm),
        in_specs=[half, half, tok, mod, mod, mod, gsp, gsp,
                  pl.BlockSpec((d, d), lambda i, j: (0, 0)),
                  pl.BlockSpec((d, nq), lambda i, j: (0, 0)),
                  pl.BlockSpec((nhp, PEER_NKEYS, LANES), lambda i, j: (0, 0, 0))],
        out_specs=[tok, tok, pl.BlockSpec((1, nhp, PEER_NKEYS, tm), lambda i, j: (i, 0, 0, j))],
        out_shape=[jax.ShapeDtypeStruct((b, t, d), F32), jax.ShapeDtypeStruct((b, t, d), F32),
                   jax.ShapeDtypeStruct((b, nhp, PEER_NKEYS, t), F32)],
        compiler_params=pltpu.CompilerParams(vmem_limit_bytes=VMEM_LIMIT),
        name="out_proj_peer_scores",
    )(oa, od, x, gt1, sh2, sc2, g1, g2, wout, wq, keys)


def _top1(s, iota):
    m = jnp.max(s, axis=0, keepdims=True)
    idx = jnp.min(jnp.where(s == m, iota, s.shape[0]), axis=0, keepdims=True)
    return m, idx, jnp.where(iota == idx, -jnp.inf, s)


def _topk_kernel(st_ref, idx_ref, gate_ref):
    tl = st_ref.shape[-1]
    k = PEER_TOPK
    iota_n = lax.broadcasted_iota(I32, (PEER_NKEYS, tl), 0)
    widths = [k // (a + 1) for a in range(k)]
    n_cand = -(-sum(widths) // 8) * 8
    iota_c = lax.broadcasted_iota(I32, (n_cand, tl), 0)
    pad = n_cand - sum(widths)
    gates, rows = [], []
    for h in range(PEER_HEADS):
        tops = []
        for p in range(2):
            s = st_ref[0, 2 * h + p]
            vals, idxs = [], []
            for _ in range(k):
                m, i, s = _top1(s, iota_n)
                vals.append(m)
                idxs.append(i)
            tops.append((jnp.concatenate(vals, axis=0), jnp.concatenate(idxs, axis=0)))
        (s0, i0), (s1, i1) = tops
        cand = jnp.concatenate([s0[a:a + 1] + s1[0:widths[a]] for a in range(k)]
                               + [jnp.full((pad, tl), -jnp.inf, F32)], axis=0)
        cidx = jnp.concatenate([i0[a:a + 1] * PEER_NKEYS + i1[0:widths[a]] for a in range(k)]
                               + [jnp.zeros((pad, tl), I32)], axis=0)
        best, experts = [], []
        for _ in range(k):
            m, pos, cand = _top1(cand, iota_c)
            best.append(m)
            experts.append(jnp.sum(jnp.where(iota_c == pos, cidx, 0), axis=0, keepdims=True))
        best = jnp.concatenate(best, axis=0)
        e = jnp.exp(best - best[0:1])
        gates.append(e / jnp.sum(e, axis=0, keepdims=True))
        rows += experts
    idx_ref[0] = (jnp.concatenate(rows, axis=0) * 4).T
    nj = PEER_NJ
    dup = (lax.broadcasted_iota(I32, (nj, 2 * nj), 1) // 2 == lax.broadcasted_iota(I32, (nj, 2 * nj), 0))
    gate_ref[0] = jnp.dot(jnp.concatenate(gates, axis=0).T, dup.astype(F32),
                          precision=lax.Precision.HIGHEST, preferred_element_type=F32)


def _topk(st, *, tl):
    b, nhp, nkeys, t = st.shape
    nj = PEER_NJ
    return pl.pallas_call(
        _topk_kernel,
        grid=(b, t // tl),
        in_specs=[pl.BlockSpec((1, nhp, nkeys, tl), lambda i, j: (i, 0, 0, j))],
        out_specs=[pl.BlockSpec((1, tl, nj), lambda i, j: (i, j, 0)),
                   pl.BlockSpec((1, tl, 2 * nj), lambda i, j: (i, j, 0))],
        out_shape=[jax.ShapeDtypeStruct((b, t, nj), I32), jax.ShapeDtypeStruct((b, t, 2 * nj), F32)],
        compiler_params=pltpu.CompilerParams(vmem_limit_bytes=VMEM_LIMIT),
        name="peer_topk",
    )(st)


def _pack_table(tab):
    e, d = tab.shape
    tb = tab.astype(BF16)
    lo = lax.bitcast_convert_type(tb[:, :d // 2], jnp.uint16).astype(jnp.uint32)
    hi = lax.bitcast_convert_type(tb[:, d // 2:], jnp.uint16).astype(jnp.uint32)
    words = lax.bitcast_convert_type((hi << 16) | lo, I32)
    return words.reshape(e * (d // 2 // LANES), LANES)


def _gather_rows(idx_ref, base, table_ref, tile_ref):
    ahead = 4
    rows = [idx_ref[base + j] for j in range(ahead)]
    for j in range(0, PEER_NJ, 2):
        for k in range(2):
            if j + k + ahead < PEER_NJ:
                rows.append(idx_ref[base + j + k + ahead])
        a = table_ref[pl.ds(pl.multiple_of(rows[j], 4), 4), :]
        b = table_ref[pl.ds(pl.multiple_of(rows[j + 1], 4), 4), :]
        tile_ref[4 * j:4 * j + 8, :] = jnp.concatenate([a, b], axis=0)


def _tile_chunk(tile_ref, c):
    return pltpu.bitcast(tile_ref[pl.ds(c, PEER_NJ, stride=4), :], BF16)


def _split_bf16(v):
    head = v.astype(BF16).astype(F32)
    return jnp.concatenate([head, v - head], axis=0)


def _idx_copy(idx_hbm, first_token, buf, sem):
    n = buf.shape[0]
    src = idx_hbm.at[pl.ds(pl.multiple_of(first_token * PEER_NJ, n), n)]
    return pltpu.make_async_copy(src, buf, sem)


def _for_each_group(idx_hbm, ibufs, sems, group):
    step = pl.program_id(0)
    half = PEER_TM // 2
    nu = PEER_UNROLL

    @pl.when(step == 0)
    def _():
        for h in range(2):
            _idx_copy(idx_hbm, h * half, ibufs[h], sems.at[h]).start()

    for h in range(2):
        tok0 = step * PEER_TM + h * half
        _idx_copy(idx_hbm, tok0, ibufs[h], sems.at[h]).wait()
        for g in range(half // nu):
            group(ibufs[h], g * nu * PEER_NJ, h * half + g * nu)

        @pl.when(step + 1 < pl.num_programs(0))
        def _():
            _idx_copy(idx_hbm, tok0 + PEER_TM, ibufs[h], sems.at[h]).start()


def _peer_scratch():
    return ([pltpu.VMEM((4 * TILE_STRIDE + 4, LANES), I32) for _ in range(PEER_UNROLL)]
            + [pltpu.SMEM((PEER_TM // 2 * PEER_NJ,), I32) for _ in range(2)]
            + [pltpu.SemaphoreType.DMA((2,))])


def _peer_u_kernel(idx_hbm, x_ref, gate_ref, u_ref, w_ref, *scratch):
    nj = PEER_NJ
    nu = PEER_UNROLL
    tile_refs, ibufs, sems = scratch[:nu], scratch[nu:nu + 2], scratch[nu + 2]
    row16 = lax.broadcasted_iota(I32, (16, 1), 0) % 8
    even = (lax.broadcasted_iota(I32, (1, 2 * nj), 1) % 2) == 0

    def token(ibuf, off, t, tile_ref):
        _gather_rows(ibuf, off, u_ref, tile_ref)
        x16 = _split_bf16(x_ref[t])
        acc = jnp.zeros((16, 2 * nj), F32)
        for c in range(4):
            lhs = jnp.where((row16 == c) | (row16 == 4 + c), x16, 0.0).astype(BF16)
            acc = acc + _dot_nt(lhs, _tile_chunk(tile_ref, c))
        a = acc[0:8] + acc[8:16]
        return jnp.sum(jnp.where(even, a, pltpu.roll(a, 4, axis=0))[0:4], axis=0, keepdims=True)

    def group(ibuf, off, t0):
        z = jnp.concatenate([token(ibuf, off + i * nj, t0 + i, tile_refs[i]) for i in range(nu)], axis=0)
        act = z + jnp.where(even, pltpu.roll(z, 2 * nj - 1, axis=1), pltpu.roll(z, 1, axis=1))
        gelu = 0.5 * act * (1.0 + lax.erf(act * (2.0 ** -0.5)))
        w_ref[t0:t0 + nu, :] = gate_ref[t0:t0 + nu, :] * gelu

    _for_each_group(idx_hbm, ibufs, sems, group)


def _peer_u(idx4, x3, gate2, table):
    n = x3.shape[0]
    nj = PEER_NJ
    tm = PEER_TM
    return pl.pallas_call(
        _peer_u_kernel,
        grid=(n // tm,),
        in_specs=[pl.BlockSpec(memory_space=pl.ANY),
                  pl.BlockSpec((tm, 8, LANES), lambda i: (i, 0, 0)),
                  pl.BlockSpec((tm, 2 * nj), lambda i: (i, 0)),
                  pl.BlockSpec(memory_space=pltpu.VMEM)],
        out_specs=pl.BlockSpec((tm, 2 * nj), lambda i: (i, 0)),
        out_shape=jax.ShapeDtypeStruct((n, 2 * nj), F32),
        scratch_shapes=_peer_scratch(),
        compiler_params=pltpu.CompilerParams(vmem_limit_bytes=VMEM_LIMIT, dimension_semantics=("arbitrary",)),
        name="peer_u",
    )(idx4, x3, gate2, table)


def _peer_v_kernel(idx_hbm, w_ref, v_ref, o_ref, *scratch):
    nj = PEER_NJ
    nu = PEER_UNROLL
    tile_refs, ibufs, sems = scratch[:nu], scratch[nu:nu + 2], scratch[nu + 2]
    row16 = lax.broadcasted_iota(I32, (16, 1), 0) % 8
    even = (lax.broadcasted_iota(I32, (1, 2 * nj), 1) % 2) == 0

    def token(ibuf, off, t, tile_ref):
        _gather_rows(ibuf, off, v_ref, tile_ref)
        w16 = _split_bf16(jnp.broadcast_to(w_ref[t:t + 1, :], (8, 2 * nj)))
        acc = jnp.zeros((16, LANES), F32)
        for c in range(4):
            keep = ((row16 == c) & even) | ((row16 == 4 + c) & jnp.logical_not(even))
            lhs = jnp.where(keep, w16, 0.0).astype(BF16)
            acc = acc + jnp.dot(lhs, _tile_chunk(tile_ref, c), preferred_element_type=F32)
        o_ref[t] = acc[0:8] + acc[8:16]

    def group(ibuf, off, t0):
        for i in range(nu):
            token(ibuf, off + i * nj, t0 + i, tile_refs[i])

    _for_each_group(idx_hbm, ibufs, sems, group)


def _peer_v(idx4, w2, table):
    n = w2.shape[0]
    nj = PEER_NJ
    tm = PEER_TM
    return pl.pallas_call(
        _peer_v_kernel,
        grid=(n // tm,),
        in_specs=[pl.BlockSpec(memory_space=pl.ANY),
                  pl.BlockSpec((tm, 2 * nj), lambda i: (i, 0)),
                  pl.BlockSpec(memory_space=pltpu.VMEM)],
        out_specs=pl.BlockSpec((tm, 8, LANES), lambda i: (i, 0, 0)),
        out_shape=jax.ShapeDtypeStruct((n, 8, LANES), F32),
        scratch_shapes=_peer_scratch(),
        compiler_params=pltpu.CompilerParams(vmem_limit_bytes=VMEM_LIMIT, dimension_semantics=("arbitrary",)),
        name="peer_v",
    )(idx4, w2, table)


def _final_kernel(x1_ref, y_ref, gt2_ref, g_ref, o_ref):
    o_ref[0] = x1_ref[0] + gt2_ref[0] * _rms(y_ref[0], g_ref[...])


def _final(x1, y, gt2, g, *, tm):
    b, t, d = x1.shape
    tok = pl.BlockSpec((1, tm, d), lambda i, j: (i, j, 0))
    return pl.pallas_call(
        _final_kernel,
        grid=(b, t // tm),
        in_specs=[tok, tok, pl.BlockSpec((1, 1, d), lambda i, j: (i, 0, 0)),
                  pl.BlockSpec((1, d), lambda i, j: (0, 0))],
        out_specs=tok,
        out_shape=jax.ShapeDtypeStruct((b, t, d), F32),
        compiler_params=pltpu.CompilerParams(vmem_limit_bytes=VMEM_LIMIT),
        name="final_residual",
    )(x1, y, gt2, g)


def _rope_tables(t):
    tok = jnp.arange(t)
    row = (tok // GRID_W).astype(F32)
    col = (tok % GRID_W).astype(F32)
    n_freq = DIFF_QK // 4
    inv = ROPE_THETA ** (-jnp.arange(n_freq, dtype=F32) / n_freq)
    ang = jnp.concatenate([row[:, None] * inv, col[:, None] * inv], axis=-1)
    ang2 = jnp.tile(jnp.repeat(ang, 2, axis=1), (1, LANES // DIFF_QK))
    sign = jnp.where(jnp.arange(LANES) % 2 == 0, -1.0, 1.0).astype(F32)
    return jnp.cos(ang2), jnp.sin(ang2) * sign


def _head_major_cols(w):
    d = w.shape[0]
    return w.reshape(d, 2, DIFF_HEADS, DIFF_QK).transpose(0, 2, 1, 3).reshape(d, -1)


def kernel(x, c, ctx, c_ctx, w_ada, b_ada, g_norm, w_in, na_rel_bias, diff_lambda, diff_subln,
           w_out, peer_wq, peer_keys, peer_u, peer_v):
    assert w_ada.shape[0] == 1, "single-layer kernel"
    b, t, d = x.shape
    n_ctx = ctx.shape[1]
    rows = t // GRID_W
    assert t % (NA_QROWS * GRID_W) == 0 and rows >= 3 * NA_QROWS

    pad = (-(b + 1)) % 8
    cc = jnp.concatenate([c, c_ctx[None], jnp.zeros((pad, d), F32)], axis=0)
    mod = _ada(cc, w_ada[0], b_ada[0][None]).reshape(b + 1 + pad, N_MOD, d)
    lat = lambda i: mod[:b, i][:, None, :]
    cxm = lambda i: mod[b:b + 1, i][:, None, :]

    gw = NA_WIDTH
    w_in0 = w_in[0]
    w_perm = jnp.concatenate([w_in0[:, :3 * gw], _head_major_cols(w_in0[:, 3 * gw:4 * gw]),
                              _head_major_cols(w_in0[:, 4 * gw:5 * gw]), w_in0[:, 5 * gw:]], axis=1).astype(BF16)
    cos2, sin2 = _rope_tables(t)
    g0 = g_norm[0, 0][None]
    qa, ka, va, qd, kd, vd = _qkv(x, lat(0), lat(1), g0, w_perm, cos2, sin2,
                                  rope=True, per_batch_mod=True, tm=512)
    _, ka_c, va_c, _, kd_c, vd_c = _qkv(ctx, cxm(0), cxm(1), g0, w_perm, cos2[:n_ctx], sin2[:n_ctx],
                                        rope=False, per_batch_mod=False, tm=n_ctx)

    out_a = _na(qa, ka, va, ka_c, va_c, _na_bias_table(na_rel_bias[0], rows))
    out_d = _diff(diff_lambda[0], qd, kd, vd, kd_c, vd_c, diff_subln[0][None], tq=DIFF_TQ, tk=DIFF_TK)

    keys = peer_keys[0].reshape(2 * PEER_HEADS, PEER_NKEYS, -1).astype(BF16)
    x1, h2, st = _out(out_a, out_d, x, lat(2), lat(3), lat(4), g_norm[0, 1][None], g_norm[0, 2][None],
                      w_out[0].astype(BF16), peer_wq[0].astype(BF16), keys, tm=256)

    idx4, gate2 = _topk(st, tl=256)
    idx4 = idx4.reshape(-1)
    gate2 = gate2.reshape(b * t, 2 * PEER_NJ)

    w2 = _peer_u(idx4, h2.reshape(b * t, 8, LANES), gate2, _pack_table(peer_u[0]))
    y = _peer_v(idx4, w2, _pack_table(peer_v[0])).reshape(b, t, d)
    return _final(x1, y, lat(5), g_norm[0, 3][None], tm=512)
```

```python
import functools
import math

import numpy as np
import jax
import jax.numpy as jnp
from jax import lax
from jax.experimental import pallas as pl
from jax.experimental.pallas import tpu as pltpu

F32 = jnp.float32
BF16 = jnp.bfloat16
I32 = jnp.int32

GRID_W = 64
HEAD_DIM = 64
NA_HEADS = 8
NA_WIDTH = NA_HEADS * HEAD_DIM
NA_WIN_R = 8
NA_WIN_C = 16
DIFF_HEADS = 4
DIFF_QK = HEAD_DIM
DIFF_V = 2 * DIFF_QK
ROPE_THETA = 10000.0
PEER_HEADS = 8
PEER_NKEYS = 128
PEER_TOPK = 16
N_MOD = 6
EPS = 1e-6
LAMBDA_INIT = 0.8 - 0.6 * math.exp(-0.3 * 0)
LOG2E = math.log2(math.e)
DIFF_TQ = 512
DIFF_TK = 512

LANES = 128
VMEM_LIMIT = 56 * 1024 * 1024
NEG = -1e30

NA_QROWS = 8
NA_QSUBROWS = 2
NA_QSUB = NA_QSUBROWS * GRID_W
NA_KROWS = 10
PEER_TM = 64
PEER_NJ = PEER_HEADS * PEER_TOPK
TILE_STRIDE = PEER_NJ + 1
PEER_UNROLL = 8


def _dot_nt(a, b):
    return lax.dot_general(a, b, (((1,), (1,)), ((), ())), preferred_element_type=F32)


def _rms(x, g):
    return x * lax.rsqrt(jnp.mean(x * x, axis=-1, keepdims=True) + EPS) * g


def _ada_kernel(c_ref, w_ref, b_ref, o_ref):
    c = c_ref[...]
    s = c / (1.0 + jnp.exp(-c))
    o_ref[...] = jnp.dot(s, w_ref[...], precision=lax.Precision.HIGHEST,
                         preferred_element_type=F32) + b_ref[...]


def _ada(cc, w, b):
    m, d = cc.shape
    n = w.shape[1]
    tn = 1024
    return pl.pallas_call(
        _ada_kernel,
        grid=(n // tn,),
        in_specs=[pl.BlockSpec((m, d), lambda j: (0, 0)),
                  pl.BlockSpec((d, tn), lambda j: (0, j)),
                  pl.BlockSpec((1, tn), lambda j: (0, j))],
        out_specs=pl.BlockSpec((m, tn), lambda j: (0, j)),
        out_shape=jax.ShapeDtypeStruct((m, n), F32),
        compiler_params=pltpu.CompilerParams(vmem_limit_bytes=VMEM_LIMIT),
        name="ada",
    )(cc, w, b)


def _qkv_kernel(x_ref, sh_ref, sc_ref, g_ref, w_ref, cos_ref, sin_ref,
                qa_ref, ka_ref, va_ref, qd_ref, kd_ref, vd_ref, *, rope):
    x = x_ref[0]
    h = _rms(x, g_ref[...]) * (1.0 + sc_ref[0]) + sh_ref[0]
    hb = h.astype(BF16)
    gw = NA_WIDTH
    scale = HEAD_DIM ** -0.5

    def proj(g):
        return jnp.dot(hb, w_ref[:, g * gw:(g + 1) * gw], preferred_element_type=F32)

    def roped(p):
        if not rope:
            return p
        cos2 = cos_ref[...]
        sin2 = sin_ref[...]
        even = (lax.broadcasted_iota(I32, cos2.shape, 1) % 2) == 0
        outs = []
        for c in range(gw // LANES):
            v = p[:, c * LANES:(c + 1) * LANES]
            nxt = pltpu.roll(v, LANES - 1, axis=1)
            prv = pltpu.roll(v, 1, axis=1)
            outs.append(v * cos2 + jnp.where(even, nxt, prv) * sin2)
        return jnp.concatenate(outs, axis=1)

    qa_ref[0] = (proj(0) * scale).astype(BF16)
    ka_ref[0] = proj(1).astype(BF16)
    va_ref[0] = proj(2).astype(BF16)
    qd_ref[0] = (roped(proj(3)) * (scale * LOG2E)).astype(BF16)
    kd_ref[0] = roped(proj(4)).astype(BF16)
    vd = proj(5).astype(BF16)
    ones = jnp.ones((vd.shape[0], DIFF_V), BF16)
    pieces = []
    for hd in range(DIFF_HEADS):
        pieces += [vd[:, hd * DIFF_V:(hd + 1) * DIFF_V], ones]
    vd_ref[0] = jnp.concatenate(pieces, axis=1)


def _qkv(x, sh, sc, g, w, cos2, sin2, *, rope, per_batch_mod, tm):
    b, l, d = x.shape
    n = w.shape[1]
    gw = NA_WIDTH
    mod_map = (lambda i, j: (i, 0, 0)) if per_batch_mod else (lambda i, j: (0, 0, 0))
    out_sds = jax.ShapeDtypeStruct((b, l, gw), BF16)
    out_spec = pl.BlockSpec((1, tm, gw), lambda i, j: (i, j, 0))
    return pl.pallas_call(
        functools.partial(_qkv_kernel, rope=rope),
        grid=(b, l // tm),
        in_specs=[pl.BlockSpec((1, tm, d), lambda i, j: (i, j, 0)),
                  pl.BlockSpec((1, 1, d), mod_map),
                  pl.BlockSpec((1, 1, d), mod_map),
                  pl.BlockSpec((1, d), lambda i, j: (0, 0)),
                  pl.BlockSpec((d, n), lambda i, j: (0, 0)),
                  pl.BlockSpec((tm, LANES), lambda i, j: (j, 0)),
                  pl.BlockSpec((tm, LANES), lambda i, j: (j, 0))],
        out_specs=[out_spec] * 5 + [pl.BlockSpec((1, tm, 2 * gw), lambda i, j: (i, j, 0))],
        out_shape=[out_sds] * 5 + [jax.ShapeDtypeStruct((b, l, 2 * gw), BF16)],
        compiler_params=pltpu.CompilerParams(vmem_limit_bytes=VMEM_LIMIT),
        name="qkv_rope" if rope else "qkv_ctx",
    )(x, sh, sc, g, w, cos2, sin2)


def _na_window(qr0, rows):
    r_start = jnp.clip(qr0 - NA_WIN_R // 2, 0, rows - NA_WIN_R)
    return jnp.minimum(r_start, rows - NA_KROWS)


def _na_kernel(pid_ref, q_ref, k_ref, v_ref, kc_ref, vc_ref, bias_ref, o_ref, *, rows):
    rb = pl.program_id(2)
    nk = NA_KROWS * GRID_W
    kc = kc_ref[0]
    vc = vc_ref[0]
    lane = lax.broadcasted_iota(I32, (1, LANES), 1)
    nsub = NA_QROWS // NA_QSUBROWS
    for s in range(nsub):
        sl = slice(s * NA_QSUB, (s + 1) * NA_QSUB)
        start = pl.multiple_of(_na_window(rb * NA_QROWS + s * NA_QSUBROWS, rows) * GRID_W, GRID_W)
        kw = k_ref[0, pl.ds(start, nk), :]
        vw = v_ref[0, pl.ds(start, nk), :]
        pid = pid_ref[rb * nsub + s]
        q = q_ref[0, sl, :]
        outs = []
        for hh in range(2):
            qh = jnp.where((lane // HEAD_DIM) == hh, q, jnp.zeros_like(q))
            s_loc = _dot_nt(qh, kw) + bias_ref[pid, hh]
            s_ctx = _dot_nt(qh, kc)
            mx = jnp.maximum(jnp.max(s_loc, axis=-1, keepdims=True),
                             jnp.max(s_ctx, axis=-1, keepdims=True))
            p_loc = jnp.exp(s_loc - mx)
            p_ctx = jnp.exp(s_ctx - mx)
            l = jnp.sum(p_loc, axis=-1, keepdims=True) + jnp.sum(p_ctx, axis=-1, keepdims=True)
            o = (jnp.dot(p_loc.astype(BF16), vw, preferred_element_type=F32)
                 + jnp.dot(p_ctx.astype(BF16), vc, preferred_element_type=F32))
            outs.append(o / l)
        o_ref[0, sl, :] = jnp.where(lane < HEAD_DIM, outs[0], outs[1]).astype(BF16)


def _na_bias_table(rel_bias, rows):
    n_ro = 2 * NA_WIN_R - 1
    qc = np.arange(GRID_W)[:, None]
    kc = np.arange(GRID_W)[None, :]
    c_start = np.clip(qc - NA_WIN_C // 2, 0, GRID_W - NA_WIN_C)
    valid_c = (kc >= c_start) & (kc < c_start + NA_WIN_C)
    co = np.clip(kc - qc + NA_WIN_C - 1, 0, 2 * NA_WIN_C - 2)
    tz = jnp.take(rel_bias, jnp.asarray(co.reshape(-1), I32), axis=2).reshape(NA_HEADS, n_ro, GRID_W, GRID_W)
    tz = jnp.where(jnp.asarray(valid_c)[None, None], tz, NEG)
    tz = jnp.concatenate([tz, jnp.full((NA_HEADS, 1, GRID_W, GRID_W), NEG, F32)], axis=1)
    patterns, pid = [], []
    for qr0 in range(0, rows, NA_QSUBROWS):
        ws = min(int(np.clip(qr0 - NA_WIN_R // 2, 0, rows - NA_WIN_R)), rows - NA_KROWS)
        qr = qr0 + np.arange(NA_QSUBROWS)[:, None]
        kr = ws + np.arange(NA_KROWS)[None, :]
        r_start = np.clip(qr - NA_WIN_R // 2, 0, rows - NA_WIN_R)
        valid_r = (kr >= r_start) & (kr < r_start + NA_WIN_R)
        assert valid_r.sum(axis=1).min() == NA_WIN_R
        sel = tuple(np.where(valid_r, kr - qr + NA_WIN_R - 1, n_ro).reshape(-1))
        if sel not in patterns:
            patterns.append(sel)
        pid.append(patterns.index(sel))
    sel = np.asarray(patterns, np.int32).reshape(-1)
    blocks = jnp.take(tz, jnp.asarray(sel), axis=1)
    blocks = blocks.reshape(NA_HEADS, len(patterns), NA_QSUBROWS, NA_KROWS, GRID_W, GRID_W)
    table = blocks.transpose(1, 0, 2, 4, 3, 5).reshape(len(patterns), NA_HEADS, NA_QSUB, NA_KROWS * GRID_W)
    return table, np.asarray(pid, np.int32)


def _na(qa, ka, va, ka_c, va_c, bias, pid):
    b, t, _ = qa.shape
    c = ka_c.shape[1]
    rows = t // GRID_W
    nrb = rows // NA_QROWS
    tq = NA_QROWS * GRID_W
    nk = NA_KROWS * GRID_W
    npat = bias.shape[0]
    return pl.pallas_call(
        functools.partial(_na_kernel, rows=rows),
        grid=(b, NA_HEADS // 2, nrb),
        in_specs=[pl.BlockSpec(memory_space=pltpu.SMEM),
                  pl.BlockSpec((1, tq, LANES), lambda i, hp, rb: (i, rb, hp)),
                  pl.BlockSpec((1, t, LANES), lambda i, hp, rb: (i, 0, hp)),
                  pl.BlockSpec((1, t, LANES), lambda i, hp, rb: (i, 0, hp)),
                  pl.BlockSpec((1, c, LANES), lambda i, hp, rb: (i, 0, hp)),
                  pl.BlockSpec((1, c, LANES), lambda i, hp, rb: (i, 0, hp)),
                  pl.BlockSpec((npat, 2, NA_QSUB, nk), lambda i, hp, rb: (0, hp, 0, 0))],
        out_specs=pl.BlockSpec((1, tq, LANES), lambda i, hp, rb: (i, rb, hp)),
        out_shape=jax.ShapeDtypeStruct((b, t, NA_WIDTH), BF16),
        compiler_params=pltpu.CompilerParams(vmem_limit_bytes=VMEM_LIMIT),
        name="na_attn",
    )(jnp.asarray(pid), qa, ka, va, ka_c, va_c, bias)


def _diff_kernel(lam_ref, q_ref, k_ref, v_ref, kc_ref, vc_ref, g_ref, o_ref, *, tk):
    q = q_ref[0]
    tq = q.shape[0]
    t = k_ref.shape[1]
    lane = lax.broadcasted_iota(I32, (1, LANES), 1)
    zero = jnp.zeros_like(q)
    qs = (jnp.where(lane < DIFF_QK, q, zero), jnp.where(lane >= DIFF_QK, q, zero))

    def update(kb, vb, carry):
        new = []
        for mi in range(2):
            m, a = carry[mi]
            s = _dot_nt(qs[mi], kb)
            mn = jnp.maximum(m, jnp.max(s, axis=-1, keepdims=True))
            p = jnp.exp2(s - mn)
            a = jnp.exp2(m - mn) * a + jnp.dot(p.astype(BF16), vb, preferred_element_type=F32)
            new.append((mn, a))
        return tuple(new)

    init = tuple((jnp.full((tq, 1), NEG, F32), jnp.zeros((tq, 2 * DIFF_V), F32)) for _ in range(2))
    carry = update(kc_ref[0], vc_ref[0], init)
    for i in range(t // tk):
        carry = update(k_ref[0, i * tk:(i + 1) * tk, :], v_ref[0, i * tk:(i + 1) * tk, :], carry)
    lp = lam_ref[...]
    lam = (jnp.exp(jnp.sum(lp[0:1] * lp[1:2], axis=-1, keepdims=True))
           - jnp.exp(jnp.sum(lp[2:3] * lp[3:4], axis=-1, keepdims=True)) + LAMBDA_INIT)
    (_, a0), (_, a1) = carry
    out = (a0[:, :DIFF_V] / a0[:, DIFF_V:DIFF_V + 1]
           - lam * (a1[:, :DIFF_V] / a1[:, DIFF_V:DIFF_V + 1]))
    o_ref[0] = (_rms(out, g_ref[...]) * (1.0 - LAMBDA_INIT)).astype(BF16)


def _diff(lam_params, qd, kd, vd, kd_c, vd_c, subln, *, tq, tk):
    b, t, w = qd.shape
    c = kd_c.shape[1]
    return pl.pallas_call(
        functools.partial(_diff_kernel, tk=tk),
        grid=(b, DIFF_HEADS, t // tq),
        in_specs=[pl.BlockSpec((4, DIFF_QK), lambda i, h, j: (0, 0)),
                  pl.BlockSpec((1, tq, LANES), lambda i, h, j: (i, j, h)),
                  pl.BlockSpec((1, t, LANES), lambda i, h, j: (i, 0, h)),
                  pl.BlockSpec((1, t, 2 * DIFF_V), lambda i, h, j: (i, 0, h)),
                  pl.BlockSpec((1, c, LANES), lambda i, h, j: (i, 0, h)),
                  pl.BlockSpec((1, c, 2 * DIFF_V), lambda i, h, j: (i, 0, h)),
                  pl.BlockSpec((1, DIFF_V), lambda i, h, j: (0, 0))],
        out_specs=pl.BlockSpec((1, tq, LANES), lambda i, h, j: (i, j, h)),
        out_shape=jax.ShapeDtypeStruct((b, t, w), BF16),
        compiler_params=pltpu.CompilerParams(vmem_limit_bytes=VMEM_LIMIT),
        name="diff_attn",
    )(lam_params, qd, kd, vd, kd_c, vd_c, subln)


def _out_kernel(oa_ref, od_ref, x_ref, gt1_ref, sh2_ref, sc2_ref, g1_ref, g2_ref,
                wout_ref, wq_ref, keys_ref, x1_ref, h2_ref, st_ref):
    wa = NA_WIDTH
    y = (jnp.dot(oa_ref[0], wout_ref[0:wa, :], preferred_element_type=F32)
         + jnp.dot(od_ref[0], wout_ref[wa:, :], preferred_element_type=F32))
    x1 = x_ref[0] + gt1_ref[0] * _rms(y, g1_ref[...])
    x1_ref[0] = x1
    h2 = _rms(x1, g2_ref[...]) * (1.0 + sc2_ref[0]) + sh2_ref[0]
    h2_ref[...] = h2.reshape(h2_ref.shape)
    qb = jnp.dot(h2.astype(BF16), wq_ref[...], preferred_element_type=F32).astype(BF16)
    for hp in range(2 * PEER_HEADS):
        st_ref[0, hp] = _dot_nt(keys_ref[hp], qb[:, hp * LANES:(hp + 1) * LANES])


def _out(oa, od, x, gt1, sh2, sc2, g1, g2, wout, wq, keys, *, tm):
    b, t, d = x.shape
    nq = wq.shape[1]
    nhp = keys.shape[0]
    mod = pl.BlockSpec((1, 1, d), lambda i, j: (i, 0, 0))
    gsp = pl.BlockSpec((1, d), lambda i, j: (0, 0))
    tok = pl.BlockSpec((1, tm, d), lambda i, j: (i, j, 0))
    half = pl.BlockSpec((1, tm, NA_WIDTH), lambda i, j: (i, j, 0))
    return pl.pallas_call(
        _out_kernel,
        grid=(b, t // tm),
        in_specs=[half, half, tok, mod, mod, mod, gsp, gsp,
                  pl.BlockSpec((d, d), lambda i, j: (0, 0)),
                  pl.BlockSpec((d, nq), lambda i, j: (0, 0)),
                  pl.BlockSpec((nhp, PEER_NKEYS, LANES), lambda i, j: (0, 0, 0))],
        out_specs=[tok, pl.BlockSpec((tm, d // LANES, LANES), lambda i, j: (i * (t // tm) + j, 0, 0)),
                   pl.BlockSpec((1, nhp, PEER_NKEYS, tm), lambda i, j: (i, 0, 0, j))],
        out_shape=[jax.ShapeDtypeStruct((b, t, d), F32), jax.ShapeDtypeStruct((b * t, d // LANES, LANES), F32),
                   jax.ShapeDtypeStruct((b, nhp, PEER_NKEYS, t), F32)],
        compiler_params=pltpu.CompilerParams(vmem_limit_bytes=VMEM_LIMIT),
        name="out_proj_peer_scores",
    )(oa, od, x, gt1, sh2, sc2, g1, g2, wout, wq, keys)


def _top1(s, iota):
    m = jnp.max(s, axis=0, keepdims=True)
    idx = jnp.min(jnp.where(s == m, iota, s.shape[0]), axis=0, keepdims=True)
    return m, idx, jnp.where(iota == idx, -jnp.inf, s)


def _topk_kernel(st_ref, idx_ref, gate_ref):
    tl = st_ref.shape[-1]
    k = PEER_TOPK
    iota_n = lax.broadcasted_iota(I32, (PEER_NKEYS, tl), 0)
    widths = [k // (a + 1) for a in range(k)]
    n_cand = -(-sum(widths) // 8) * 8
    iota_c = lax.broadcasted_iota(I32, (n_cand, tl), 0)
    pad = n_cand - sum(widths)
    gates, rows = [], []
    for h in range(PEER_HEADS):
        tops = []
        for p in range(2):
            s = st_ref[0, 2 * h + p]
            vals, idxs = [], []
            for _ in range(k):
                m, i, s = _top1(s, iota_n)
                vals.append(m)
                idxs.append(i)
            tops.append((jnp.concatenate(vals, axis=0), jnp.concatenate(idxs, axis=0)))
        (s0, i0), (s1, i1) = tops
        cand = jnp.concatenate([s0[a:a + 1] + s1[0:widths[a]] for a in range(k)]
                               + [jnp.full((pad, tl), -jnp.inf, F32)], axis=0)
        cidx = jnp.concatenate([i0[a:a + 1] * PEER_NKEYS + i1[0:widths[a]] for a in range(k)]
                               + [jnp.zeros((pad, tl), I32)], axis=0)
        best, experts = [], []
        for _ in range(k):
            m, pos, cand = _top1(cand, iota_c)
            best.append(m)
            experts.append(jnp.sum(jnp.where(iota_c == pos, cidx, 0), axis=0, keepdims=True))
        best = jnp.concatenate(best, axis=0)
        e = jnp.exp(best - best[0:1])
        gates.append(e / jnp.sum(e, axis=0, keepdims=True))
        rows += experts
    idx_ref[0] = (jnp.concatenate(rows, axis=0) * 4).T
    nj = PEER_NJ
    dup = (lax.broadcasted_iota(I32, (nj, 2 * nj), 1) // 2 == lax.broadcasted_iota(I32, (nj, 2 * nj), 0))
    gate_ref[0] = jnp.dot(jnp.concatenate(gates, axis=0).T, dup.astype(F32),
                          precision=lax.Precision.HIGHEST, preferred_element_type=F32)


def _topk(st, *, tl):
    b, nhp, nkeys, t = st.shape
    nj = PEER_NJ
    return pl.pallas_call(
        _topk_kernel,
        grid=(b, t // tl),
        in_specs=[pl.BlockSpec((1, nhp, nkeys, tl), lambda i, j: (i, 0, 0, j))],
        out_specs=[pl.BlockSpec((1, tl, nj), lambda i, j: (i, j, 0)),
                   pl.BlockSpec((1, tl, 2 * nj), lambda i, j: (i, j, 0))],
        out_shape=[jax.ShapeDtypeStruct((b, t, nj), I32), jax.ShapeDtypeStruct((b, t, 2 * nj), F32)],
        compiler_params=pltpu.CompilerParams(vmem_limit_bytes=VMEM_LIMIT),
        name="peer_topk",
    )(st)


def _pack_table(tab):
    e, d = tab.shape
    tb = tab.astype(BF16)
    lo = lax.bitcast_convert_type(tb[:, :d // 2], jnp.uint16).astype(jnp.uint32)
    hi = lax.bitcast_convert_type(tb[:, d // 2:], jnp.uint16).astype(jnp.uint32)
    words = lax.bitcast_convert_type((hi << 16) | lo, I32)
    return words.reshape(e * (d // 2 // LANES), LANES)


def _gather_rows(idx_ref, base, table_ref, tile_ref):
    ahead = 4
    rows = [idx_ref[base + j] for j in range(ahead)]
    for j in range(0, PEER_NJ, 2):
        for k in range(2):
            if j + k + ahead < PEER_NJ:
                rows.append(idx_ref[base + j + k + ahead])
        a = table_ref[pl.ds(pl.multiple_of(rows[j], 4), 4), :]
        b = table_ref[pl.ds(pl.multiple_of(rows[j + 1], 4), 4), :]
        tile_ref[4 * j:4 * j + 8, :] = jnp.concatenate([a, b], axis=0)


def _tile_chunk(tile_ref, c):
    return pltpu.bitcast(tile_ref[pl.ds(c, PEER_NJ, stride=4), :], BF16)


def _split_bf16(v):
    head = v.astype(BF16).astype(F32)
    return jnp.concatenate([head, v - head], axis=0)


def _idx_copy(idx_hbm, first_token, buf, sem):
    n = buf.shape[0]
    src = idx_hbm.at[pl.ds(pl.multiple_of(first_token * PEER_NJ, n), n)]
    return pltpu.make_async_copy(src, buf, sem)


def _for_each_group(idx_hbm, ibufs, sems, group):
    step = pl.program_id(0)
    half = PEER_TM // 2
    nu = PEER_UNROLL

    @pl.when(step == 0)
    def _():
        for h in range(2):
            _idx_copy(idx_hbm, h * half, ibufs[h], sems.at[h]).start()

    for h in range(2):
        tok0 = step * PEER_TM + h * half
        _idx_copy(idx_hbm, tok0, ibufs[h], sems.at[h]).wait()
        for g in range(half // nu):
            group(ibufs[h], g * nu * PEER_NJ, h * half + g * nu)

        @pl.when(step + 1 < pl.num_programs(0))
        def _():
            _idx_copy(idx_hbm, tok0 + PEER_TM, ibufs[h], sems.at[h]).start()


def _peer_scratch():
    return ([pltpu.VMEM((4 * TILE_STRIDE + 4, LANES), I32) for _ in range(PEER_UNROLL)]
            + [pltpu.SMEM((PEER_TM // 2 * PEER_NJ,), I32) for _ in range(2)]
            + [pltpu.SemaphoreType.DMA((2,))])


def _peer_u_kernel(idx_hbm, x_ref, gate_ref, u_ref, w_ref, *scratch):
    nj = PEER_NJ
    nu = PEER_UNROLL
    tile_refs, ibufs, sems = scratch[:nu], scratch[nu:nu + 2], scratch[nu + 2]
    row16 = lax.broadcasted_iota(I32, (16, 1), 0) % 8
    even = (lax.broadcasted_iota(I32, (1, 2 * nj), 1) % 2) == 0

    def token(ibuf, off, t, tile_ref):
        _gather_rows(ibuf, off, u_ref, tile_ref)
        x16 = _split_bf16(x_ref[t])
        acc = jnp.zeros((16, 2 * nj), F32)
        for c in range(4):
            lhs = jnp.where((row16 == c) | (row16 == 4 + c), x16, 0.0).astype(BF16)
            acc = acc + _dot_nt(lhs, _tile_chunk(tile_ref, c))
        a = acc[0:8] + acc[8:16]
        return jnp.sum(jnp.where(even, a, pltpu.roll(a, 4, axis=0))[0:4], axis=0, keepdims=True)

    def group(ibuf, off, t0):
        z = jnp.concatenate([token(ibuf, off + i * nj, t0 + i, tile_refs[i]) for i in range(nu)], axis=0)
        act = z + jnp.where(even, pltpu.roll(z, 2 * nj - 1, axis=1), pltpu.roll(z, 1, axis=1))
        gelu = 0.5 * act * (1.0 + lax.erf(act * (2.0 ** -0.5)))
        w_ref[t0:t0 + nu, :] = gate_ref[t0:t0 + nu, :] * gelu

    _for_each_group(idx_hbm, ibufs, sems, group)


def _peer_u(idx4, x3, gate2, table):
    n = x3.shape[0]
    nj = PEER_NJ
    tm = PEER_TM
    return pl.pallas_call(
        _peer_u_kernel,
        grid=(n // tm,),
        in_specs=[pl.BlockSpec(memory_space=pl.ANY),
                  pl.BlockSpec((tm, 8, LANES), lambda i: (i, 0, 0)),
                  pl.BlockSpec((tm, 2 * nj), lambda i: (i, 0)),
                  pl.BlockSpec(memory_space=pltpu.VMEM)],
        out_specs=pl.BlockSpec((tm, 2 * nj), lambda i: (i, 0)),
        out_shape=jax.ShapeDtypeStruct((n, 2 * nj), F32),
        scratch_shapes=_peer_scratch(),
        compiler_params=pltpu.CompilerParams(vmem_limit_bytes=VMEM_LIMIT, dimension_semantics=("arbitrary",)),
        name="peer_u",
    )(idx4, x3, gate2, table)


def _peer_v_kernel(idx_hbm, w_ref, v_ref, o_ref, *scratch):
    nj = PEER_NJ
    nu = PEER_UNROLL
    tile_refs, ibufs, sems = scratch[:nu], scratch[nu:nu + 2], scratch[nu + 2]
    row16 = lax.broadcasted_iota(I32, (16, 1), 0) % 8
    even = (lax.broadcasted_iota(I32, (1, 2 * nj), 1) % 2) == 0

    def token(ibuf, off, t, tile_ref):
        _gather_rows(ibuf, off, v_ref, tile_ref)
        w16 = _split_bf16(jnp.broadcast_to(w_ref[t:t + 1, :], (8, 2 * nj)))
        acc = jnp.zeros((16, LANES), F32)
        for c in range(4):
            keep = ((row16 == c) & even) | ((row16 == 4 + c) & jnp.logical_not(even))
            lhs = jnp.where(keep, w16, 0.0).astype(BF16)
            acc = acc + jnp.dot(lhs, _tile_chunk(tile_ref, c), preferred_element_type=F32)
        o_ref[t] = acc[0:8] + acc[8:16]

    def group(ibuf, off, t0):
        for i in range(nu):
            token(ibuf, off + i * nj, t0 + i, tile_refs[i])

    _for_each_group(idx_hbm, ibufs, sems, group)


def _peer_v(idx4, w2, table):
    n = w2.shape[0]
    nj = PEER_NJ
    tm = PEER_TM
    return pl.pallas_call(
        _peer_v_kernel,
        grid=(n // tm,),
        in_specs=[pl.BlockSpec(memory_space=pl.ANY),
                  pl.BlockSpec((tm, 2 * nj), lambda i: (i, 0)),
                  pl.BlockSpec(memory_space=pltpu.VMEM)],
        out_specs=pl.BlockSpec((tm, 8, LANES), lambda i: (i, 0, 0)),
        out_shape=jax.ShapeDtypeStruct((n, 8, LANES), F32),
        scratch_shapes=_peer_scratch(),
        compiler_params=pltpu.CompilerParams(vmem_limit_bytes=VMEM_LIMIT, dimension_semantics=("arbitrary",)),
        name="peer_v",
    )(idx4, w2, table)


def _final_kernel(x1_ref, y_ref, gt2_ref, g_ref, o_ref):
    y = y_ref[...].reshape(x1_ref.shape[1:])
    o_ref[0] = x1_ref[0] + gt2_ref[0] * _rms(y, g_ref[...])


def _final(x1, y3, gt2, g, *, tm):
    b, t, d = x1.shape
    nt = t // tm
    tok = pl.BlockSpec((1, tm, d), lambda i, j: (i, j, 0))
    return pl.pallas_call(
        _final_kernel,
        grid=(b, nt),
        in_specs=[tok, pl.BlockSpec((tm, d // LANES, LANES), lambda i, j: (i * nt + j, 0, 0)),
                  pl.BlockSpec((1, 1, d), lambda i, j: (i, 0, 0)),
                  pl.BlockSpec((1, d), lambda i, j: (0, 0))],
        out_specs=tok,
        out_shape=jax.ShapeDtypeStruct((b, t, d), F32),
        compiler_params=pltpu.CompilerParams(vmem_limit_bytes=VMEM_LIMIT),
        name="final_residual",
    )(x1, y3, gt2, g)


def _rope_tables(t):
    tok = jnp.arange(t)
    row = (tok // GRID_W).astype(F32)
    col = (tok % GRID_W).astype(F32)
    n_freq = DIFF_QK // 4
    inv = ROPE_THETA ** (-jnp.arange(n_freq, dtype=F32) / n_freq)
    ang = jnp.concatenate([row[:, None] * inv, col[:, None] * inv], axis=-1)
    ang2 = jnp.tile(jnp.repeat(ang, 2, axis=1), (1, LANES // DIFF_QK))
    sign = jnp.where(jnp.arange(LANES) % 2 == 0, -1.0, 1.0).astype(F32)
    return jnp.cos(ang2), jnp.sin(ang2) * sign


def _head_major_cols(w):
    d = w.shape[0]
    return w.reshape(d, 2, DIFF_HEADS, DIFF_QK).transpose(0, 2, 1, 3).reshape(d, -1)


def kernel(x, c, ctx, c_ctx, w_ada, b_ada, g_norm, w_in, na_rel_bias, diff_lambda, diff_subln,
           w_out, peer_wq, peer_keys, peer_u, peer_v):
    assert w_ada.shape[0] == 1, "single-layer kernel"
    b, t, d = x.shape
    n_ctx = ctx.shape[1]
    rows = t // GRID_W
    assert t % (NA_QROWS * GRID_W) == 0 and rows >= NA_KROWS

    pad = (-(b + 1)) % 8
    cc = jnp.concatenate([c, c_ctx[None], jnp.zeros((pad, d), F32)], axis=0)
    mod = _ada(cc, w_ada[0], b_ada[0][None]).reshape(b + 1 + pad, N_MOD, d)
    lat = lambda i: mod[:b, i][:, None, :]
    cxm = lambda i: mod[b:b + 1, i][:, None, :]

    gw = NA_WIDTH
    w_in0 = w_in[0]
    w_perm = jnp.concatenate([w_in0[:, :3 * gw], _head_major_cols(w_in0[:, 3 * gw:4 * gw]),
                              _head_major_cols(w_in0[:, 4 * gw:5 * gw]), w_in0[:, 5 * gw:]], axis=1).astype(BF16)
    cos2, sin2 = _rope_tables(t)
    g0 = g_norm[0, 0][None]
    qa, ka, va, qd, kd, vd = _qkv(x, lat(0), lat(1), g0, w_perm, cos2, sin2,
                                  rope=True, per_batch_mod=True, tm=512)
    _, ka_c, va_c, _, kd_c, vd_c = _qkv(ctx, cxm(0), cxm(1), g0, w_perm, cos2[:n_ctx], sin2[:n_ctx],
                                        rope=False, per_batch_mod=False, tm=n_ctx)

    out_a = _na(qa, ka, va, ka_c, va_c, *_na_bias_table(na_rel_bias[0], rows))
    out_d = _diff(diff_lambda[0], qd, kd, vd, kd_c, vd_c, diff_subln[0][None], tq=DIFF_TQ, tk=DIFF_TK)

    keys = peer_keys[0].reshape(2 * PEER_HEADS, PEER_NKEYS, -1).astype(BF16)
    x1, h2, st = _out(out_a, out_d, x, lat(2), lat(3), lat(4), g_norm[0, 1][None], g_norm[0, 2][None],
                      w_out[0].astype(BF16), peer_wq[0].astype(BF16), keys, tm=256)

    idx4, gate2 = _topk(st, tl=256)
    idx4 = idx4.reshape(-1)
    gate2 = gate2.reshape(b * t, 2 * PEER_NJ)

    w2 = _peer_u(idx4, h2, gate2, _pack_table(peer_u[0]))
    y3 = _peer_v(idx4, w2, _pack_table(peer_v[0]))
    return _final(x1, y3, lat(5), g_norm[0, 3][None], tm=512)
```

```python
import functools
import math

import numpy as np
import jax
import jax.numpy as jnp
from jax import lax
from jax.experimental import pallas as pl
from jax.experimental.pallas import tpu as pltpu

F32 = jnp.float32
BF16 = jnp.bfloat16
I32 = jnp.int32

GRID_W = 64
HEAD_DIM = 64
NA_HEADS = 8
NA_WIDTH = NA_HEADS * HEAD_DIM
NA_WIN_R = 8
NA_WIN_C = 16
DIFF_HEADS = 4
DIFF_QK = HEAD_DIM
DIFF_V = 2 * DIFF_QK
ROPE_THETA = 10000.0
PEER_HEADS = 8
PEER_NKEYS = 128
PEER_TOPK = 16
N_MOD = 6
EPS = 1e-6
LAMBDA_INIT = 0.8 - 0.6 * math.exp(-0.3 * 0)
LOG2E = math.log2(math.e)
DIFF_TQ = 512
DIFF_TK = 512

LANES = 128
VMEM_LIMIT = 56 * 1024 * 1024
NEG = -1e30

NA_QROWS = 8
NA_QSUBROWS = 2
NA_QSUB = NA_QSUBROWS * GRID_W
NA_KROWS = 10
PEER_TM = 64
PEER_NJ = PEER_HEADS * PEER_TOPK
TILE_STRIDE = PEER_NJ + 1
PEER_UNROLL = 8


def _dot_nt(a, b):
    return lax.dot_general(a, b, (((1,), (1,)), ((), ())), preferred_element_type=F32)


def _rms(x, g):
    return x * lax.rsqrt(jnp.mean(x * x, axis=-1, keepdims=True) + EPS) * g


def _ada_kernel(c_ref, w_ref, b_ref, o_ref):
    c = c_ref[...]
    s = c / (1.0 + jnp.exp(-c))
    o_ref[...] = jnp.dot(s, w_ref[...], precision=lax.Precision.HIGHEST,
                         preferred_element_type=F32) + b_ref[...]


def _ada(cc, w, b):
    m, d = cc.shape
    n = w.shape[1]
    tn = 1024
    return pl.pallas_call(
        _ada_kernel,
        grid=(n // tn,),
        in_specs=[pl.BlockSpec((m, d), lambda j: (0, 0)),
                  pl.BlockSpec((d, tn), lambda j: (0, j)),
                  pl.BlockSpec((1, tn), lambda j: (0, j))],
        out_specs=pl.BlockSpec((m, tn), lambda j: (0, j)),
        out_shape=jax.ShapeDtypeStruct((m, n), F32),
        compiler_params=pltpu.CompilerParams(vmem_limit_bytes=VMEM_LIMIT),
        name="ada",
    )(cc, w, b)


def _qkv_kernel(x_ref, sh_ref, sc_ref, g_ref, w_ref, cos_ref, sin_ref,
                qa_ref, ka_ref, va_ref, qd_ref, kd_ref, vd_ref, *, rope):
    x = x_ref[0]
    h = _rms(x, g_ref[...]) * (1.0 + sc_ref[0]) + sh_ref[0]
    hb = h.astype(BF16)
    gw = NA_WIDTH
    scale = HEAD_DIM ** -0.5

    def proj(g):
        return jnp.dot(hb, w_ref[:, g * gw:(g + 1) * gw], preferred_element_type=F32)

    def roped(p):
        if not rope:
            return p
        cos2 = cos_ref[...]
        sin2 = sin_ref[...]
        even = (lax.broadcasted_iota(I32, cos2.shape, 1) % 2) == 0
        outs = []
        for c in range(gw // LANES):
            v = p[:, c * LANES:(c + 1) * LANES]
            nxt = pltpu.roll(v, LANES - 1, axis=1)
            prv = pltpu.roll(v, 1, axis=1)
            outs.append(v * cos2 + jnp.where(even, nxt, prv) * sin2)
        return jnp.concatenate(outs, axis=1)

    qa_ref[0] = (proj(0) * scale).astype(BF16)
    ka_ref[0] = proj(1).astype(BF16)
    va_ref[0] = proj(2).astype(BF16)
    qd_ref[0] = (roped(proj(3)) * (scale * LOG2E)).astype(BF16)
    kd_ref[0] = roped(proj(4)).astype(BF16)
    vd = proj(5).astype(BF16)
    ones = jnp.ones((vd.shape[0], DIFF_V), BF16)
    pieces = []
    for hd in range(DIFF_HEADS):
        pieces += [vd[:, hd * DIFF_V:(hd + 1) * DIFF_V], ones]
    vd_ref[0] = jnp.concatenate(pieces, axis=1)


def _qkv(x, sh, sc, g, w, cos2, sin2, *, rope, per_batch_mod, tm):
    b, l, d = x.shape
    n = w.shape[1]
    gw = NA_WIDTH
    mod_map = (lambda i, j: (i, 0, 0)) if per_batch_mod else (lambda i, j: (0, 0, 0))
    out_sds = jax.ShapeDtypeStruct((b, l, gw), BF16)
    out_spec = pl.BlockSpec((1, tm, gw), lambda i, j: (i, j, 0))
    return pl.pallas_call(
        functools.partial(_qkv_kernel, rope=rope),
        grid=(b, l // tm),
        in_specs=[pl.BlockSpec((1, tm, d), lambda i, j: (i, j, 0)),
                  pl.BlockSpec((1, 1, d), mod_map),
                  pl.BlockSpec((1, 1, d), mod_map),
                  pl.BlockSpec((1, d), lambda i, j: (0, 0)),
                  pl.BlockSpec((d, n), lambda i, j: (0, 0)),
                  pl.BlockSpec((tm, LANES), lambda i, j: (j, 0)),
                  pl.BlockSpec((tm, LANES), lambda i, j: (j, 0))],
        out_specs=[out_spec] * 5 + [pl.BlockSpec((1, tm, 2 * gw), lambda i, j: (i, j, 0))],
        out_shape=[out_sds] * 5 + [jax.ShapeDtypeStruct((b, l, 2 * gw), BF16)],
        compiler_params=pltpu.CompilerParams(vmem_limit_bytes=VMEM_LIMIT),
        name="qkv_rope" if rope else "qkv_ctx",
    )(x, sh, sc, g, w, cos2, sin2)


def _na_window(qr0, rows):
    r_start = jnp.clip(qr0 - NA_WIN_R // 2, 0, rows - NA_WIN_R)
    return jnp.minimum(r_start, rows - NA_KROWS)


def _na_kernel(pid_ref, q_ref, k_ref, v_ref, kc_ref, vc_ref, bias_ref, o_ref, *, rows):
    rb = pl.program_id(2)
    nk = NA_KROWS * GRID_W
    kc = kc_ref[0]
    vc = vc_ref[0]
    lane = lax.broadcasted_iota(I32, (1, LANES), 1)
    nsub = NA_QROWS // NA_QSUBROWS
    for s in range(nsub):
        sl = slice(s * NA_QSUB, (s + 1) * NA_QSUB)
        start = pl.multiple_of(_na_window(rb * NA_QROWS + s * NA_QSUBROWS, rows) * GRID_W, GRID_W)
        kw = k_ref[0, pl.ds(start, nk), :]
        vw = v_ref[0, pl.ds(start, nk), :]
        pid = pid_ref[rb * nsub + s]
        q = q_ref[0, sl, :]
        zero = jnp.zeros_like(q)
        q2 = jnp.concatenate([jnp.where(lane < HEAD_DIM, q, zero), jnp.where(lane >= HEAD_DIM, q, zero)], axis=0)
        s_loc = _dot_nt(q2, kw) + jnp.concatenate([bias_ref[pid, 0], bias_ref[pid, 1]], axis=0)
        s_ctx = _dot_nt(q2, kc)
        mx = jnp.maximum(jnp.max(s_loc, axis=-1, keepdims=True), jnp.max(s_ctx, axis=-1, keepdims=True))
        p_loc = jnp.exp(s_loc - mx)
        p_ctx = jnp.exp(s_ctx - mx)
        l = jnp.sum(p_loc, axis=-1, keepdims=True) + jnp.sum(p_ctx, axis=-1, keepdims=True)
        o = (jnp.dot(p_loc.astype(BF16), vw, preferred_element_type=F32)
             + jnp.dot(p_ctx.astype(BF16), vc, preferred_element_type=F32)) / l
        o_ref[0, sl, :] = jnp.where(lane < HEAD_DIM, o[:NA_QSUB], o[NA_QSUB:]).astype(BF16)


def _na_bias_table(rel_bias, rows):
    n_ro = 2 * NA_WIN_R - 1
    qc = np.arange(GRID_W)[:, None]
    kc = np.arange(GRID_W)[None, :]
    c_start = np.clip(qc - NA_WIN_C // 2, 0, GRID_W - NA_WIN_C)
    valid_c = (kc >= c_start) & (kc < c_start + NA_WIN_C)
    co = np.clip(kc - qc + NA_WIN_C - 1, 0, 2 * NA_WIN_C - 2)
    tz = jnp.take(rel_bias, jnp.asarray(co.reshape(-1), I32), axis=2).reshape(NA_HEADS, n_ro, GRID_W, GRID_W)
    tz = jnp.where(jnp.asarray(valid_c)[None, None], tz, NEG)
    tz = jnp.concatenate([tz, jnp.full((NA_HEADS, 1, GRID_W, GRID_W), NEG, F32)], axis=1)
    patterns, pid = [], []
    for qr0 in range(0, rows, NA_QSUBROWS):
        ws = min(int(np.clip(qr0 - NA_WIN_R // 2, 0, rows - NA_WIN_R)), rows - NA_KROWS)
        qr = qr0 + np.arange(NA_QSUBROWS)[:, None]
        kr = ws + np.arange(NA_KROWS)[None, :]
        r_start = np.clip(qr - NA_WIN_R // 2, 0, rows - NA_WIN_R)
        valid_r = (kr >= r_start) & (kr < r_start + NA_WIN_R)
        assert valid_r.sum(axis=1).min() == NA_WIN_R
        sel = tuple(np.where(valid_r, kr - qr + NA_WIN_R - 1, n_ro).reshape(-1))
        if sel not in patterns:
            patterns.append(sel)
        pid.append(patterns.index(sel))
    sel = np.asarray(patterns, np.int32).reshape(-1)
    blocks = jnp.take(tz, jnp.asarray(sel), axis=1)
    blocks = blocks.reshape(NA_HEADS, len(patterns), NA_QSUBROWS, NA_KROWS, GRID_W, GRID_W)
    table = blocks.transpose(1, 0, 2, 4, 3, 5).reshape(len(patterns), NA_HEADS, NA_QSUB, NA_KROWS * GRID_W)
    return table, np.asarray(pid, np.int32)


def _na(qa, ka, va, ka_c, va_c, bias, pid):
    b, t, _ = qa.shape
    c = ka_c.shape[1]
    rows = t // GRID_W
    nrb = rows // NA_QROWS
    tq = NA_QROWS * GRID_W
    nk = NA_KROWS * GRID_W
    npat = bias.shape[0]
    return pl.pallas_call(
        functools.partial(_na_kernel, rows=rows),
        grid=(b, NA_HEADS // 2, nrb),
        in_specs=[pl.BlockSpec(memory_space=pltpu.SMEM),
                  pl.BlockSpec((1, tq, LANES), lambda i, hp, rb: (i, rb, hp)),
                  pl.BlockSpec((1, t, LANES), lambda i, hp, rb: (i, 0, hp)),
                  pl.BlockSpec((1, t, LANES), lambda i, hp, rb: (i, 0, hp)),
                  pl.BlockSpec((1, c, LANES), lambda i, hp, rb: (i, 0, hp)),
                  pl.BlockSpec((1, c, LANES), lambda i, hp, rb: (i, 0, hp)),
                  pl.BlockSpec((npat, 2, NA_QSUB, nk), lambda i, hp, rb: (0, hp, 0, 0))],
        out_specs=pl.BlockSpec((1, tq, LANES), lambda i, hp, rb: (i, rb, hp)),
        out_shape=jax.ShapeDtypeStruct((b, t, NA_WIDTH), BF16),
        compiler_params=pltpu.CompilerParams(vmem_limit_bytes=VMEM_LIMIT),
        name="na_attn",
    )(jnp.asarray(pid), qa, ka, va, ka_c, va_c, bias)


def _diff_kernel(lam_ref, q_ref, k_ref, v_ref, kc_ref, vc_ref, g_ref, o_ref, *, tk):
    q = q_ref[0]
    tq = q.shape[0]
    t = k_ref.shape[1]
    lane = lax.broadcasted_iota(I32, (1, LANES), 1)
    zero = jnp.zeros_like(q)
    q2 = jnp.concatenate([jnp.where(lane < DIFF_QK, q, zero), jnp.where(lane >= DIFF_QK, q, zero)], axis=0)

    def update(kb, vb, carry):
        m, a = carry
        s = _dot_nt(q2, kb)
        mn = jnp.maximum(m, jnp.max(s, axis=-1, keepdims=True))
        p = jnp.exp2(s - mn)
        return mn, jnp.exp2(m - mn) * a + jnp.dot(p.astype(BF16), vb, preferred_element_type=F32)

    init = (jnp.full((2 * tq, 1), NEG, F32), jnp.zeros((2 * tq, 2 * DIFF_V), F32))
    carry = update(kc_ref[0], vc_ref[0], init)
    for i in range(t // tk):
        carry = update(k_ref[0, i * tk:(i + 1) * tk, :], v_ref[0, i * tk:(i + 1) * tk, :], carry)
    lp = lam_ref[...]
    lam = (jnp.exp(jnp.sum(lp[0:1] * lp[1:2], axis=-1, keepdims=True))
           - jnp.exp(jnp.sum(lp[2:3] * lp[3:4], axis=-1, keepdims=True)) + LAMBDA_INIT)
    a = carry[1]
    a = a[:, :DIFF_V] / a[:, DIFF_V:DIFF_V + 1]
    out = a[:tq] - lam * a[tq:]
    o_ref[0] = (_rms(out, g_ref[...]) * (1.0 - LAMBDA_INIT)).astype(BF16)


def _diff(lam_params, qd, kd, vd, kd_c, vd_c, subln, *, tq, tk):
    b, t, w = qd.shape
    c = kd_c.shape[1]
    return pl.pallas_call(
        functools.partial(_diff_kernel, tk=tk),
        grid=(b, DIFF_HEADS, t // tq),
        in_specs=[pl.BlockSpec((4, DIFF_QK), lambda i, h, j: (0, 0)),
                  pl.BlockSpec((1, tq, LANES), lambda i, h, j: (i, j, h)),
                  pl.BlockSpec((1, t, LANES), lambda i, h, j: (i, 0, h)),
                  pl.BlockSpec((1, t, 2 * DIFF_V), lambda i, h, j: (i, 0, h)),
                  pl.BlockSpec((1, c, LANES), lambda i, h, j: (i, 0, h)),
                  pl.BlockSpec((1, c, 2 * DIFF_V), lambda i, h, j: (i, 0, h)),
                  pl.BlockSpec((1, DIFF_V), lambda i, h, j: (0, 0))],
        out_specs=pl.BlockSpec((1, tq, LANES), lambda i, h, j: (i, j, h)),
        out_shape=jax.ShapeDtypeStruct((b, t, w), BF16),
        compiler_params=pltpu.CompilerParams(vmem_limit_bytes=VMEM_LIMIT),
        name="diff_attn",
    )(lam_params, qd, kd, vd, kd_c, vd_c, subln)


def _out_kernel(oa_ref, od_ref, x_ref, gt1_ref, sh2_ref, sc2_ref, g1_ref, g2_ref,
                wout_ref, wq_ref, keys_ref, x1_ref, h2_ref, st_ref):
    wa = NA_WIDTH
    y = (jnp.dot(oa_ref[0], wout_ref[0:wa, :], preferred_element_type=F32)
         + jnp.dot(od_ref[0], wout_ref[wa:, :], preferred_element_type=F32))
    x1 = x_ref[0] + gt1_ref[0] * _rms(y, g1_ref[...])
    x1_ref[0] = x1
    h2 = _rms(x1, g2_ref[...]) * (1.0 + sc2_ref[0]) + sh2_ref[0]
    h2_ref[...] = h2.reshape(h2_ref.shape)
    qb = jnp.dot(h2.astype(BF16), wq_ref[...], preferred_element_type=F32).astype(BF16)
    for hp in range(2 * PEER_HEADS):
        st_ref[0, hp] = _dot_nt(keys_ref[hp], qb[:, hp * LANES:(hp + 1) * LANES])


def _out(oa, od, x, gt1, sh2, sc2, g1, g2, wout, wq, keys, *, tm):
    b, t, d = x.shape
    nq = wq.shape[1]
    nhp = keys.shape[0]
    mod = pl.BlockSpec((1, 1, d), lambda i, j: (i, 0, 0))
    gsp = pl.BlockSpec((1, d), lambda i, j: (0, 0))
    tok = pl.BlockSpec((1, tm, d), lambda i, j: (i, j, 0))
    half = pl.BlockSpec((1, tm, NA_WIDTH), lambda i, j: (i, j, 0))
    return pl.pallas_call(
        _out_kernel,
        grid=(b, t // tm),
        in_specs=[half, half, tok, mod, mod, mod, gsp, gsp,
                  pl.BlockSpec((d, d), lambda i, j: (0, 0)),
                  pl.BlockSpec((d, nq), lambda i, j: (0, 0)),
                  pl.BlockSpec((nhp, PEER_NKEYS, LANES), lambda i, j: (0, 0, 0))],
        out_specs=[tok, pl.BlockSpec((tm, d // LANES, LANES), lambda i, j: (i * (t // tm) + j, 0, 0)),
                   pl.BlockSpec((1, nhp, PEER_NKEYS, tm), lambda i, j: (i, 0, 0, j))],
        out_shape=[jax.ShapeDtypeStruct((b, t, d), F32), jax.ShapeDtypeStruct((b * t, d // LANES, LANES), F32),
                   jax.ShapeDtypeStruct((b, nhp, PEER_NKEYS, t), F32)],
        compiler_params=pltpu.CompilerParams(vmem_limit_bytes=VMEM_LIMIT),
        name="out_proj_peer_scores",
    )(oa, od, x, gt1, sh2, sc2, g1, g2, wout, wq, keys)


def _top1(s, iota):
    m = jnp.max(s, axis=0, keepdims=True)
    idx = jnp.min(jnp.where(s == m, iota, s.shape[0]), axis=0, keepdims=True)
    return m, idx, jnp.where(iota == idx, -jnp.inf, s)


def _topk_kernel(st_ref, idx_ref, gate_ref):
    tl = st_ref.shape[-1]
    k = PEER_TOPK
    iota_n = lax.broadcasted_iota(I32, (PEER_NKEYS, tl), 0)
    widths = [k // (a + 1) for a in range(k)]
    n_cand = -(-sum(widths) // 8) * 8
    iota_c = lax.broadcasted_iota(I32, (n_cand, tl), 0)
    pad = n_cand - sum(widths)
    gates, rows = [], []
    for h in range(PEER_HEADS):
        tops = []
        for p in range(2):
            s = st_ref[0, 2 * h + p]
            vals, idxs = [], []
            for _ in range(k):
                m, i, s = _top1(s, iota_n)
                vals.append(m)
                idxs.append(i)
            tops.append((jnp.concatenate(vals, axis=0), jnp.concatenate(idxs, axis=0)))
        (s0, i0), (s1, i1) = tops
        cand = jnp.concatenate([s0[a:a + 1] + s1[0:widths[a]] for a in range(k)]
                               + [jnp.full((pad, tl), -jnp.inf, F32)], axis=0)
        cidx = jnp.concatenate([i0[a:a + 1] * PEER_NKEYS + i1[0:widths[a]] for a in range(k)]
                               + [jnp.zeros((pad, tl), I32)], axis=0)
        best, experts = [], []
        for _ in range(k):
            m, pos, cand = _top1(cand, iota_c)
            best.append(m)
            experts.append(jnp.sum(jnp.where(iota_c == pos, cidx, 0), axis=0, keepdims=True))
        best = jnp.concatenate(best, axis=0)
        e = jnp.exp(best - best[0:1])
        gates.append(e / jnp.sum(e, axis=0, keepdims=True))
        rows += experts
    idx_ref[0] = (jnp.concatenate(rows, axis=0) * 4).T
    nj = PEER_NJ
    dup = (lax.broadcasted_iota(I32, (nj, 2 * nj), 1) // 2 == lax.broadcasted_iota(I32, (nj, 2 * nj), 0))
    gate_ref[0] = jnp.dot(jnp.concatenate(gates, axis=0).T, dup.astype(F32),
                          precision=lax.Precision.HIGHEST, preferred_element_type=F32)


def _topk(st, *, tl):
    b, nhp, nkeys, t = st.shape
    nj = PEER_NJ
    return pl.pallas_call(
        _topk_kernel,
        grid=(b, t // tl),
        in_specs=[pl.BlockSpec((1, nhp, nkeys, tl), lambda i, j: (i, 0, 0, j))],
        out_specs=[pl.BlockSpec((1, tl, nj), lambda i, j: (i, j, 0)),
                   pl.BlockSpec((1, tl, 2 * nj), lambda i, j: (i, j, 0))],
        out_shape=[jax.ShapeDtypeStruct((b, t, nj), I32), jax.ShapeDtypeStruct((b, t, 2 * nj), F32)],
        compiler_params=pltpu.CompilerParams(vmem_limit_bytes=VMEM_LIMIT),
        name="peer_topk",
    )(st)


def _pack_table(tab):
    e, d = tab.shape
    tb = tab.astype(BF16)
    lo = lax.bitcast_convert_type(tb[:, :d // 2], jnp.uint16).astype(jnp.uint32)
    hi = lax.bitcast_convert_type(tb[:, d // 2:], jnp.uint16).astype(jnp.uint32)
    words = lax.bitcast_convert_type((hi << 16) | lo, I32)
    return words.reshape(e * (d // 2 // LANES), LANES)


def _gather_rows(idx_ref, trow, table_ref, tile_ref):
    ahead = 4
    rows = [idx_ref[trow, j] for j in range(ahead)]
    for j in range(0, PEER_NJ, 2):
        for k in range(2):
            if j + k + ahead < PEER_NJ:
                rows.append(idx_ref[trow, j + k + ahead])
        a = table_ref[pl.ds(pl.multiple_of(rows[j], 4), 4), :]
        b = table_ref[pl.ds(pl.multiple_of(rows[j + 1], 4), 4), :]
        tile_ref[4 * j:4 * j + 8, :] = jnp.concatenate([a, b], axis=0)


def _tile_chunk(tile_ref, c):
    return pltpu.bitcast(tile_ref[pl.ds(c, PEER_NJ, stride=4), :], BF16)


def _split_bf16(v):
    head = v.astype(BF16).astype(F32)
    return jnp.concatenate([head, v - head], axis=0)


def _idx_copy(idx_hbm, first_token, buf, sem):
    n = buf.shape[0]
    src = idx_hbm.at[pl.ds(pl.multiple_of(first_token, n), n), :]
    return pltpu.make_async_copy(src, buf, sem)


def _for_each_group(idx_hbm, ibufs, sems, group):
    step = pl.program_id(0)
    half = PEER_TM // 2
    nu = PEER_UNROLL

    @pl.when(step == 0)
    def _():
        for h in range(2):
            _idx_copy(idx_hbm, h * half, ibufs[h], sems.at[h]).start()

    for h in range(2):
        tok0 = step * PEER_TM + h * half
        _idx_copy(idx_hbm, tok0, ibufs[h], sems.at[h]).wait()
        for g in range(half // nu):
            group(ibufs[h], g * nu, h * half + g * nu)

        @pl.when(step + 1 < pl.num_programs(0))
        def _():
            _idx_copy(idx_hbm, tok0 + PEER_TM, ibufs[h], sems.at[h]).start()


def _peer_scratch():
    return ([pltpu.VMEM((4 * TILE_STRIDE + 4, LANES), I32) for _ in range(PEER_UNROLL)]
            + [pltpu.SMEM((PEER_TM // 2, PEER_NJ), I32) for _ in range(2)]
            + [pltpu.SemaphoreType.DMA((2,))])


def _peer_u_kernel(idx_hbm, x_ref, gate_ref, u_ref, w_ref, *scratch):
    nj = PEER_NJ
    nu = PEER_UNROLL
    tile_refs, ibufs, sems = scratch[:nu], scratch[nu:nu + 2], scratch[nu + 2]
    row16 = lax.broadcasted_iota(I32, (16, 1), 0) % 8
    even = (lax.broadcasted_iota(I32, (1, 2 * nj), 1) % 2) == 0

    def token(ibuf, off, t, tile_ref):
        _gather_rows(ibuf, off, u_ref, tile_ref)
        x16 = _split_bf16(x_ref[t])
        acc = jnp.zeros((16, 2 * nj), F32)
        for c in range(4):
            lhs = jnp.where((row16 == c) | (row16 == 4 + c), x16, 0.0).astype(BF16)
            acc = acc + _dot_nt(lhs, _tile_chunk(tile_ref, c))
        a = acc[0:8] + acc[8:16]
        return jnp.sum(jnp.where(even, a, pltpu.roll(a, 4, axis=0))[0:4], axis=0, keepdims=True)

    def group(ibuf, off, t0):
        z = jnp.concatenate([token(ibuf, off + i, t0 + i, tile_refs[i]) for i in range(nu)], axis=0)
        act = z + jnp.where(even, pltpu.roll(z, 2 * nj - 1, axis=1), pltpu.roll(z, 1, axis=1))
        gelu = 0.5 * act * (1.0 + lax.erf(act * (2.0 ** -0.5)))
        w_ref[t0:t0 + nu, :] = gate_ref[t0:t0 + nu, :] * gelu

    _for_each_group(idx_hbm, ibufs, sems, group)


def _peer_u(idx4, x3, gate2, table):
    n = x3.shape[0]
    nj = PEER_NJ
    tm = PEER_TM
    return pl.pallas_call(
        _peer_u_kernel,
        grid=(n // tm,),
        in_specs=[pl.BlockSpec(memory_space=pl.ANY),
                  pl.BlockSpec((tm, 8, LANES), lambda i: (i, 0, 0)),
                  pl.BlockSpec((tm, 2 * nj), lambda i: (i, 0)),
                  pl.BlockSpec(memory_space=pltpu.VMEM)],
        out_specs=pl.BlockSpec((tm, 2 * nj), lambda i: (i, 0)),
        out_shape=jax.ShapeDtypeStruct((n, 2 * nj), F32),
        scratch_shapes=_peer_scratch(),
        compiler_params=pltpu.CompilerParams(vmem_limit_bytes=VMEM_LIMIT, dimension_semantics=("arbitrary",)),
        name="peer_u",
    )(idx4, x3, gate2, table)


def _peer_v_kernel(idx_hbm, w_ref, v_ref, o_ref, *scratch):
    nj = PEER_NJ
    nu = PEER_UNROLL
    tile_refs, ibufs, sems = scratch[:nu], scratch[nu:nu + 2], scratch[nu + 2]
    row16 = lax.broadcasted_iota(I32, (16, 1), 0) % 8
    even = (lax.broadcasted_iota(I32, (1, 2 * nj), 1) % 2) == 0

    def token(ibuf, off, t, tile_ref):
        _gather_rows(ibuf, off, v_ref, tile_ref)
        w16 = _split_bf16(jnp.broadcast_to(w_ref[t:t + 1, :], (8, 2 * nj)))
        acc = jnp.zeros((16, LANES), F32)
        for c in range(4):
            keep = ((row16 == c) & even) | ((row16 == 4 + c) & jnp.logical_not(even))
            lhs = jnp.where(keep, w16, 0.0).astype(BF16)
            acc = acc + jnp.dot(lhs, _tile_chunk(tile_ref, c), preferred_element_type=F32)
        o_ref[t] = acc[0:8] + acc[8:16]

    def group(ibuf, off, t0):
        for i in range(nu):
            token(ibuf, off + i, t0 + i, tile_refs[i])

    _for_each_group(idx_hbm, ibufs, sems, group)


def _peer_v(idx4, w2, table):
    n = w2.shape[0]
    nj = PEER_NJ
    tm = PEER_TM
    return pl.pallas_call(
        _peer_v_kernel,
        grid=(n // tm,),
        in_specs=[pl.BlockSpec(memory_space=pl.ANY),
                  pl.BlockSpec((tm, 2 * nj), lambda i: (i, 0)),
                  pl.BlockSpec(memory_space=pltpu.VMEM)],
        out_specs=pl.BlockSpec((tm, 8, LANES), lambda i: (i, 0, 0)),
        out_shape=jax.ShapeDtypeStruct((n, 8, LANES), F32),
        scratch_shapes=_peer_scratch(),
        compiler_params=pltpu.CompilerParams(vmem_limit_bytes=VMEM_LIMIT, dimension_semantics=("arbitrary",)),
        name="peer_v",
    )(idx4, w2, table)


def _final_kernel(x1_ref, y_ref, gt2_ref, g_ref, o_ref):
    y = y_ref[...].reshape(x1_ref.shape[1:])
    o_ref[0] = x1_ref[0] + gt2_ref[0] * _rms(y, g_ref[...])


def _final(x1, y3, gt2, g, *, tm):
    b, t, d = x1.shape
    nt = t // tm
    tok = pl.BlockSpec((1, tm, d), lambda i, j: (i, j, 0))
    return pl.pallas_call(
        _final_kernel,
        grid=(b, nt),
        in_specs=[tok, pl.BlockSpec((tm, d // LANES, LANES), lambda i, j: (i * nt + j, 0, 0)),
                  pl.BlockSpec((1, 1, d), lambda i, j: (i, 0, 0)),
                  pl.BlockSpec((1, d), lambda i, j: (0, 0))],
        out_specs=tok,
        out_shape=jax.ShapeDtypeStruct((b, t, d), F32),
        compiler_params=pltpu.CompilerParams(vmem_limit_bytes=VMEM_LIMIT),
        name="final_residual",
    )(x1, y3, gt2, g)


def _rope_tables(t):
    tok = jnp.arange(t)
    row = (tok // GRID_W).astype(F32)
    col = (tok % GRID_W).astype(F32)
    n_freq = DIFF_QK // 4
    inv = ROPE_THETA ** (-jnp.arange(n_freq, dtype=F32) / n_freq)
    ang = jnp.concatenate([row[:, None] * inv, col[:, None] * inv], axis=-1)
    ang2 = jnp.tile(jnp.repeat(ang, 2, axis=1), (1, LANES // DIFF_QK))
    sign = jnp.where(jnp.arange(LANES) % 2 == 0, -1.0, 1.0).astype(F32)
    return jnp.cos(ang2), jnp.sin(ang2) * sign


def _head_major_cols(w):
    d = w.shape[0]
    return w.reshape(d, 2, DIFF_HEADS, DIFF_QK).transpose(0, 2, 1, 3).reshape(d, -1)


def kernel(x, c, ctx, c_ctx, w_ada, b_ada, g_norm, w_in, na_rel_bias, diff_lambda, diff_subln,
           w_out, peer_wq, peer_keys, peer_u, peer_v):
    assert w_ada.shape[0] == 1, "single-layer kernel"
    b, t, d = x.shape
    n_ctx = ctx.shape[1]
    rows = t // GRID_W
    assert t % (NA_QROWS * GRID_W) == 0 and rows >= NA_KROWS

    pad = (-(b + 1)) % 8
    cc = jnp.concatenate([c, c_ctx[None], jnp.zeros((pad, d), F32)], axis=0)
    mod = _ada(cc, w_ada[0], b_ada[0][None]).reshape(b + 1 + pad, N_MOD, d)
    lat = lambda i: mod[:b, i][:, None, :]
    cxm = lambda i: mod[b:b + 1, i][:, None, :]

    gw = NA_WIDTH
    w_in0 = w_in[0]
    w_perm = jnp.concatenate([w_in0[:, :3 * gw], _head_major_cols(w_in0[:, 3 * gw:4 * gw]),
                              _head_major_cols(w_in0[:, 4 * gw:5 * gw]), w_in0[:, 5 * gw:]], axis=1).astype(BF16)
    cos2, sin2 = _rope_tables(t)
    g0 = g_norm[0, 0][None]
    qa, ka, va, qd, kd, vd = _qkv(x, lat(0), lat(1), g0, w_perm, cos2, sin2,
                                  rope=True, per_batch_mod=True, tm=512)
    _, ka_c, va_c, _, kd_c, vd_c = _qkv(ctx, cxm(0), cxm(1), g0, w_perm, cos2[:n_ctx], sin2[:n_ctx],
                                        rope=False, per_batch_mod=False, tm=n_ctx)

    out_a = _na(qa, ka, va, ka_c, va_c, *_na_bias_table(na_rel_bias[0], rows))
    out_d = _diff(diff_lambda[0], qd, kd, vd, kd_c, vd_c, diff_subln[0][None], tq=DIFF_TQ, tk=DIFF_TK)

    keys = peer_keys[0].reshape(2 * PEER_HEADS, PEER_NKEYS, -1).astype(BF16)
    x1, h2, st = _out(out_a, out_d, x, lat(2), lat(3), lat(4), g_norm[0, 1][None], g_norm[0, 2][None],
                      w_out[0].astype(BF16), peer_wq[0].astype(BF16), keys, tm=256)

    idx4, gate2 = _topk(st, tl=256)
    idx4 = idx4.reshape(b * t, PEER_NJ)
    gate2 = gate2.reshape(b * t, 2 * PEER_NJ)

    w2 = _peer_u(idx4, h2, gate2, _pack_table(peer_u[0]))
    y3 = _peer_v(idx4, w2, _pack_table(peer_v[0]))
    return _final(x1, y3, lat(5), g_norm[0, 3][None], tm=512)
```

```python
import functools
import math

import numpy as np
import jax
import jax.numpy as jnp
from jax import lax
from jax.experimental import pallas as pl
from jax.experimental.pallas import tpu as pltpu

F32 = jnp.float32
BF16 = jnp.bfloat16
I32 = jnp.int32

GRID_W = 64
HEAD_DIM = 64
NA_HEADS = 8
NA_WIDTH = NA_HEADS * HEAD_DIM
NA_WIN_R = 8
NA_WIN_C = 16
DIFF_HEADS = 4
DIFF_QK = HEAD_DIM
DIFF_V = 2 * DIFF_QK
ROPE_THETA = 10000.0
PEER_HEADS = 8
PEER_NKEYS = 128
PEER_TOPK = 16
N_MOD = 6
EPS = 1e-6
LAMBDA_INIT = 0.8 - 0.6 * math.exp(-0.3 * 0)
LOG2E = math.log2(math.e)
DIFF_TQ = 512
DIFF_TK = 512

LANES = 128
VMEM_LIMIT = 56 * 1024 * 1024
NEG = -1e30

NA_QROWS = 8
NA_QSUBROWS = 2
NA_QSUB = NA_QSUBROWS * GRID_W
NA_KROWS = 10
PEER_TM = 128
PEER_NJ = PEER_HEADS * PEER_TOPK
TILE_STRIDE = PEER_NJ + 1
PEER_UNROLL = 8


def _dot_nt(a, b):
    return lax.dot_general(a, b, (((1,), (1,)), ((), ())), preferred_element_type=F32)


def _rms(x, g):
    return x * lax.rsqrt(jnp.mean(x * x, axis=-1, keepdims=True) + EPS) * g


def _ada_kernel(c_ref, w_ref, b_ref, o_ref):
    c = c_ref[...]
    s = c / (1.0 + jnp.exp(-c))
    o_ref[...] = jnp.dot(s, w_ref[...], precision=lax.Precision.HIGHEST,
                         preferred_element_type=F32) + b_ref[...]


def _ada(cc, w, b):
    m, d = cc.shape
    n = w.shape[1]
    tn = 1024
    return pl.pallas_call(
        _ada_kernel,
        grid=(n // tn,),
        in_specs=[pl.BlockSpec((m, d), lambda j: (0, 0)),
                  pl.BlockSpec((d, tn), lambda j: (0, j)),
                  pl.BlockSpec((1, tn), lambda j: (0, j))],
        out_specs=pl.BlockSpec((m, tn), lambda j: (0, j)),
        out_shape=jax.ShapeDtypeStruct((m, n), F32),
        compiler_params=pltpu.CompilerParams(vmem_limit_bytes=VMEM_LIMIT),
        name="ada",
    )(cc, w, b)


def _qkv_kernel(x_ref, sh_ref, sc_ref, g_ref, w_ref, cos_ref, sin_ref,
                qa_ref, ka_ref, va_ref, qd_ref, kd_ref, vd_ref, *, rope):
    x = x_ref[0]
    h = _rms(x, g_ref[...]) * (1.0 + sc_ref[0]) + sh_ref[0]
    hb = h.astype(BF16)
    gw = NA_WIDTH
    scale = HEAD_DIM ** -0.5

    def proj(g):
        return jnp.dot(hb, w_ref[:, g * gw:(g + 1) * gw], preferred_element_type=F32)

    def roped(p):
        if not rope:
            return p
        cos2 = cos_ref[...]
        sin2 = sin_ref[...]
        even = (lax.broadcasted_iota(I32, cos2.shape, 1) % 2) == 0
        outs = []
        for c in range(gw // LANES):
            v = p[:, c * LANES:(c + 1) * LANES]
            nxt = pltpu.roll(v, LANES - 1, axis=1)
            prv = pltpu.roll(v, 1, axis=1)
            outs.append(v * cos2 + jnp.where(even, nxt, prv) * sin2)
        return jnp.concatenate(outs, axis=1)

    qa_ref[0] = (proj(0) * scale).astype(BF16)
    ka_ref[0] = proj(1).astype(BF16)
    va_ref[0] = proj(2).astype(BF16)
    qd_ref[0] = (roped(proj(3)) * (scale * LOG2E)).astype(BF16)
    kd_ref[0] = roped(proj(4)).astype(BF16)
    vd = proj(5).astype(BF16)
    ones = jnp.ones((vd.shape[0], DIFF_V), BF16)
    pieces = []
    for hd in range(DIFF_HEADS):
        pieces += [vd[:, hd * DIFF_V:(hd + 1) * DIFF_V], ones]
    vd_ref[0] = jnp.concatenate(pieces, axis=1)


def _qkv(x, sh, sc, g, w, cos2, sin2, *, rope, per_batch_mod, tm):
    b, l, d = x.shape
    n = w.shape[1]
    gw = NA_WIDTH
    mod_map = (lambda i, j: (i, 0, 0)) if per_batch_mod else (lambda i, j: (0, 0, 0))
    out_sds = jax.ShapeDtypeStruct((b, l, gw), BF16)
    out_spec = pl.BlockSpec((1, tm, gw), lambda i, j: (i, j, 0))
    return pl.pallas_call(
        functools.partial(_qkv_kernel, rope=rope),
        grid=(b, l // tm),
        in_specs=[pl.BlockSpec((1, tm, d), lambda i, j: (i, j, 0)),
                  pl.BlockSpec((1, 1, d), mod_map),
                  pl.BlockSpec((1, 1, d), mod_map),
                  pl.BlockSpec((1, d), lambda i, j: (0, 0)),
                  pl.BlockSpec((d, n), lambda i, j: (0, 0)),
                  pl.BlockSpec((tm, LANES), lambda i, j: (j, 0)),
                  pl.BlockSpec((tm, LANES), lambda i, j: (j, 0))],
        out_specs=[out_spec] * 5 + [pl.BlockSpec((1, tm, 2 * gw), lambda i, j: (i, j, 0))],
        out_shape=[out_sds] * 5 + [jax.ShapeDtypeStruct((b, l, 2 * gw), BF16)],
        compiler_params=pltpu.CompilerParams(vmem_limit_bytes=VMEM_LIMIT),
        name="qkv_rope" if rope else "qkv_ctx",
    )(x, sh, sc, g, w, cos2, sin2)


def _na_window(qr0, rows):
    r_start = jnp.clip(qr0 - NA_WIN_R // 2, 0, rows - NA_WIN_R)
    return jnp.minimum(r_start, rows - NA_KROWS)


def _na_kernel(pid_ref, q_ref, k_ref, v_ref, kc_ref, vc_ref, bias_ref, o_ref, *, rows):
    rb = pl.program_id(2)
    nk = NA_KROWS * GRID_W
    kc = kc_ref[0]
    vc = vc_ref[0]
    lane = lax.broadcasted_iota(I32, (1, LANES), 1)
    nsub = NA_QROWS // NA_QSUBROWS
    for s in range(nsub):
        sl = slice(s * NA_QSUB, (s + 1) * NA_QSUB)
        start = pl.multiple_of(_na_window(rb * NA_QROWS + s * NA_QSUBROWS, rows) * GRID_W, GRID_W)
        kw = k_ref[0, pl.ds(start, nk), :]
        vw = v_ref[0, pl.ds(start, nk), :]
        pid = pid_ref[rb * nsub + s]
        q = q_ref[0, sl, :]
        zero = jnp.zeros_like(q)
        q2 = jnp.concatenate([jnp.where(lane < HEAD_DIM, q, zero), jnp.where(lane >= HEAD_DIM, q, zero)], axis=0)
        s_loc = _dot_nt(q2, kw) + jnp.concatenate([bias_ref[pid, 0], bias_ref[pid, 1]], axis=0)
        s_ctx = _dot_nt(q2, kc)
        mx = jnp.maximum(jnp.max(s_loc, axis=-1, keepdims=True), jnp.max(s_ctx, axis=-1, keepdims=True))
        p_loc = jnp.exp(s_loc - mx)
        p_ctx = jnp.exp(s_ctx - mx)
        l = jnp.sum(p_loc, axis=-1, keepdims=True) + jnp.sum(p_ctx, axis=-1, keepdims=True)
        o = (jnp.dot(p_loc.astype(BF16), vw, preferred_element_type=F32)
             + jnp.dot(p_ctx.astype(BF16), vc, preferred_element_type=F32)) / l
        o_ref[0, sl, :] = jnp.where(lane < HEAD_DIM, o[:NA_QSUB], o[NA_QSUB:]).astype(BF16)


def _na_bias_table(rel_bias, rows):
    n_ro = 2 * NA_WIN_R - 1
    qc = np.arange(GRID_W)[:, None]
    kc = np.arange(GRID_W)[None, :]
    c_start = np.clip(qc - NA_WIN_C // 2, 0, GRID_W - NA_WIN_C)
    valid_c = (kc >= c_start) & (kc < c_start + NA_WIN_C)
    co = np.clip(kc - qc + NA_WIN_C - 1, 0, 2 * NA_WIN_C - 2)
    tz = jnp.take(rel_bias, jnp.asarray(co.reshape(-1), I32), axis=2).reshape(NA_HEADS, n_ro, GRID_W, GRID_W)
    tz = jnp.where(jnp.asarray(valid_c)[None, None], tz, NEG)
    tz = jnp.concatenate([tz, jnp.full((NA_HEADS, 1, GRID_W, GRID_W), NEG, F32)], axis=1)
    patterns, pid = [], []
    for qr0 in range(0, rows, NA_QSUBROWS):
        ws = min(int(np.clip(qr0 - NA_WIN_R // 2, 0, rows - NA_WIN_R)), rows - NA_KROWS)
        qr = qr0 + np.arange(NA_QSUBROWS)[:, None]
        kr = ws + np.arange(NA_KROWS)[None, :]
        r_start = np.clip(qr - NA_WIN_R // 2, 0, rows - NA_WIN_R)
        valid_r = (kr >= r_start) & (kr < r_start + NA_WIN_R)
        assert valid_r.sum(axis=1).min() == NA_WIN_R
        sel = tuple(np.where(valid_r, kr - qr + NA_WIN_R - 1, n_ro).reshape(-1))
        if sel not in patterns:
            patterns.append(sel)
        pid.append(patterns.index(sel))
    sel = np.asarray(patterns, np.int32).reshape(-1)
    blocks = jnp.take(tz, jnp.asarray(sel), axis=1)
    blocks = blocks.reshape(NA_HEADS, len(patterns), NA_QSUBROWS, NA_KROWS, GRID_W, GRID_W)
    table = blocks.transpose(1, 0, 2, 4, 3, 5).reshape(len(patterns), NA_HEADS, NA_QSUB, NA_KROWS * GRID_W)
    return table, np.asarray(pid, np.int32)


def _na(qa, ka, va, ka_c, va_c, bias, pid):
    b, t, _ = qa.shape
    c = ka_c.shape[1]
    rows = t // GRID_W
    nrb = rows // NA_QROWS
    tq = NA_QROWS * GRID_W
    nk = NA_KROWS * GRID_W
    npat = bias.shape[0]
    return pl.pallas_call(
        functools.partial(_na_kernel, rows=rows),
        grid=(b, NA_HEADS // 2, nrb),
        in_specs=[pl.BlockSpec(memory_space=pltpu.SMEM),
                  pl.BlockSpec((1, tq, LANES), lambda i, hp, rb: (i, rb, hp)),
                  pl.BlockSpec((1, t, LANES), lambda i, hp, rb: (i, 0, hp)),
                  pl.BlockSpec((1, t, LANES), lambda i, hp, rb: (i, 0, hp)),
                  pl.BlockSpec((1, c, LANES), lambda i, hp, rb: (i, 0, hp)),
                  pl.BlockSpec((1, c, LANES), lambda i, hp, rb: (i, 0, hp)),
                  pl.BlockSpec((npat, 2, NA_QSUB, nk), lambda i, hp, rb: (0, hp, 0, 0))],
        out_specs=pl.BlockSpec((1, tq, LANES), lambda i, hp, rb: (i, rb, hp)),
        out_shape=jax.ShapeDtypeStruct((b, t, NA_WIDTH), BF16),
        compiler_params=pltpu.CompilerParams(vmem_limit_bytes=VMEM_LIMIT),
        name="na_attn",
    )(jnp.asarray(pid), qa, ka, va, ka_c, va_c, bias)


def _diff_kernel(lam_ref, q_ref, k_ref, v_ref, kc_ref, vc_ref, g_ref, o_ref, *, tk):
    q = q_ref[0]
    tq = q.shape[0]
    t = k_ref.shape[1]
    lane = lax.broadcasted_iota(I32, (1, LANES), 1)
    zero = jnp.zeros_like(q)
    q2 = jnp.concatenate([jnp.where(lane < DIFF_QK, q, zero), jnp.where(lane >= DIFF_QK, q, zero)], axis=0)

    def update(kb, vb, carry):
        m, a = carry
        s = _dot_nt(q2, kb)
        mn = jnp.maximum(m, jnp.max(s, axis=-1, keepdims=True))
        p = jnp.exp2(s - mn)
        return mn, jnp.exp2(m - mn) * a + jnp.dot(p.astype(BF16), vb, preferred_element_type=F32)

    init = (jnp.full((2 * tq, 1), NEG, F32), jnp.zeros((2 * tq, 2 * DIFF_V), F32))
    carry = update(kc_ref[0], vc_ref[0], init)
    for i in range(t // tk):
        carry = update(k_ref[0, i * tk:(i + 1) * tk, :], v_ref[0, i * tk:(i + 1) * tk, :], carry)
    lp = lam_ref[...]
    lam = (jnp.exp(jnp.sum(lp[0:1] * lp[1:2], axis=-1, keepdims=True))
           - jnp.exp(jnp.sum(lp[2:3] * lp[3:4], axis=-1, keepdims=True)) + LAMBDA_INIT)
    a = carry[1]
    a = a[:, :DIFF_V] / a[:, DIFF_V:DIFF_V + 1]
    out = a[:tq] - lam * a[tq:]
    o_ref[0] = (_rms(out, g_ref[...]) * (1.0 - LAMBDA_INIT)).astype(BF16)


def _diff(lam_params, qd, kd, vd, kd_c, vd_c, subln, *, tq, tk):
    b, t, w = qd.shape
    c = kd_c.shape[1]
    return pl.pallas_call(
        functools.partial(_diff_kernel, tk=tk),
        grid=(b, DIFF_HEADS, t // tq),
        in_specs=[pl.BlockSpec((4, DIFF_QK), lambda i, h, j: (0, 0)),
                  pl.BlockSpec((1, tq, LANES), lambda i, h, j: (i, j, h)),
                  pl.BlockSpec((1, t, LANES), lambda i, h, j: (i, 0, h)),
                  pl.BlockSpec((1, t, 2 * DIFF_V), lambda i, h, j: (i, 0, h)),
                  pl.BlockSpec((1, c, LANES), lambda i, h, j: (i, 0, h)),
                  pl.BlockSpec((1, c, 2 * DIFF_V), lambda i, h, j: (i, 0, h)),
                  pl.BlockSpec((1, DIFF_V), lambda i, h, j: (0, 0))],
        out_specs=pl.BlockSpec((1, tq, LANES), lambda i, h, j: (i, j, h)),
        out_shape=jax.ShapeDtypeStruct((b, t, w), BF16),
        compiler_params=pltpu.CompilerParams(vmem_limit_bytes=VMEM_LIMIT),
        name="diff_attn",
    )(lam_params, qd, kd, vd, kd_c, vd_c, subln)


def _out_kernel(oa_ref, od_ref, x_ref, gt1_ref, sh2_ref, sc2_ref, g1_ref, g2_ref,
                wout_ref, wq_ref, keys_ref, x1_ref, h2_ref, st_ref):
    wa = NA_WIDTH
    y = (jnp.dot(oa_ref[0], wout_ref[0:wa, :], preferred_element_type=F32)
         + jnp.dot(od_ref[0], wout_ref[wa:, :], preferred_element_type=F32))
    x1 = x_ref[0] + gt1_ref[0] * _rms(y, g1_ref[...])
    x1_ref[0] = x1
    h2 = _rms(x1, g2_ref[...]) * (1.0 + sc2_ref[0]) + sh2_ref[0]
    h2_ref[...] = h2.reshape(h2_ref.shape)
    qb = jnp.dot(h2.astype(BF16), wq_ref[...], preferred_element_type=F32).astype(BF16)
    for hp in range(2 * PEER_HEADS):
        st_ref[0, hp] = _dot_nt(keys_ref[hp], qb[:, hp * LANES:(hp + 1) * LANES])


def _out(oa, od, x, gt1, sh2, sc2, g1, g2, wout, wq, keys, *, tm):
    b, t, d = x.shape
    nq = wq.shape[1]
    nhp = keys.shape[0]
    mod = pl.BlockSpec((1, 1, d), lambda i, j: (i, 0, 0))
    gsp = pl.BlockSpec((1, d), lambda i, j: (0, 0))
    tok = pl.BlockSpec((1, tm, d), lambda i, j: (i, j, 0))
    half = pl.BlockSpec((1, tm, NA_WIDTH), lambda i, j: (i, j, 0))
    return pl.pallas_call(
        _out_kernel,
        grid=(b, t // tm),
        in_specs=[half, half, tok, mod, mod, mod, gsp, gsp,
                  pl.BlockSpec((d, d), lambda i, j: (0, 0)),
                  pl.BlockSpec((d, nq), lambda i, j: (0, 0)),
                  pl.BlockSpec((nhp, PEER_NKEYS, LANES), lambda i, j: (0, 0, 0))],
        out_specs=[tok, pl.BlockSpec((tm, d // LANES, LANES), lambda i, j: (i * (t // tm) + j, 0, 0)),
                   pl.BlockSpec((1, nhp, PEER_NKEYS, tm), lambda i, j: (i, 0, 0, j))],
        out_shape=[jax.ShapeDtypeStruct((b, t, d), F32), jax.ShapeDtypeStruct((b * t, d // LANES, LANES), F32),
                   jax.ShapeDtypeStruct((b, nhp, PEER_NKEYS, t), F32)],
        compiler_params=pltpu.CompilerParams(vmem_limit_bytes=VMEM_LIMIT),
        name="out_proj_peer_scores",
    )(oa, od, x, gt1, sh2, sc2, g1, g2, wout, wq, keys)


def _top1(s, iota):
    m = jnp.max(s, axis=0, keepdims=True)
    idx = jnp.min(jnp.where(s == m, iota, s.shape[0]), axis=0, keepdims=True)
    return m, idx, jnp.where(iota == idx, -jnp.inf, s)


def _topk_kernel(st_ref, idx_ref, gate_ref):
    tl = st_ref.shape[-1]
    k = PEER_TOPK
    iota_n = lax.broadcasted_iota(I32, (PEER_NKEYS, tl), 0)
    widths = [k // (a + 1) for a in range(k)]
    n_cand = -(-sum(widths) // 8) * 8
    iota_c = lax.broadcasted_iota(I32, (n_cand, tl), 0)
    pad = n_cand - sum(widths)
    gates, rows = [], []
    for h in range(PEER_HEADS):
        tops = []
        for p in range(2):
            s = st_ref[0, 2 * h + p]
            vals, idxs = [], []
            for _ in range(k):
                m, i, s = _top1(s, iota_n)
                vals.append(m)
                idxs.append(i)
            tops.append((jnp.concatenate(vals, axis=0), jnp.concatenate(idxs, axis=0)))
        (s0, i0), (s1, i1) = tops
        cand = jnp.concatenate([s0[a:a + 1] + s1[0:widths[a]] for a in range(k)]
                               + [jnp.full((pad, tl), -jnp.inf, F32)], axis=0)
        cidx = jnp.concatenate([i0[a:a + 1] * PEER_NKEYS + i1[0:widths[a]] for a in range(k)]
                               + [jnp.zeros((pad, tl), I32)], axis=0)
        best, experts = [], []
        for _ in range(k):
            m, pos, cand = _top1(cand, iota_c)
            best.append(m)
            experts.append(jnp.sum(jnp.where(iota_c == pos, cidx, 0), axis=0, keepdims=True))
        best = jnp.concatenate(best, axis=0)
        e = jnp.exp(best - best[0:1])
        gates.append(e / jnp.sum(e, axis=0, keepdims=True))
        rows += experts
    idx_ref[0] = (jnp.concatenate(rows, axis=0) * 4).T
    nj = PEER_NJ
    dup = (lax.broadcasted_iota(I32, (nj, 2 * nj), 1) // 2 == lax.broadcasted_iota(I32, (nj, 2 * nj), 0))
    gate_ref[0] = jnp.dot(jnp.concatenate(gates, axis=0).T, dup.astype(F32),
                          precision=lax.Precision.HIGHEST, preferred_element_type=F32)


def _topk(st, *, tl):
    b, nhp, nkeys, t = st.shape
    nj = PEER_NJ
    return pl.pallas_call(
        _topk_kernel,
        grid=(b, t // tl),
        in_specs=[pl.BlockSpec((1, nhp, nkeys, tl), lambda i, j: (i, 0, 0, j))],
        out_specs=[pl.BlockSpec((1, tl, nj), lambda i, j: (i, j, 0)),
                   pl.BlockSpec((1, tl, 2 * nj), lambda i, j: (i, j, 0))],
        out_shape=[jax.ShapeDtypeStruct((b, t, nj), I32), jax.ShapeDtypeStruct((b, t, 2 * nj), F32)],
        compiler_params=pltpu.CompilerParams(vmem_limit_bytes=VMEM_LIMIT),
        name="peer_topk",
    )(st)


def _pack_table(tab):
    e, d = tab.shape
    tb = tab.astype(BF16)
    lo = lax.bitcast_convert_type(tb[:, :d // 2], jnp.uint16).astype(jnp.uint32)
    hi = lax.bitcast_convert_type(tb[:, d // 2:], jnp.uint16).astype(jnp.uint32)
    words = lax.bitcast_convert_type((hi << 16) | lo, I32)
    return words.reshape(e * (d // 2 // LANES), LANES)


def _gather_rows(idx_ref, trow, table_ref, tile_ref):
    ahead = 4
    rows = [idx_ref[trow, j] for j in range(ahead)]
    for j in range(0, PEER_NJ, 2):
        for k in range(2):
            if j + k + ahead < PEER_NJ:
                rows.append(idx_ref[trow, j + k + ahead])
        a = table_ref[pl.ds(pl.multiple_of(rows[j], 4), 4), :]
        b = table_ref[pl.ds(pl.multiple_of(rows[j + 1], 4), 4), :]
        tile_ref[4 * j:4 * j + 8, :] = jnp.concatenate([a, b], axis=0)


def _tile_chunk(tile_ref, c):
    return pltpu.bitcast(tile_ref[pl.ds(c, PEER_NJ, stride=4), :], BF16)


def _split_bf16(v):
    head = v.astype(BF16).astype(F32)
    return jnp.concatenate([head, v - head], axis=0)


def _idx_copy(idx_hbm, first_token, buf, sem):
    n = buf.shape[0]
    src = idx_hbm.at[pl.ds(pl.multiple_of(first_token, n), n), :]
    return pltpu.make_async_copy(src, buf, sem)


def _for_each_group(idx_hbm, ibufs, sems, group):
    step = pl.program_id(0)
    half = PEER_TM // 2
    nu = PEER_UNROLL

    @pl.when(step == 0)
    def _():
        for h in range(2):
            _idx_copy(idx_hbm, h * half, ibufs[h], sems.at[h]).start()

    for h in range(2):
        tok0 = step * PEER_TM + h * half
        _idx_copy(idx_hbm, tok0, ibufs[h], sems.at[h]).wait()
        for g in range(half // nu):
            group(ibufs[h], g * nu, h * half + g * nu)

        @pl.when(step + 1 < pl.num_programs(0))
        def _():
            _idx_copy(idx_hbm, tok0 + PEER_TM, ibufs[h], sems.at[h]).start()


def _peer_scratch():
    return ([pltpu.VMEM((4 * TILE_STRIDE + 4, LANES), I32) for _ in range(PEER_UNROLL)]
            + [pltpu.SMEM((PEER_TM // 2, PEER_NJ), I32) for _ in range(2)]
            + [pltpu.SemaphoreType.DMA((2,))])


def _peer_u_kernel(idx_hbm, x_ref, gate_ref, u_ref, w_ref, *scratch):
    nj = PEER_NJ
    nu = PEER_UNROLL
    tile_refs, ibufs, sems = scratch[:nu], scratch[nu:nu + 2], scratch[nu + 2]
    row16 = lax.broadcasted_iota(I32, (16, 1), 0) % 8
    even = (lax.broadcasted_iota(I32, (1, 2 * nj), 1) % 2) == 0

    def token(ibuf, off, t, tile_ref):
        _gather_rows(ibuf, off, u_ref, tile_ref)
        x16 = _split_bf16(x_ref[t])
        acc = jnp.zeros((16, 2 * nj), F32)
        for c in range(4):
            lhs = jnp.where((row16 == c) | (row16 == 4 + c), x16, 0.0).astype(BF16)
            acc = acc + _dot_nt(lhs, _tile_chunk(tile_ref, c))
        a = acc[0:8] + acc[8:16]
        return jnp.sum(jnp.where(even, a, pltpu.roll(a, 4, axis=0))[0:4], axis=0, keepdims=True)

    def group(ibuf, off, t0):
        z = jnp.concatenate([token(ibuf, off + i, t0 + i, tile_refs[i]) for i in range(nu)], axis=0)
        act = z + jnp.where(even, pltpu.roll(z, 2 * nj - 1, axis=1), pltpu.roll(z, 1, axis=1))
        gelu = 0.5 * act * (1.0 + lax.erf(act * (2.0 ** -0.5)))
        w_ref[t0:t0 + nu, :] = gate_ref[t0:t0 + nu, :] * gelu

    _for_each_group(idx_hbm, ibufs, sems, group)


def _peer_u(idx4, x3, gate2, table):
    n = x3.shape[0]
    nj = PEER_NJ
    tm = PEER_TM
    return pl.pallas_call(
        _peer_u_kernel,
        grid=(n // tm,),
        in_specs=[pl.BlockSpec(memory_space=pl.ANY),
                  pl.BlockSpec((tm, 8, LANES), lambda i: (i, 0, 0)),
                  pl.BlockSpec((tm, 2 * nj), lambda i: (i, 0)),
                  pl.BlockSpec(memory_space=pltpu.VMEM)],
        out_specs=pl.BlockSpec((tm, 2 * nj), lambda i: (i, 0)),
        out_shape=jax.ShapeDtypeStruct((n, 2 * nj), F32),
        scratch_shapes=_peer_scratch(),
        compiler_params=pltpu.CompilerParams(vmem_limit_bytes=VMEM_LIMIT, dimension_semantics=("arbitrary",)),
        name="peer_u",
    )(idx4, x3, gate2, table)


def _peer_v_kernel(idx_hbm, w_ref, v_ref, o_ref, *scratch):
    nj = PEER_NJ
    nu = PEER_UNROLL
    tile_refs, ibufs, sems = scratch[:nu], scratch[nu:nu + 2], scratch[nu + 2]
    row16 = lax.broadcasted_iota(I32, (16, 1), 0) % 8
    even = (lax.broadcasted_iota(I32, (1, 2 * nj), 1) % 2) == 0

    def token(ibuf, off, t, tile_ref):
        _gather_rows(ibuf, off, v_ref, tile_ref)
        w16 = _split_bf16(jnp.broadcast_to(w_ref[t:t + 1, :], (8, 2 * nj)))
        acc = jnp.zeros((16, LANES), F32)
        for c in range(4):
            keep = ((row16 == c) & even) | ((row16 == 4 + c) & jnp.logical_not(even))
            lhs = jnp.where(keep, w16, 0.0).astype(BF16)
            acc = acc + jnp.dot(lhs, _tile_chunk(tile_ref, c), preferred_element_type=F32)
        o_ref[t] = acc[0:8] + acc[8:16]

    def group(ibuf, off, t0):
        for i in range(nu):
            token(ibuf, off + i, t0 + i, tile_refs[i])

    _for_each_group(idx_hbm, ibufs, sems, group)


def _peer_v(idx4, w2, table):
    n = w2.shape[0]
    nj = PEER_NJ
    tm = PEER_TM
    return pl.pallas_call(
        _peer_v_kernel,
        grid=(n // tm,),
        in_specs=[pl.BlockSpec(memory_space=pl.ANY),
                  pl.BlockSpec((tm, 2 * nj), lambda i: (i, 0)),
                  pl.BlockSpec(memory_space=pltpu.VMEM)],
        out_specs=pl.BlockSpec((tm, 8, LANES), lambda i: (i, 0, 0)),
        out_shape=jax.ShapeDtypeStruct((n, 8, LANES), F32),
        scratch_shapes=_peer_scratch(),
        compiler_params=pltpu.CompilerParams(vmem_limit_bytes=VMEM_LIMIT, dimension_semantics=("arbitrary",)),
        name="peer_v",
    )(idx4, w2, table)


def _final_kernel(x1_ref, y_ref, gt2_ref, g_ref, o_ref):
    y = y_ref[...].reshape(x1_ref.shape[1:])
    o_ref[0] = x1_ref[0] + gt2_ref[0] * _rms(y, g_ref[...])


def _final(x1, y3, gt2, g, *, tm):
    b, t, d = x1.shape
    nt = t // tm
    tok = pl.BlockSpec((1, tm, d), lambda i, j: (i, j, 0))
    return pl.pallas_call(
        _final_kernel,
        grid=(b, nt),
        in_specs=[tok, pl.BlockSpec((tm, d // LANES, LANES), lambda i, j: (i * nt + j, 0, 0)),
                  pl.BlockSpec((1, 1, d), lambda i, j: (i, 0, 0)),
                  pl.BlockSpec((1, d), lambda i, j: (0, 0))],
        out_specs=tok,
        out_shape=jax.ShapeDtypeStruct((b, t, d), F32),
        compiler_params=pltpu.CompilerParams(vmem_limit_bytes=VMEM_LIMIT),
        name="final_residual",
    )(x1, y3, gt2, g)


def _rope_tables(t):
    tok = jnp.arange(t)
    row = (tok // GRID_W).astype(F32)
    col = (tok % GRID_W).astype(F32)
    n_freq = DIFF_QK // 4
    inv = ROPE_THETA ** (-jnp.arange(n_freq, dtype=F32) / n_freq)
    ang = jnp.concatenate([row[:, None] * inv, col[:, None] * inv], axis=-1)
    ang2 = jnp.tile(jnp.repeat(ang, 2, axis=1), (1, LANES // DIFF_QK))
    sign = jnp.where(jnp.arange(LANES) % 2 == 0, -1.0, 1.0).astype(F32)
    return jnp.cos(ang2), jnp.sin(ang2) * sign


def _head_major_cols(w):
    d = w.shape[0]
    return w.reshape(d, 2, DIFF_HEADS, DIFF_QK).transpose(0, 2, 1, 3).reshape(d, -1)


def kernel(x, c, ctx, c_ctx, w_ada, b_ada, g_norm, w_in, na_rel_bias, diff_lambda, diff_subln,
           w_out, peer_wq, peer_keys, peer_u, peer_v):
    assert w_ada.shape[0] == 1, "single-layer kernel"
    b, t, d = x.shape
    n_ctx = ctx.shape[1]
    rows = t // GRID_W
    assert t % (NA_QROWS * GRID_W) == 0 and rows >= NA_KROWS

    pad = (-(b + 1)) % 8
    cc = jnp.concatenate([c, c_ctx[None], jnp.zeros((pad, d), F32)], axis=0)
    mod = _ada(cc, w_ada[0], b_ada[0][None]).reshape(b + 1 + pad, N_MOD, d)
    lat = lambda i: mod[:b, i][:, None, :]
    cxm = lambda i: mod[b:b + 1, i][:, None, :]

    gw = NA_WIDTH
    w_in0 = w_in[0]
    w_perm = jnp.concatenate([w_in0[:, :3 * gw], _head_major_cols(w_in0[:, 3 * gw:4 * gw]),
                              _head_major_cols(w_in0[:, 4 * gw:5 * gw]), w_in0[:, 5 * gw:]], axis=1).astype(BF16)
    cos2, sin2 = _rope_tables(t)
    g0 = g_norm[0, 0][None]
    qa, ka, va, qd, kd, vd = _qkv(x, lat(0), lat(1), g0, w_perm, cos2, sin2,
                                  rope=True, per_batch_mod=True, tm=512)
    _, ka_c, va_c, _, kd_c, vd_c = _qkv(ctx, cxm(0), cxm(1), g0, w_perm, cos2[:n_ctx], sin2[:n_ctx],
                                        rope=False, per_batch_mod=False, tm=n_ctx)

    out_a = _na(qa, ka, va, ka_c, va_c, *_na_bias_table(na_rel_bias[0], rows))
    out_d = _diff(diff_lambda[0], qd, kd, vd, kd_c, vd_c, diff_subln[0][None], tq=DIFF_TQ, tk=DIFF_TK)

    keys = peer_keys[0].reshape(2 * PEER_HEADS, PEER_NKEYS, -1).astype(BF16)
    x1, h2, st = _out(out_a, out_d, x, lat(2), lat(3), lat(4), g_norm[0, 1][None], g_norm[0, 2][None],
                      w_out[0].astype(BF16), peer_wq[0].astype(BF16), keys, tm=512)

    idx4, gate2 = _topk(st, tl=256)
    idx4 = idx4.reshape(b * t, PEER_NJ)
    gate2 = gate2.reshape(b * t, 2 * PEER_NJ)

    w2 = _peer_u(idx4, h2, gate2, _pack_table(peer_u[0]))
    y3 = _peer_v(idx4, w2, _pack_table(peer_v[0]))
    return _final(x1, y3, lat(5), g_norm[0, 3][None], tm=512)
```

```python
import functools
import math

import numpy as np
import jax
import jax.numpy as jnp
from jax import lax
from jax.experimental import pallas as pl
from jax.experimental.pallas import tpu as pltpu

F32 = jnp.float32
BF16 = jnp.bfloat16
I32 = jnp.int32

GRID_W = 64
HEAD_DIM = 64
NA_HEADS = 8
NA_WIDTH = NA_HEADS * HEAD_DIM
NA_WIN_R = 8
NA_WIN_C = 16
DIFF_HEADS = 4
DIFF_QK = HEAD_DIM
DIFF_V = 2 * DIFF_QK
ROPE_THETA = 10000.0
PEER_HEADS = 8
PEER_NKEYS = 128
PEER_TOPK = 16
N_MOD = 6
EPS = 1e-6
LAMBDA_INIT = 0.8 - 0.6 * math.exp(-0.3 * 0)
LOG2E = math.log2(math.e)
DIFF_TQ = 512
DIFF_TK = 512

LANES = 128
VMEM_LIMIT = 56 * 1024 * 1024
NEG = -1e30

NA_QROWS = 8
NA_QSUBROWS = 2
NA_QSUB = NA_QSUBROWS * GRID_W
NA_KROWS = 10
PEER_TM = 128
PEER_NJ = PEER_HEADS * PEER_TOPK
TILE_STRIDE = PEER_NJ + 1
PEER_UNROLL = 8


def _dot_nt(a, b):
    return lax.dot_general(a, b, (((1,), (1,)), ((), ())), preferred_element_type=F32)


def _rms(x, g):
    return x * lax.rsqrt(jnp.mean(x * x, axis=-1, keepdims=True) + EPS) * g


def _ada_kernel(c_ref, w_ref, b_ref, o_ref):
    c = c_ref[...]
    s = c / (1.0 + jnp.exp(-c))
    o_ref[...] = jnp.dot(s, w_ref[...], precision=lax.Precision.HIGHEST,
                         preferred_element_type=F32) + b_ref[...]


def _ada(cc, w, b):
    m, d = cc.shape
    n = w.shape[1]
    tn = 1024
    return pl.pallas_call(
        _ada_kernel,
        grid=(n // tn,),
        in_specs=[pl.BlockSpec((m, d), lambda j: (0, 0)),
                  pl.BlockSpec((d, tn), lambda j: (0, j)),
                  pl.BlockSpec((1, tn), lambda j: (0, j))],
        out_specs=pl.BlockSpec((m, tn), lambda j: (0, j)),
        out_shape=jax.ShapeDtypeStruct((m, n), F32),
        compiler_params=pltpu.CompilerParams(vmem_limit_bytes=VMEM_LIMIT),
        name="ada",
    )(cc, w, b)


def _qkv_kernel(x_ref, sh_ref, sc_ref, g_ref, w_ref, cos_ref, sin_ref,
                qa_ref, ka_ref, va_ref, qd_ref, kd_ref, vd_ref, *, rope):
    x = x_ref[0]
    h = _rms(x, g_ref[...]) * (1.0 + sc_ref[0]) + sh_ref[0]
    hb = h.astype(BF16)
    gw = NA_WIDTH
    scale = HEAD_DIM ** -0.5

    def proj(g):
        return jnp.dot(hb, w_ref[:, g * gw:(g + 1) * gw], preferred_element_type=F32)

    def roped(p):
        if not rope:
            return p
        cos2 = cos_ref[...]
        sin2 = sin_ref[...]
        even = (lax.broadcasted_iota(I32, cos2.shape, 1) % 2) == 0
        outs = []
        for c in range(gw // LANES):
            v = p[:, c * LANES:(c + 1) * LANES]
            nxt = pltpu.roll(v, LANES - 1, axis=1)
            prv = pltpu.roll(v, 1, axis=1)
            outs.append(v * cos2 + jnp.where(even, nxt, prv) * sin2)
        return jnp.concatenate(outs, axis=1)

    qa_ref[0] = (proj(0) * scale).astype(BF16)
    ka_ref[0] = proj(1).astype(BF16)
    va_ref[0] = proj(2).astype(BF16)
    qd_ref[0] = (roped(proj(3)) * (scale * LOG2E)).astype(BF16)
    kd_ref[0] = roped(proj(4)).astype(BF16)
    vd = proj(5).astype(BF16)
    ones = jnp.ones((vd.shape[0], DIFF_V), BF16)
    pieces = []
    for hd in range(DIFF_HEADS):
        pieces += [vd[:, hd * DIFF_V:(hd + 1) * DIFF_V], ones]
    vd_ref[0] = jnp.concatenate(pieces, axis=1)


def _qkv(x, sh, sc, g, w, cos2, sin2, *, rope, per_batch_mod, tm):
    b, l, d = x.shape
    n = w.shape[1]
    gw = NA_WIDTH
    mod_map = (lambda i, j: (i, 0, 0)) if per_batch_mod else (lambda i, j: (0, 0, 0))
    out_sds = jax.ShapeDtypeStruct((b, l, gw), BF16)
    out_spec = pl.BlockSpec((1, tm, gw), lambda i, j: (i, j, 0))
    return pl.pallas_call(
        functools.partial(_qkv_kernel, rope=rope),
        grid=(b, l // tm),
        in_specs=[pl.BlockSpec((1, tm, d), lambda i, j: (i, j, 0)),
                  pl.BlockSpec((1, 1, d), mod_map),
                  pl.BlockSpec((1, 1, d), mod_map),
                  pl.BlockSpec((1, d), lambda i, j: (0, 0)),
                  pl.BlockSpec((d, n), lambda i, j: (0, 0)),
                  pl.BlockSpec((tm, LANES), lambda i, j: (j, 0)),
                  pl.BlockSpec((tm, LANES), lambda i, j: (j, 0))],
        out_specs=[out_spec] * 5 + [pl.BlockSpec((1, tm, 2 * gw), lambda i, j: (i, j, 0))],
        out_shape=[out_sds] * 5 + [jax.ShapeDtypeStruct((b, l, 2 * gw), BF16)],
        compiler_params=pltpu.CompilerParams(vmem_limit_bytes=VMEM_LIMIT),
        name="qkv_rope" if rope else "qkv_ctx",
    )(x, sh, sc, g, w, cos2, sin2)


def _na_window(qr0, rows):
    r_start = jnp.clip(qr0 - NA_WIN_R // 2, 0, rows - NA_WIN_R)
    return jnp.minimum(r_start, rows - NA_KROWS)


def _na_kernel(pid_ref, q_ref, k_ref, v_ref, kc_ref, vc_ref, bias_ref, o_ref, *, rows):
    rb = pl.program_id(2)
    nk = NA_KROWS * GRID_W
    kc = kc_ref[0]
    vc = vc_ref[0]
    lane = lax.broadcasted_iota(I32, (1, LANES), 1)
    nsub = NA_QROWS // NA_QSUBROWS
    for s in range(nsub):
        sl = slice(s * NA_QSUB, (s + 1) * NA_QSUB)
        start = pl.multiple_of(_na_window(rb * NA_QROWS + s * NA_QSUBROWS, rows) * GRID_W, GRID_W)
        kw = k_ref[0, pl.ds(start, nk), :]
        vw = v_ref[0, pl.ds(start, nk), :]
        pid = pid_ref[rb * nsub + s]
        q = q_ref[0, sl, :]
        zero = jnp.zeros_like(q)
        q2 = jnp.concatenate([jnp.where(lane < HEAD_DIM, q, zero), jnp.where(lane >= HEAD_DIM, q, zero)], axis=0)
        s_loc = _dot_nt(q2, kw) + jnp.concatenate([bias_ref[pid, 0], bias_ref[pid, 1]], axis=0)
        s_ctx = _dot_nt(q2, kc)
        mx = jnp.maximum(jnp.max(s_loc, axis=-1, keepdims=True), jnp.max(s_ctx, axis=-1, keepdims=True))
        p_loc = jnp.exp(s_loc - mx)
        p_ctx = jnp.exp(s_ctx - mx)
        l = jnp.sum(p_loc, axis=-1, keepdims=True) + jnp.sum(p_ctx, axis=-1, keepdims=True)
        o = (jnp.dot(p_loc.astype(BF16), vw, preferred_element_type=F32)
             + jnp.dot(p_ctx.astype(BF16), vc, preferred_element_type=F32)) / l
        o_ref[0, sl, :] = jnp.where(lane < HEAD_DIM, o[:NA_QSUB], o[NA_QSUB:]).astype(BF16)


def _na_bias_table(rel_bias, rows):
    n_ro = 2 * NA_WIN_R - 1
    qc = np.arange(GRID_W)[:, None]
    kc = np.arange(GRID_W)[None, :]
    c_start = np.clip(qc - NA_WIN_C // 2, 0, GRID_W - NA_WIN_C)
    valid_c = (kc >= c_start) & (kc < c_start + NA_WIN_C)
    co = np.clip(kc - qc + NA_WIN_C - 1, 0, 2 * NA_WIN_C - 2)
    tz = jnp.take(rel_bias, jnp.asarray(co.reshape(-1), I32), axis=2).reshape(NA_HEADS, n_ro, GRID_W, GRID_W)
    tz = jnp.where(jnp.asarray(valid_c)[None, None], tz, NEG)
    tz = jnp.concatenate([tz, jnp.full((NA_HEADS, 1, GRID_W, GRID_W), NEG, F32)], axis=1)
    patterns, pid = [], []
    for qr0 in range(0, rows, NA_QSUBROWS):
        ws = min(int(np.clip(qr0 - NA_WIN_R // 2, 0, rows - NA_WIN_R)), rows - NA_KROWS)
        qr = qr0 + np.arange(NA_QSUBROWS)[:, None]
        kr = ws + np.arange(NA_KROWS)[None, :]
        r_start = np.clip(qr - NA_WIN_R // 2, 0, rows - NA_WIN_R)
        valid_r = (kr >= r_start) & (kr < r_start + NA_WIN_R)
        assert valid_r.sum(axis=1).min() == NA_WIN_R
        sel = tuple(np.where(valid_r, kr - qr + NA_WIN_R - 1, n_ro).reshape(-1))
        if sel not in patterns:
            patterns.append(sel)
        pid.append(patterns.index(sel))
    sel = np.asarray(patterns, np.int32).reshape(-1)
    blocks = jnp.take(tz, jnp.asarray(sel), axis=1)
    blocks = blocks.reshape(NA_HEADS, len(patterns), NA_QSUBROWS, NA_KROWS, GRID_W, GRID_W)
    table = blocks.transpose(1, 0, 2, 4, 3, 5).reshape(len(patterns), NA_HEADS, NA_QSUB, NA_KROWS * GRID_W)
    return table, np.asarray(pid, np.int32)


def _na(qa, ka, va, ka_c, va_c, bias, pid):
    b, t, _ = qa.shape
    c = ka_c.shape[1]
    rows = t // GRID_W
    nrb = rows // NA_QROWS
    tq = NA_QROWS * GRID_W
    nk = NA_KROWS * GRID_W
    npat = bias.shape[0]
    return pl.pallas_call(
        functools.partial(_na_kernel, rows=rows),
        grid=(b, NA_HEADS // 2, nrb),
        in_specs=[pl.BlockSpec(memory_space=pltpu.SMEM),
                  pl.BlockSpec((1, tq, LANES), lambda i, hp, rb: (i, rb, hp)),
                  pl.BlockSpec((1, t, LANES), lambda i, hp, rb: (i, 0, hp)),
                  pl.BlockSpec((1, t, LANES), lambda i, hp, rb: (i, 0, hp)),
                  pl.BlockSpec((1, c, LANES), lambda i, hp, rb: (i, 0, hp)),
                  pl.BlockSpec((1, c, LANES), lambda i, hp, rb: (i, 0, hp)),
                  pl.BlockSpec((npat, 2, NA_QSUB, nk), lambda i, hp, rb: (0, hp, 0, 0))],
        out_specs=pl.BlockSpec((1, tq, LANES), lambda i, hp, rb: (i, rb, hp)),
        out_shape=jax.ShapeDtypeStruct((b, t, NA_WIDTH), BF16),
        compiler_params=pltpu.CompilerParams(vmem_limit_bytes=VMEM_LIMIT),
        name="na_attn",
    )(jnp.asarray(pid), qa, ka, va, ka_c, va_c, bias)


def _diff_kernel(lam_ref, q_ref, k_ref, v_ref, kc_ref, vc_ref, g_ref, o_ref, *, tk):
    q = q_ref[0]
    tq = q.shape[0]
    t = k_ref.shape[1]
    lane = lax.broadcasted_iota(I32, (1, LANES), 1)
    zero = jnp.zeros_like(q)
    q2 = jnp.concatenate([jnp.where(lane < DIFF_QK, q, zero), jnp.where(lane >= DIFF_QK, q, zero)], axis=0)

    def update(kb, vb, carry):
        m, a = carry
        s = _dot_nt(q2, kb)
        mn = jnp.maximum(m, jnp.max(s, axis=-1, keepdims=True))
        p = jnp.exp2(s - mn)
        return mn, jnp.exp2(m - mn) * a + jnp.dot(p.astype(BF16), vb, preferred_element_type=F32)

    init = (jnp.full((2 * tq, 1), NEG, F32), jnp.zeros((2 * tq, 2 * DIFF_V), F32))
    carry = update(kc_ref[0], vc_ref[0], init)
    for i in range(t // tk):
        carry = update(k_ref[0, i * tk:(i + 1) * tk, :], v_ref[0, i * tk:(i + 1) * tk, :], carry)
    lp = lam_ref[...]
    lam = (jnp.exp(jnp.sum(lp[0:1] * lp[1:2], axis=-1, keepdims=True))
           - jnp.exp(jnp.sum(lp[2:3] * lp[3:4], axis=-1, keepdims=True)) + LAMBDA_INIT)
    a = carry[1]
    a = a[:, :DIFF_V] / a[:, DIFF_V:DIFF_V + 1]
    out = a[:tq] - lam * a[tq:]
    o_ref[0] = (_rms(out, g_ref[...]) * (1.0 - LAMBDA_INIT)).astype(BF16)


def _diff(lam_params, qd, kd, vd, kd_c, vd_c, subln, *, tq, tk):
    b, t, w = qd.shape
    c = kd_c.shape[1]
    return pl.pallas_call(
        functools.partial(_diff_kernel, tk=tk),
        grid=(b, DIFF_HEADS, t // tq),
        in_specs=[pl.BlockSpec((4, DIFF_QK), lambda i, h, j: (0, 0)),
                  pl.BlockSpec((1, tq, LANES), lambda i, h, j: (i, j, h)),
                  pl.BlockSpec((1, t, LANES), lambda i, h, j: (i, 0, h)),
                  pl.BlockSpec((1, t, 2 * DIFF_V), lambda i, h, j: (i, 0, h)),
                  pl.BlockSpec((1, c, LANES), lambda i, h, j: (i, 0, h)),
                  pl.BlockSpec((1, c, 2 * DIFF_V), lambda i, h, j: (i, 0, h)),
                  pl.BlockSpec((1, DIFF_V), lambda i, h, j: (0, 0))],
        out_specs=pl.BlockSpec((1, tq, LANES), lambda i, h, j: (i, j, h)),
        out_shape=jax.ShapeDtypeStruct((b, t, w), BF16),
        compiler_params=pltpu.CompilerParams(vmem_limit_bytes=VMEM_LIMIT),
        name="diff_attn",
    )(lam_params, qd, kd, vd, kd_c, vd_c, subln)


def _out_kernel(oa_ref, od_ref, x_ref, gt1_ref, sh2_ref, sc2_ref, g1_ref, g2_ref,
                wout_ref, wq_ref, keys_ref, x1_ref, h2_ref, st_ref):
    wa = NA_WIDTH
    y = (jnp.dot(oa_ref[0], wout_ref[0:wa, :], preferred_element_type=F32)
         + jnp.dot(od_ref[0], wout_ref[wa:, :], preferred_element_type=F32))
    x1 = x_ref[0] + gt1_ref[0] * _rms(y, g1_ref[...])
    x1_ref[0] = x1
    h2 = _rms(x1, g2_ref[...]) * (1.0 + sc2_ref[0]) + sh2_ref[0]
    h2_ref[...] = h2.reshape(h2_ref.shape)
    qb = jnp.dot(h2.astype(BF16), wq_ref[...], preferred_element_type=F32).astype(BF16)
    for hp in range(2 * PEER_HEADS):
        st_ref[0, hp] = _dot_nt(keys_ref[hp], qb[:, hp * LANES:(hp + 1) * LANES])


def _out(oa, od, x, gt1, sh2, sc2, g1, g2, wout, wq, keys, *, tm):
    b, t, d = x.shape
    nq = wq.shape[1]
    nhp = keys.shape[0]
    mod = pl.BlockSpec((1, 1, d), lambda i, j: (i, 0, 0))
    gsp = pl.BlockSpec((1, d), lambda i, j: (0, 0))
    tok = pl.BlockSpec((1, tm, d), lambda i, j: (i, j, 0))
    half = pl.BlockSpec((1, tm, NA_WIDTH), lambda i, j: (i, j, 0))
    return pl.pallas_call(
        _out_kernel,
        grid=(b, t // tm),
        in_specs=[half, half, tok, mod, mod, mod, gsp, gsp,
                  pl.BlockSpec((d, d), lambda i, j: (0, 0)),
                  pl.BlockSpec((d, nq), lambda i, j: (0, 0)),
                  pl.BlockSpec((nhp, PEER_NKEYS, LANES), lambda i, j: (0, 0, 0))],
        out_specs=[tok, pl.BlockSpec((tm, d // LANES, LANES), lambda i, j: (i * (t // tm) + j, 0, 0)),
                   pl.BlockSpec((1, nhp, PEER_NKEYS, tm), lambda i, j: (i, 0, 0, j))],
        out_shape=[jax.ShapeDtypeStruct((b, t, d), F32), jax.ShapeDtypeStruct((b * t, d // LANES, LANES), F32),
                   jax.ShapeDtypeStruct((b, nhp, PEER_NKEYS, t), F32)],
        compiler_params=pltpu.CompilerParams(vmem_limit_bytes=VMEM_LIMIT),
        name="out_proj_peer_scores",
    )(oa, od, x, gt1, sh2, sc2, g1, g2, wout, wq, keys)


def _top1(s, iota):
    m = jnp.max(s, axis=0, keepdims=True)
    idx = jnp.min(jnp.where(s == m, iota, s.shape[0]), axis=0, keepdims=True)
    return m, idx, jnp.where(iota == idx, -jnp.inf, s)


def _topk_kernel(st_ref, idx_ref, gate_ref):
    tl = st_ref.shape[-1]
    k = PEER_TOPK
    iota_n = lax.broadcasted_iota(I32, (PEER_NKEYS, tl), 0)
    widths = [k // (a + 1) for a in range(k)]
    n_cand = -(-sum(widths) // 8) * 8
    iota_c = lax.broadcasted_iota(I32, (n_cand, tl), 0)
    pad = n_cand - sum(widths)
    gates, rows = [], []
    for h in range(PEER_HEADS):
        tops = []
        for p in range(2):
            s = st_ref[0, 2 * h + p]
            vals, idxs = [], []
            for _ in range(k):
                m, i, s = _top1(s, iota_n)
                vals.append(m)
                idxs.append(i)
            tops.append((jnp.concatenate(vals, axis=0), jnp.concatenate(idxs, axis=0)))
        (s0, i0), (s1, i1) = tops
        cand = jnp.concatenate([s0[a:a + 1] + s1[0:widths[a]] for a in range(k)]
                               + [jnp.full((pad, tl), -jnp.inf, F32)], axis=0)
        cidx = jnp.concatenate([i0[a:a + 1] * PEER_NKEYS + i1[0:widths[a]] for a in range(k)]
                               + [jnp.zeros((pad, tl), I32)], axis=0)
        best, experts = [], []
        for _ in range(k):
            m, pos, cand = _top1(cand, iota_c)
            best.append(m)
            experts.append(jnp.sum(jnp.where(iota_c == pos, cidx, 0), axis=0, keepdims=True))
        best = jnp.concatenate(best, axis=0)
        e = jnp.exp(best - best[0:1])
        gates.append(e / jnp.sum(e, axis=0, keepdims=True))
        rows += experts
    idx_ref[0] = (jnp.concatenate(rows, axis=0) * 4).T
    nj = PEER_NJ
    dup = (lax.broadcasted_iota(I32, (nj, 2 * nj), 1) // 2 == lax.broadcasted_iota(I32, (nj, 2 * nj), 0))
    gate_ref[0] = jnp.dot(jnp.concatenate(gates, axis=0).T, dup.astype(F32),
                          precision=lax.Precision.HIGHEST, preferred_element_type=F32)


def _topk(st, *, tl):
    b, nhp, nkeys, t = st.shape
    nj = PEER_NJ
    return pl.pallas_call(
        _topk_kernel,
        grid=(b, t // tl),
        in_specs=[pl.BlockSpec((1, nhp, nkeys, tl), lambda i, j: (i, 0, 0, j))],
        out_specs=[pl.BlockSpec((1, tl, nj), lambda i, j: (i, j, 0)),
                   pl.BlockSpec((1, tl, 2 * nj), lambda i, j: (i, j, 0))],
        out_shape=[jax.ShapeDtypeStruct((b, t, nj), I32), jax.ShapeDtypeStruct((b, t, 2 * nj), F32)],
        compiler_params=pltpu.CompilerParams(vmem_limit_bytes=VMEM_LIMIT),
        name="peer_topk",
    )(st)


def _pack_table(tab):
    e, d = tab.shape
    tb = tab.astype(BF16)
    lo = lax.bitcast_convert_type(tb[:, :d // 2], jnp.uint16).astype(jnp.uint32)
    hi = lax.bitcast_convert_type(tb[:, d // 2:], jnp.uint16).astype(jnp.uint32)
    words = lax.bitcast_convert_type((hi << 16) | lo, I32)
    return words.reshape(e * (d // 2 // LANES), LANES)


def _gather_rows(idx_ref, trow, table_ref, tile_ref):
    half = PEER_NJ // 2
    ahead = 2
    rows = {}
    for j in range(ahead):
        rows[j], rows[j + half] = idx_ref[trow, j], idx_ref[trow, j + half]
    for j in range(half):
        if j + ahead < half:
            rows[j + ahead], rows[j + ahead + half] = idx_ref[trow, j + ahead], idx_ref[trow, j + ahead + half]
        a = table_ref[pl.ds(pl.multiple_of(rows[j], 4), 4), :]
        b = table_ref[pl.ds(pl.multiple_of(rows[j + half], 4), 4), :]
        tile_ref[pl.ds(j, 8, stride=TILE_STRIDE), :] = jnp.concatenate([a, b], axis=0)


def _tile_chunk(tile_ref, c):
    half = PEER_NJ // 2
    lo, hi = c * TILE_STRIDE, (c + 4) * TILE_STRIDE
    return pltpu.bitcast(jnp.concatenate([tile_ref[lo:lo + half, :], tile_ref[hi:hi + half, :]], axis=0), BF16)


def _split_bf16(v):
    head = v.astype(BF16).astype(F32)
    return jnp.concatenate([head, v - head], axis=0)


def _idx_copy(idx_hbm, first_token, buf, sem):
    n = buf.shape[0]
    src = idx_hbm.at[pl.ds(pl.multiple_of(first_token, n), n), :]
    return pltpu.make_async_copy(src, buf, sem)


def _for_each_group(idx_hbm, ibufs, sems, group):
    step = pl.program_id(0)
    half = PEER_TM // 2
    nu = PEER_UNROLL

    @pl.when(step == 0)
    def _():
        for h in range(2):
            _idx_copy(idx_hbm, h * half, ibufs[h], sems.at[h]).start()

    for h in range(2):
        tok0 = step * PEER_TM + h * half
        _idx_copy(idx_hbm, tok0, ibufs[h], sems.at[h]).wait()
        for g in range(half // nu):
            group(ibufs[h], g * nu, h * half + g * nu)

        @pl.when(step + 1 < pl.num_programs(0))
        def _():
            _idx_copy(idx_hbm, tok0 + PEER_TM, ibufs[h], sems.at[h]).start()


def _peer_scratch():
    return ([pltpu.VMEM((8 * TILE_STRIDE + 8, LANES), I32) for _ in range(PEER_UNROLL)]
            + [pltpu.SMEM((PEER_TM // 2, PEER_NJ), I32) for _ in range(2)]
            + [pltpu.SemaphoreType.DMA((2,))])


def _peer_u_kernel(idx_hbm, x_ref, gate_ref, u_ref, w_ref, *scratch):
    nj = PEER_NJ
    nu = PEER_UNROLL
    tile_refs, ibufs, sems = scratch[:nu], scratch[nu:nu + 2], scratch[nu + 2]
    row16 = lax.broadcasted_iota(I32, (16, 1), 0) % 8
    even = (lax.broadcasted_iota(I32, (1, 2 * nj), 1) % 2) == 0

    def token(ibuf, off, t, tile_ref):
        _gather_rows(ibuf, off, u_ref, tile_ref)
        x16 = _split_bf16(x_ref[t])
        acc = jnp.zeros((16, 2 * nj), F32)
        for c in range(4):
            lhs = jnp.where((row16 == c) | (row16 == 4 + c), x16, 0.0).astype(BF16)
            acc = acc + _dot_nt(lhs, _tile_chunk(tile_ref, c))
        a = acc[0:8] + acc[8:16]
        return jnp.sum(jnp.where(even, a, pltpu.roll(a, 4, axis=0))[0:4], axis=0, keepdims=True)

    def group(ibuf, off, t0):
        z = jnp.concatenate([token(ibuf, off + i, t0 + i, tile_refs[i]) for i in range(nu)], axis=0)
        act = z + jnp.where(even, pltpu.roll(z, 2 * nj - 1, axis=1), pltpu.roll(z, 1, axis=1))
        gelu = 0.5 * act * (1.0 + lax.erf(act * (2.0 ** -0.5)))
        w_ref[t0:t0 + nu, :] = gate_ref[t0:t0 + nu, :] * gelu

    _for_each_group(idx_hbm, ibufs, sems, group)


def _peer_u(idx4, x3, gate2, table):
    n = x3.shape[0]
    nj = PEER_NJ
    tm = PEER_TM
    return pl.pallas_call(
        _peer_u_kernel,
        grid=(n // tm,),
        in_specs=[pl.BlockSpec(memory_space=pl.ANY),
                  pl.BlockSpec((tm, 8, LANES), lambda i: (i, 0, 0)),
                  pl.BlockSpec((tm, 2 * nj), lambda i: (i, 0)),
                  pl.BlockSpec(memory_space=pltpu.VMEM)],
        out_specs=pl.BlockSpec((tm, 2 * nj), lambda i: (i, 0)),
        out_shape=jax.ShapeDtypeStruct((n, 2 * nj), F32),
        scratch_shapes=_peer_scratch(),
        compiler_params=pltpu.CompilerParams(vmem_limit_bytes=VMEM_LIMIT, dimension_semantics=("arbitrary",)),
        name="peer_u",
    )(idx4, x3, gate2, table)


def _peer_v_kernel(idx_hbm, w_ref, v_ref, o_ref, *scratch):
    nj = PEER_NJ
    nu = PEER_UNROLL
    tile_refs, ibufs, sems = scratch[:nu], scratch[nu:nu + 2], scratch[nu + 2]
    row16 = lax.broadcasted_iota(I32, (16, 1), 0) % 8
    even = (lax.broadcasted_iota(I32, (1, 2 * nj), 1) % 2) == 0

    def token(ibuf, off, t, tile_ref):
        _gather_rows(ibuf, off, v_ref, tile_ref)
        w16 = _split_bf16(jnp.broadcast_to(w_ref[t:t + 1, :], (8, 2 * nj)))
        acc = jnp.zeros((16, LANES), F32)
        for c in range(4):
            keep = ((row16 == c) & even) | ((row16 == 4 + c) & jnp.logical_not(even))
            lhs = jnp.where(keep, w16, 0.0).astype(BF16)
            acc = acc + jnp.dot(lhs, _tile_chunk(tile_ref, c), preferred_element_type=F32)
        o_ref[t] = acc[0:8] + acc[8:16]

    def group(ibuf, off, t0):
        for i in range(nu):
            token(ibuf, off + i, t0 + i, tile_refs[i])

    _for_each_group(idx_hbm, ibufs, sems, group)


def _peer_v(idx4, w2, table):
    n = w2.shape[0]
    nj = PEER_NJ
    tm = PEER_TM
    return pl.pallas_call(
        _peer_v_kernel,
        grid=(n // tm,),
        in_specs=[pl.BlockSpec(memory_space=pl.ANY),
                  pl.BlockSpec((tm, 2 * nj), lambda i: (i, 0)),
                  pl.BlockSpec(memory_space=pltpu.VMEM)],
        out_specs=pl.BlockSpec((tm, 8, LANES), lambda i: (i, 0, 0)),
        out_shape=jax.ShapeDtypeStruct((n, 8, LANES), F32),
        scratch_shapes=_peer_scratch(),
        compiler_params=pltpu.CompilerParams(vmem_limit_bytes=VMEM_LIMIT, dimension_semantics=("arbitrary",)),
        name="peer_v",
    )(idx4, w2, table)


def _final_kernel(x1_ref, y_ref, gt2_ref, g_ref, o_ref):
    y = y_ref[...].reshape(x1_ref.shape[1:])
    o_ref[0] = x1_ref[0] + gt2_ref[0] * _rms(y, g_ref[...])


def _final(x1, y3, gt2, g, *, tm):
    b, t, d = x1.shape
    nt = t // tm
    tok = pl.BlockSpec((1, tm, d), lambda i, j: (i, j, 0))
    return pl.pallas_call(
        _final_kernel,
        grid=(b, nt),
        in_specs=[tok, pl.BlockSpec((tm, d // LANES, LANES), lambda i, j: (i * nt + j, 0, 0)),
                  pl.BlockSpec((1, 1, d), lambda i, j: (i, 0, 0)),
                  pl.BlockSpec((1, d), lambda i, j: (0, 0))],
        out_specs=tok,
        out_shape=jax.ShapeDtypeStruct((b, t, d), F32),
        compiler_params=pltpu.CompilerParams(vmem_limit_bytes=VMEM_LIMIT),
        name="final_residual",
    )(x1, y3, gt2, g)


def _rope_tables(t):
    tok = jnp.arange(t)
    row = (tok // GRID_W).astype(F32)
    col = (tok % GRID_W).astype(F32)
    n_freq = DIFF_QK // 4
    inv = ROPE_THETA ** (-jnp.arange(n_freq, dtype=F32) / n_freq)
    ang = jnp.concatenate([row[:, None] * inv, col[:, None] * inv], axis=-1)
    ang2 = jnp.tile(jnp.repeat(ang, 2, axis=1), (1, LANES // DIFF_QK))
    sign = jnp.where(jnp.arange(LANES) % 2 == 0, -1.0, 1.0).astype(F32)
    return jnp.cos(ang2), jnp.sin(ang2) * sign


def _head_major_cols(w):
    d = w.shape[0]
    return w.reshape(d, 2, DIFF_HEADS, DIFF_QK).transpose(0, 2, 1, 3).reshape(d, -1)


def kernel(x, c, ctx, c_ctx, w_ada, b_ada, g_norm, w_in, na_rel_bias, diff_lambda, diff_subln,
           w_out, peer_wq, peer_keys, peer_u, peer_v):
    assert w_ada.shape[0] == 1, "single-layer kernel"
    b, t, d = x.shape
    n_ctx = ctx.shape[1]
    rows = t // GRID_W
    assert t % (NA_QROWS * GRID_W) == 0 and rows >= NA_KROWS

    pad = (-(b + 1)) % 8
    cc = jnp.concatenate([c, c_ctx[None], jnp.zeros((pad, d), F32)], axis=0)
    mod = _ada(cc, w_ada[0], b_ada[0][None]).reshape(b + 1 + pad, N_MOD, d)
    lat = lambda i: mod[:b, i][:, None, :]
    cxm = lambda i: mod[b:b + 1, i][:, None, :]

    gw = NA_WIDTH
    w_in0 = w_in[0]
    w_perm = jnp.concatenate([w_in0[:, :3 * gw], _head_major_cols(w_in0[:, 3 * gw:4 * gw]),
                              _head_major_cols(w_in0[:, 4 * gw:5 * gw]), w_in0[:, 5 * gw:]], axis=1).astype(BF16)
    cos2, sin2 = _rope_tables(t)
    g0 = g_norm[0, 0][None]
    qa, ka, va, qd, kd, vd = _qkv(x, lat(0), lat(1), g0, w_perm, cos2, sin2,
                                  rope=True, per_batch_mod=True, tm=512)
    _, ka_c, va_c, _, kd_c, vd_c = _qkv(ctx, cxm(0), cxm(1), g0, w_perm, cos2[:n_ctx], sin2[:n_ctx],
                                        rope=False, per_batch_mod=False, tm=n_ctx)

    out_a = _na(qa, ka, va, ka_c, va_c, *_na_bias_table(na_rel_bias[0], rows))
    out_d = _diff(diff_lambda[0], qd, kd, vd, kd_c, vd_c, diff_subln[0][None], tq=DIFF_TQ, tk=DIFF_TK)

    keys = peer_keys[0].reshape(2 * PEER_HEADS, PEER_NKEYS, -1).astype(BF16)
    x1, h2, st = _out(out_a, out_d, x, lat(2), lat(3), lat(4), g_norm[0, 1][None], g_norm[0, 2][None],
                      w_out[0].astype(BF16), peer_wq[0].astype(BF16), keys, tm=512)

    idx4, gate2 = _topk(st, tl=256)
    idx4 = idx4.reshape(b * t, PEER_NJ)
    gate2 = gate2.reshape(b * t, 2 * PEER_NJ)

    w2 = _peer_u(idx4, h2, gate2, _pack_table(peer_u[0]))
    y3 = _peer_v(idx4, w2, _pack_table(peer_v[0]))
    return _final(x1, y3, lat(5), g_norm[0, 3][None], tm=512)
```

```python
import functools
import math

import numpy as np
import jax
import jax.numpy as jnp
from jax import lax
from jax.experimental import pallas as pl
from jax.experimental.pallas import tpu as pltpu

F32 = jnp.float32
BF16 = jnp.bfloat16
I32 = jnp.int32

GRID_W = 64
HEAD_DIM = 64
NA_HEADS = 8
NA_WIDTH = NA_HEADS * HEAD_DIM
NA_WIN_R = 8
NA_WIN_C = 16
DIFF_HEADS = 4
DIFF_QK = HEAD_DIM
DIFF_V = 2 * DIFF_QK
ROPE_THETA = 10000.0
PEER_HEADS = 8
PEER_NKEYS = 128
PEER_TOPK = 16
N_MOD = 6
EPS = 1e-6
LAMBDA_INIT = 0.8 - 0.6 * math.exp(-0.3 * 0)
LOG2E = math.log2(math.e)
DIFF_TQ = 512
DIFF_TK = 512

LANES = 128
VMEM_LIMIT = 56 * 1024 * 1024
NEG = -1e30

NA_QROWS = 8
NA_QSUBROWS = 2
NA_QSUB = NA_QSUBROWS * GRID_W
NA_KROWS = 10
PEER_TM = 128
PEER_NJ = PEER_HEADS * PEER_TOPK
TILE_STRIDE = PEER_NJ + 1
PEER_UNROLL = 8


def _dot_nt(a, b):
    return lax.dot_general(a, b, (((1,), (1,)), ((), ())), preferred_element_type=F32)


def _rms(x, g):
    return x * lax.rsqrt(jnp.mean(x * x, axis=-1, keepdims=True) + EPS) * g


def _ada_kernel(c_ref, w_ref, b_ref, o_ref):
    c = c_ref[...]
    s = c / (1.0 + jnp.exp(-c))
    o_ref[...] = jnp.dot(s, w_ref[...], precision=lax.Precision.HIGHEST,
                         preferred_element_type=F32) + b_ref[...]


def _ada(cc, w, b):
    m, d = cc.shape
    n = w.shape[1]
    tn = 1024
    return pl.pallas_call(
        _ada_kernel,
        grid=(n // tn,),
        in_specs=[pl.BlockSpec((m, d), lambda j: (0, 0)),
                  pl.BlockSpec((d, tn), lambda j: (0, j)),
                  pl.BlockSpec((1, tn), lambda j: (0, j))],
        out_specs=pl.BlockSpec((m, tn), lambda j: (0, j)),
        out_shape=jax.ShapeDtypeStruct((m, n), F32),
        compiler_params=pltpu.CompilerParams(vmem_limit_bytes=VMEM_LIMIT),
        name="ada",
    )(cc, w, b)


def _qkv_kernel(x_ref, sh_ref, sc_ref, g_ref, w_ref, cos_ref, sin_ref,
                qa_ref, ka_ref, va_ref, qd_ref, kd_ref, vd_ref, *, rope):
    x = x_ref[0]
    h = _rms(x, g_ref[...]) * (1.0 + sc_ref[0]) + sh_ref[0]
    hb = h.astype(BF16)
    gw = NA_WIDTH
    scale = HEAD_DIM ** -0.5

    def proj(g):
        return jnp.dot(hb, w_ref[:, g * gw:(g + 1) * gw], preferred_element_type=F32)

    def roped(p):
        if not rope:
            return p
        cos2 = cos_ref[...]
        sin2 = sin_ref[...]
        even = (lax.broadcasted_iota(I32, cos2.shape, 1) % 2) == 0
        outs = []
        for c in range(gw // LANES):
            v = p[:, c * LANES:(c + 1) * LANES]
            nxt = pltpu.roll(v, LANES - 1, axis=1)
            prv = pltpu.roll(v, 1, axis=1)
            outs.append(v * cos2 + jnp.where(even, nxt, prv) * sin2)
        return jnp.concatenate(outs, axis=1)

    qa_ref[0] = (proj(0) * scale).astype(BF16)
    ka_ref[0] = proj(1).astype(BF16)
    va_ref[0] = proj(2).astype(BF16)
    qd_ref[0] = (roped(proj(3)) * (scale * LOG2E)).astype(BF16)
    kd_ref[0] = roped(proj(4)).astype(BF16)
    vd = proj(5).astype(BF16)
    ones = jnp.ones((vd.shape[0], DIFF_V), BF16)
    pieces = []
    for hd in range(DIFF_HEADS):
        pieces += [vd[:, hd * DIFF_V:(hd + 1) * DIFF_V], ones]
    vd_ref[0] = jnp.concatenate(pieces, axis=1)


def _qkv(x, sh, sc, g, w, cos2, sin2, *, rope, per_batch_mod, tm):
    b, l, d = x.shape
    n = w.shape[1]
    gw = NA_WIDTH
    mod_map = (lambda i, j: (i, 0, 0)) if per_batch_mod else (lambda i, j: (0, 0, 0))
    out_sds = jax.ShapeDtypeStruct((b, l, gw), BF16)
    out_spec = pl.BlockSpec((1, tm, gw), lambda i, j: (i, j, 0))
    return pl.pallas_call(
        functools.partial(_qkv_kernel, rope=rope),
        grid=(b, l // tm),
        in_specs=[pl.BlockSpec((1, tm, d), lambda i, j: (i, j, 0)),
                  pl.BlockSpec((1, 1, d), mod_map),
                  pl.BlockSpec((1, 1, d), mod_map),
                  pl.BlockSpec((1, d), lambda i, j: (0, 0)),
                  pl.BlockSpec((d, n), lambda i, j: (0, 0)),
                  pl.BlockSpec((tm, LANES), lambda i, j: (j, 0)),
                  pl.BlockSpec((tm, LANES), lambda i, j: (j, 0))],
        out_specs=[out_spec] * 5 + [pl.BlockSpec((1, tm, 2 * gw), lambda i, j: (i, j, 0))],
        out_shape=[out_sds] * 5 + [jax.ShapeDtypeStruct((b, l, 2 * gw), BF16)],
        compiler_params=pltpu.CompilerParams(vmem_limit_bytes=VMEM_LIMIT),
        name="qkv_rope" if rope else "qkv_ctx",
    )(x, sh, sc, g, w, cos2, sin2)


def _na_window(qr0, rows):
    r_start = jnp.clip(qr0 - NA_WIN_R // 2, 0, rows - NA_WIN_R)
    return jnp.minimum(r_start, rows - NA_KROWS)


def _na_kernel(pid_ref, q_ref, k_ref, v_ref, kc_ref, vc_ref, bias_ref, o_ref, *, rows):
    rb = pl.program_id(2)
    nk = NA_KROWS * GRID_W
    kc = kc_ref[0]
    vc = vc_ref[0]
    lane = lax.broadcasted_iota(I32, (1, LANES), 1)
    nsub = NA_QROWS // NA_QSUBROWS
    for s in range(nsub):
        sl = slice(s * NA_QSUB, (s + 1) * NA_QSUB)
        start = pl.multiple_of(_na_window(rb * NA_QROWS + s * NA_QSUBROWS, rows) * GRID_W, GRID_W)
        kw = k_ref[0, pl.ds(start, nk), :]
        vw = v_ref[0, pl.ds(start, nk), :]
        pid = pid_ref[rb * nsub + s]
        q = q_ref[0, sl, :]
        zero = jnp.zeros_like(q)
        q2 = jnp.concatenate([jnp.where(lane < HEAD_DIM, q, zero), jnp.where(lane >= HEAD_DIM, q, zero)], axis=0)
        s_loc = _dot_nt(q2, kw) + jnp.concatenate([bias_ref[pid, 0], bias_ref[pid, 1]], axis=0)
        s_ctx = _dot_nt(q2, kc)
        mx = jnp.maximum(jnp.max(s_loc, axis=-1, keepdims=True), jnp.max(s_ctx, axis=-1, keepdims=True))
        p_loc = jnp.exp(s_loc - mx)
        p_ctx = jnp.exp(s_ctx - mx)
        l = jnp.sum(p_loc, axis=-1, keepdims=True) + jnp.sum(p_ctx, axis=-1, keepdims=True)
        o = (jnp.dot(p_loc.astype(BF16), vw, preferred_element_type=F32)
             + jnp.dot(p_ctx.astype(BF16), vc, preferred_element_type=F32)) / l
        o_ref[0, sl, :] = jnp.where(lane < HEAD_DIM, o[:NA_QSUB], o[NA_QSUB:]).astype(BF16)


def _na_bias_table(rel_bias, rows):
    n_ro = 2 * NA_WIN_R - 1
    qc = np.arange(GRID_W)[:, None]
    kc = np.arange(GRID_W)[None, :]
    c_start = np.clip(qc - NA_WIN_C // 2, 0, GRID_W - NA_WIN_C)
    valid_c = (kc >= c_start) & (kc < c_start + NA_WIN_C)
    co = np.clip(kc - qc + NA_WIN_C - 1, 0, 2 * NA_WIN_C - 2)
    tz = jnp.take(rel_bias, jnp.asarray(co.reshape(-1), I32), axis=2).reshape(NA_HEADS, n_ro, GRID_W, GRID_W)
    tz = jnp.where(jnp.asarray(valid_c)[None, None], tz, NEG)
    tz = jnp.concatenate([tz, jnp.full((NA_HEADS, 1, GRID_W, GRID_W), NEG, F32)], axis=1)
    patterns, pid = [], []
    for qr0 in range(0, rows, NA_QSUBROWS):
        ws = min(int(np.clip(qr0 - NA_WIN_R // 2, 0, rows - NA_WIN_R)), rows - NA_KROWS)
        qr = qr0 + np.arange(NA_QSUBROWS)[:, None]
        kr = ws + np.arange(NA_KROWS)[None, :]
        r_start = np.clip(qr - NA_WIN_R // 2, 0, rows - NA_WIN_R)
        valid_r = (kr >= r_start) & (kr < r_start + NA_WIN_R)
        assert valid_r.sum(axis=1).min() == NA_WIN_R
        sel = tuple(np.where(valid_r, kr - qr + NA_WIN_R - 1, n_ro).reshape(-1))
        if sel not in patterns:
            patterns.append(sel)
        pid.append(patterns.index(sel))
    sel = np.asarray(patterns, np.int32).reshape(-1)
    blocks = jnp.take(tz, jnp.asarray(sel), axis=1)
    blocks = blocks.reshape(NA_HEADS, len(patterns), NA_QSUBROWS, NA_KROWS, GRID_W, GRID_W)
    table = blocks.transpose(1, 0, 2, 4, 3, 5).reshape(len(patterns), NA_HEADS, NA_QSUB, NA_KROWS * GRID_W)
    return table, np.asarray(pid, np.int32)


def _na(qa, ka, va, ka_c, va_c, bias, pid):
    b, t, _ = qa.shape
    c = ka_c.shape[1]
    rows = t // GRID_W
    nrb = rows // NA_QROWS
    tq = NA_QROWS * GRID_W
    nk = NA_KROWS * GRID_W
    npat = bias.shape[0]
    return pl.pallas_call(
        functools.partial(_na_kernel, rows=rows),
        grid=(b, NA_HEADS // 2, nrb),
        in_specs=[pl.BlockSpec(memory_space=pltpu.SMEM),
                  pl.BlockSpec((1, tq, LANES), lambda i, hp, rb: (i, rb, hp)),
                  pl.BlockSpec((1, t, LANES), lambda i, hp, rb: (i, 0, hp)),
                  pl.BlockSpec((1, t, LANES), lambda i, hp, rb: (i, 0, hp)),
                  pl.BlockSpec((1, c, LANES), lambda i, hp, rb: (i, 0, hp)),
                  pl.BlockSpec((1, c, LANES), lambda i, hp, rb: (i, 0, hp)),
                  pl.BlockSpec((npat, 2, NA_QSUB, nk), lambda i, hp, rb: (0, hp, 0, 0))],
        out_specs=pl.BlockSpec((1, tq, LANES), lambda i, hp, rb: (i, rb, hp)),
        out_shape=jax.ShapeDtypeStruct((b, t, NA_WIDTH), BF16),
        compiler_params=pltpu.CompilerParams(vmem_limit_bytes=VMEM_LIMIT),
        name="na_attn",
    )(jnp.asarray(pid), qa, ka, va, ka_c, va_c, bias)


def _diff_kernel(lam_ref, q_ref, k_ref, v_ref, kc_ref, vc_ref, g_ref, o_ref, *, tk):
    q = q_ref[0]
    tq = q.shape[0]
    t = k_ref.shape[1]
    lane = lax.broadcasted_iota(I32, (1, LANES), 1)
    zero = jnp.zeros_like(q)
    q2 = jnp.concatenate([jnp.where(lane < DIFF_QK, q, zero), jnp.where(lane >= DIFF_QK, q, zero)], axis=0)

    def update(kb, vb, carry):
        m, a = carry
        s = _dot_nt(q2, kb)
        mn = jnp.maximum(m, jnp.max(s, axis=-1, keepdims=True))
        p = jnp.exp2(s - mn)
        return mn, jnp.exp2(m - mn) * a + jnp.dot(p.astype(BF16), vb, preferred_element_type=F32)

    init = (jnp.full((2 * tq, 1), NEG, F32), jnp.zeros((2 * tq, 2 * DIFF_V), F32))
    carry = update(kc_ref[0], vc_ref[0], init)
    for i in range(t // tk):
        carry = update(k_ref[0, i * tk:(i + 1) * tk, :], v_ref[0, i * tk:(i + 1) * tk, :], carry)
    lp = lam_ref[...]
    lam = (jnp.exp(jnp.sum(lp[0:1] * lp[1:2], axis=-1, keepdims=True))
           - jnp.exp(jnp.sum(lp[2:3] * lp[3:4], axis=-1, keepdims=True)) + LAMBDA_INIT)
    a = carry[1]
    a = a[:, :DIFF_V] / a[:, DIFF_V:DIFF_V + 1]
    out = a[:tq] - lam * a[tq:]
    o_ref[0] = (_rms(out, g_ref[...]) * (1.0 - LAMBDA_INIT)).astype(BF16)


def _diff(lam_params, qd, kd, vd, kd_c, vd_c, subln, *, tq, tk):
    b, t, w = qd.shape
    c = kd_c.shape[1]
    return pl.pallas_call(
        functools.partial(_diff_kernel, tk=tk),
        grid=(b, DIFF_HEADS, t // tq),
        in_specs=[pl.BlockSpec((4, DIFF_QK), lambda i, h, j: (0, 0)),
                  pl.BlockSpec((1, tq, LANES), lambda i, h, j: (i, j, h)),
                  pl.BlockSpec((1, t, LANES), lambda i, h, j: (i, 0, h)),
                  pl.BlockSpec((1, t, 2 * DIFF_V), lambda i, h, j: (i, 0, h)),
                  pl.BlockSpec((1, c, LANES), lambda i, h, j: (i, 0, h)),
                  pl.BlockSpec((1, c, 2 * DIFF_V), lambda i, h, j: (i, 0, h)),
                  pl.BlockSpec((1, DIFF_V), lambda i, h, j: (0, 0))],
        out_specs=pl.BlockSpec((1, tq, LANES), lambda i, h, j: (i, j, h)),
        out_shape=jax.ShapeDtypeStruct((b, t, w), BF16),
        compiler_params=pltpu.CompilerParams(vmem_limit_bytes=VMEM_LIMIT),
        name="diff_attn",
    )(lam_params, qd, kd, vd, kd_c, vd_c, subln)


def _out_kernel(oa_ref, od_ref, x_ref, gt1_ref, sh2_ref, sc2_ref, g1_ref, g2_ref,
                wout_ref, wq_ref, keys_ref, x1_ref, h2_ref, st_ref):
    wa = NA_WIDTH
    y = (jnp.dot(oa_ref[0], wout_ref[0:wa, :], preferred_element_type=F32)
         + jnp.dot(od_ref[0], wout_ref[wa:, :], preferred_element_type=F32))
    x1 = x_ref[0] + gt1_ref[0] * _rms(y, g1_ref[...])
    x1_ref[0] = x1
    h2 = _rms(x1, g2_ref[...]) * (1.0 + sc2_ref[0]) + sh2_ref[0]
    h2_ref[...] = h2.reshape(h2_ref.shape)
    qb = jnp.dot(h2.astype(BF16), wq_ref[...], preferred_element_type=F32).astype(BF16)
    for hp in range(2 * PEER_HEADS):
        st_ref[0, hp] = _dot_nt(keys_ref[hp], qb[:, hp * LANES:(hp + 1) * LANES])


def _out(oa, od, x, gt1, sh2, sc2, g1, g2, wout, wq, keys, *, tm):
    b, t, d = x.shape
    nq = wq.shape[1]
    nhp = keys.shape[0]
    mod = pl.BlockSpec((1, 1, d), lambda i, j: (i, 0, 0))
    gsp = pl.BlockSpec((1, d), lambda i, j: (0, 0))
    tok = pl.BlockSpec((1, tm, d), lambda i, j: (i, j, 0))
    half = pl.BlockSpec((1, tm, NA_WIDTH), lambda i, j: (i, j, 0))
    return pl.pallas_call(
        _out_kernel,
        grid=(b, t // tm),
        in_specs=[half, half, tok, mod, mod, mod, gsp, gsp,
                  pl.BlockSpec((d, d), lambda i, j: (0, 0)),
                  pl.BlockSpec((d, nq), lambda i, j: (0, 0)),
                  pl.BlockSpec((nhp, PEER_NKEYS, LANES), lambda i, j: (0, 0, 0))],
        out_specs=[tok, pl.BlockSpec((tm, d // LANES, LANES), lambda i, j: (i * (t // tm) + j, 0, 0)),
                   pl.BlockSpec((1, nhp, PEER_NKEYS, tm), lambda i, j: (i, 0, 0, j))],
        out_shape=[jax.ShapeDtypeStruct((b, t, d), F32), jax.ShapeDtypeStruct((b * t, d // LANES, LANES), F32),
                   jax.ShapeDtypeStruct((b, nhp, PEER_NKEYS, t), F32)],
        compiler_params=pltpu.CompilerParams(vmem_limit_bytes=VMEM_LIMIT),
        name="out_proj_peer_scores",
    )(oa, od, x, gt1, sh2, sc2, g1, g2, wout, wq, keys)


def _top1(s, iota):
    m = jnp.max(s, axis=0, keepdims=True)
    idx = jnp.min(jnp.where(s == m, iota, s.shape[0]), axis=0, keepdims=True)
    return m, idx, jnp.where(iota == idx, -jnp.inf, s)


def _topk_kernel(st_ref, idx_ref, gate_ref, stop_ref, itop_ref, best_ref, expert_ref):
    tl = st_ref.shape[-1]
    k = PEER_TOPK
    nlist = 2 * PEER_HEADS
    iota_n = lax.broadcasted_iota(I32, (PEER_NKEYS, tl), 0)

    row8 = lax.broadcasted_iota(I32, (8, PEER_NKEYS), 0)
    col8 = lax.broadcasted_iota(I32, (8, PEER_NKEYS), 1)
    probe = jnp.where(row8 == 0, col8, jnp.where(row8 == 1, 1, 0)).astype(BF16)
    unique = jnp.ones((1, tl), F32)
    for li in range(nlist):
        s = st_ref[0, li]
        vals, sums = [], []
        for _ in range(k):
            m = jnp.max(s, axis=0, keepdims=True)
            eq = s == m
            sums.append(jnp.dot(probe, jnp.where(eq, 1.0, 0.0).astype(BF16), preferred_element_type=F32)[0:2])
            s = jnp.where(eq, -jnp.inf, s)
            vals.append(m)
        stop_ref[li] = jnp.concatenate(vals, axis=0)
        itop_ref[li] = jnp.concatenate([x[0:1] for x in sums], axis=0).astype(I32)
        cnt = jnp.concatenate([x[1:2] for x in sums], axis=0)
        unique = jnp.minimum(unique, jnp.min(jnp.where(cnt == 1.0, 1.0, 0.0), axis=0, keepdims=True))

    @pl.when(jnp.min(unique) < 1.0)
    def _():
        for li in range(nlist):
            s = st_ref[0, li]
            vals, idxs = [], []
            for _ in range(k):
                m, i, s = _top1(s, iota_n)
                vals.append(m)
                idxs.append(i)
            stop_ref[li] = jnp.concatenate(vals, axis=0)
            itop_ref[li] = jnp.concatenate(idxs, axis=0)

    widths = [k // (a + 1) for a in range(k)]
    n_cand = -(-sum(widths) // 8) * 8
    iota_c = lax.broadcasted_iota(I32, (n_cand, tl), 0)
    pad = n_cand - sum(widths)

    def candidates(h):
        s0, i0 = stop_ref[2 * h], itop_ref[2 * h]
        s1, i1 = stop_ref[2 * h + 1], itop_ref[2 * h + 1]
        cand = jnp.concatenate([s0[a:a + 1] + s1[0:widths[a]] for a in range(k)]
                               + [jnp.full((pad, tl), -jnp.inf, F32)], axis=0)
        cidx = jnp.concatenate([i0[a:a + 1] * PEER_NKEYS + i1[0:widths[a]] for a in range(k)]
                               + [jnp.zeros((pad, tl), I32)], axis=0)
        return cand, cidx

    unique = jnp.ones((1, tl), F32)
    for h in range(PEER_HEADS):
        cand, cidx = candidates(h)
        best, experts = [], []
        for _ in range(k):
            m = jnp.max(cand, axis=0, keepdims=True)
            eq = cand == m
            experts.append(jnp.sum(jnp.where(eq, cidx, 0), axis=0, keepdims=True))
            cand = jnp.where(eq, -jnp.inf, cand)
            best.append(m)
        removed = jnp.sum(jnp.where(cand == -jnp.inf, 1.0, 0.0), axis=0, keepdims=True)
        unique = jnp.minimum(unique, jnp.where(removed == float(k + pad), 1.0, 0.0))
        best_ref[h] = jnp.concatenate(best, axis=0)
        expert_ref[h] = jnp.concatenate(experts, axis=0)

    @pl.when(jnp.min(unique) < 1.0)
    def _():
        for h in range(PEER_HEADS):
            cand, cidx = candidates(h)
            best, experts = [], []
            for _ in range(k):
                m, pos, cand = _top1(cand, iota_c)
                best.append(m)
                experts.append(jnp.sum(jnp.where(iota_c == pos, cidx, 0), axis=0, keepdims=True))
            best_ref[h] = jnp.concatenate(best, axis=0)
            expert_ref[h] = jnp.concatenate(experts, axis=0)

    gates = []
    for h in range(PEER_HEADS):
        best = best_ref[h]
        e = jnp.exp(best - best[0:1])
        gates.append(e / jnp.sum(e, axis=0, keepdims=True))
    nj = PEER_NJ
    idx_ref[0] = (expert_ref[...].reshape(nj, tl) * 4).T
    dup = (lax.broadcasted_iota(I32, (nj, 2 * nj), 1) // 2 == lax.broadcasted_iota(I32, (nj, 2 * nj), 0))
    gate_ref[0] = jnp.dot(jnp.concatenate(gates, axis=0).T, dup.astype(F32),
                          precision=lax.Precision.HIGHEST, preferred_element_type=F32)


def _topk(st, *, tl):
    b, nhp, nkeys, t = st.shape
    nj = PEER_NJ
    return pl.pallas_call(
        _topk_kernel,
        grid=(b, t // tl),
        in_specs=[pl.BlockSpec((1, nhp, nkeys, tl), lambda i, j: (i, 0, 0, j))],
        out_specs=[pl.BlockSpec((1, tl, nj), lambda i, j: (i, j, 0)),
                   pl.BlockSpec((1, tl, 2 * nj), lambda i, j: (i, j, 0))],
        out_shape=[jax.ShapeDtypeStruct((b, t, nj), I32), jax.ShapeDtypeStruct((b, t, 2 * nj), F32)],
        scratch_shapes=[pltpu.VMEM((nhp, PEER_TOPK, tl), F32), pltpu.VMEM((nhp, PEER_TOPK, tl), I32),
                        pltpu.VMEM((PEER_HEADS, PEER_TOPK, tl), F32), pltpu.VMEM((PEER_HEADS, PEER_TOPK, tl), I32)],
        compiler_params=pltpu.CompilerParams(vmem_limit_bytes=VMEM_LIMIT),
        name="peer_topk",
    )(st)


def _pack_table(tab):
    e, d = tab.shape
    tb = tab.astype(BF16)
    lo = lax.bitcast_convert_type(tb[:, :d // 2], jnp.uint16).astype(jnp.uint32)
    hi = lax.bitcast_convert_type(tb[:, d // 2:], jnp.uint16).astype(jnp.uint32)
    words = lax.bitcast_convert_type((hi << 16) | lo, I32)
    return words.reshape(e * (d // 2 // LANES), LANES)


def _gather_rows(idx_ref, trow, table_ref, tile_ref):
    half = PEER_NJ // 2
    ahead = 2
    rows = {}
    for j in range(ahead):
        rows[j], rows[j + half] = idx_ref[trow, j], idx_ref[trow, j + half]
    for j in range(half):
        if j + ahead < half:
            rows[j + ahead], rows[j + ahead + half] = idx_ref[trow, j + ahead], idx_ref[trow, j + ahead + half]
        a = table_ref[pl.ds(pl.multiple_of(rows[j], 4), 4), :]
        b = table_ref[pl.ds(pl.multiple_of(rows[j + half], 4), 4), :]
        tile_ref[pl.ds(j, 8, stride=TILE_STRIDE), :] = jnp.concatenate([a, b], axis=0)


def _tile_chunk(tile_ref, c):
    half = PEER_NJ // 2
    lo, hi = c * TILE_STRIDE, (c + 4) * TILE_STRIDE
    return pltpu.bitcast(jnp.concatenate([tile_ref[lo:lo + half, :], tile_ref[hi:hi + half, :]], axis=0), BF16)


def _split_bf16(v):
    head = v.astype(BF16).astype(F32)
    return jnp.concatenate([head, v - head], axis=0)


def _idx_copy(idx_hbm, first_token, buf, sem):
    n = buf.shape[0]
    src = idx_hbm.at[pl.ds(pl.multiple_of(first_token, n), n), :]
    return pltpu.make_async_copy(src, buf, sem)


def _for_each_group(idx_hbm, ibufs, sems, group):
    step = pl.program_id(0)
    half = PEER_TM // 2
    nu = PEER_UNROLL

    @pl.when(step == 0)
    def _():
        for h in range(2):
            _idx_copy(idx_hbm, h * half, ibufs[h], sems.at[h]).start()

    for h in range(2):
        tok0 = step * PEER_TM + h * half
        _idx_copy(idx_hbm, tok0, ibufs[h], sems.at[h]).wait()
        for g in range(half // nu):
            group(ibufs[h], g * nu, h * half + g * nu)

        @pl.when(step + 1 < pl.num_programs(0))
        def _():
            _idx_copy(idx_hbm, tok0 + PEER_TM, ibufs[h], sems.at[h]).start()


def _peer_scratch():
    return ([pltpu.VMEM((8 * TILE_STRIDE + 8, LANES), I32) for _ in range(PEER_UNROLL)]
            + [pltpu.SMEM((PEER_TM // 2, PEER_NJ), I32) for _ in range(2)]
            + [pltpu.SemaphoreType.DMA((2,))])


def _peer_u_kernel(idx_hbm, x_ref, gate_ref, u_ref, w_ref, *scratch):
    nj = PEER_NJ
    nu = PEER_UNROLL
    tile_refs, ibufs, sems = scratch[:nu], scratch[nu:nu + 2], scratch[nu + 2]
    row16 = lax.broadcasted_iota(I32, (16, 1), 0) % 8
    even = (lax.broadcasted_iota(I32, (1, 2 * nj), 1) % 2) == 0

    def token(ibuf, off, t, tile_ref):
        _gather_rows(ibuf, off, u_ref, tile_ref)
        x16 = _split_bf16(x_ref[t])
        acc = jnp.zeros((16, 2 * nj), F32)
        for c in range(4):
            lhs = jnp.where((row16 == c) | (row16 == 4 + c), x16, 0.0).astype(BF16)
            acc = acc + _dot_nt(lhs, _tile_chunk(tile_ref, c))
        a = acc[0:8] + acc[8:16]
        return jnp.sum(jnp.where(even, a, pltpu.roll(a, 4, axis=0))[0:4], axis=0, keepdims=True)

    def group(ibuf, off, t0):
        z = jnp.concatenate([token(ibuf, off + i, t0 + i, tile_refs[i]) for i in range(nu)], axis=0)
        act = z + jnp.where(even, pltpu.roll(z, 2 * nj - 1, axis=1), pltpu.roll(z, 1, axis=1))
        gelu = 0.5 * act * (1.0 + lax.erf(act * (2.0 ** -0.5)))
        w_ref[t0:t0 + nu, :] = gate_ref[t0:t0 + nu, :] * gelu

    _for_each_group(idx_hbm, ibufs, sems, group)


def _peer_u(idx4, x3, gate2, table):
    n = x3.shape[0]
    nj = PEER_NJ
    tm = PEER_TM
    return pl.pallas_call(
        _peer_u_kernel,
        grid=(n // tm,),
        in_specs=[pl.BlockSpec(memory_space=pl.ANY),
                  pl.BlockSpec((tm, 8, LANES), lambda i: (i, 0, 0)),
                  pl.BlockSpec((tm, 2 * nj), lambda i: (i, 0)),
                  pl.BlockSpec(memory_space=pltpu.VMEM)],
        out_specs=pl.BlockSpec((tm, 2 * nj), lambda i: (i, 0)),
        out_shape=jax.ShapeDtypeStruct((n, 2 * nj), F32),
        scratch_shapes=_peer_scratch(),
        compiler_params=pltpu.CompilerParams(vmem_limit_bytes=VMEM_LIMIT, dimension_semantics=("arbitrary",)),
        name="peer_u",
    )(idx4, x3, gate2, table)


def _peer_v_kernel(idx_hbm, w_ref, v_ref, o_ref, *scratch):
    nj = PEER_NJ
    nu = PEER_UNROLL
    tile_refs, ibufs, sems = scratch[:nu], scratch[nu:nu + 2], scratch[nu + 2]
    row16 = lax.broadcasted_iota(I32, (16, 1), 0) % 8
    even = (lax.broadcasted_iota(I32, (1, 2 * nj), 1) % 2) == 0

    def token(ibuf, off, t, tile_ref):
        _gather_rows(ibuf, off, v_ref, tile_ref)
        w16 = _split_bf16(jnp.broadcast_to(w_ref[t:t + 1, :], (8, 2 * nj)))
        acc = jnp.zeros((16, LANES), F32)
        for c in range(4):
            keep = ((row16 == c) & even) | ((row16 == 4 + c) & jnp.logical_not(even))
            lhs = jnp.where(keep, w16, 0.0).astype(BF16)
            acc = acc + jnp.dot(lhs, _tile_chunk(tile_ref, c), preferred_element_type=F32)
        o_ref[t] = acc[0:8] + acc[8:16]

    def group(ibuf, off, t0):
        for i in range(nu):
            token(ibuf, off + i, t0 + i, tile_refs[i])

    _for_each_group(idx_hbm, ibufs, sems, group)


def _peer_v(idx4, w2, table):
    n = w2.shape[0]
    nj = PEER_NJ
    tm = PEER_TM
    return pl.pallas_call(
        _peer_v_kernel,
        grid=(n // tm,),
        in_specs=[pl.BlockSpec(memory_space=pl.ANY),
                  pl.BlockSpec((tm, 2 * nj), lambda i: (i, 0)),
                  pl.BlockSpec(memory_space=pltpu.VMEM)],
        out_specs=pl.BlockSpec((tm, 8, LANES), lambda i: (i, 0, 0)),
        out_shape=jax.ShapeDtypeStruct((n, 8, LANES), F32),
        scratch_shapes=_peer_scratch(),
        compiler_params=pltpu.CompilerParams(vmem_limit_bytes=VMEM_LIMIT, dimension_semantics=("arbitrary",)),
        name="peer_v",
    )(idx4, w2, table)


def _final_kernel(x1_ref, y_ref, gt2_ref, g_ref, o_ref):
    y = y_ref[...].reshape(x1_ref.shape[1:])
    o_ref[0] = x1_ref[0] + gt2_ref[0] * _rms(y, g_ref[...])


def _final(x1, y3, gt2, g, *, tm):
    b, t, d = x1.shape
    nt = t // tm
    tok = pl.BlockSpec((1, tm, d), lambda i, j: (i, j, 0))
    return pl.pallas_call(
        _final_kernel,
        grid=(b, nt),
        in_specs=[tok, pl.BlockSpec((tm, d // LANES, LANES), lambda i, j: (i * nt + j, 0, 0)),
                  pl.BlockSpec((1, 1, d), lambda i, j: (i, 0, 0)),
                  pl.BlockSpec((1, d), lambda i, j: (0, 0))],
        out_specs=tok,
        out_shape=jax.ShapeDtypeStruct((b, t, d), F32),
        compiler_params=pltpu.CompilerParams(vmem_limit_bytes=VMEM_LIMIT),
        name="final_residual",
    )(x1, y3, gt2, g)


def _rope_tables(t):
    tok = jnp.arange(t)
    row = (tok // GRID_W).astype(F32)
    col = (tok % GRID_W).astype(F32)
    n_freq = DIFF_QK // 4
    inv = ROPE_THETA ** (-jnp.arange(n_freq, dtype=F32) / n_freq)
    ang = jnp.concatenate([row[:, None] * inv, col[:, None] * inv], axis=-1)
    ang2 = jnp.tile(jnp.repeat(ang, 2, axis=1), (1, LANES // DIFF_QK))
    sign = jnp.where(jnp.arange(LANES) % 2 == 0, -1.0, 1.0).astype(F32)
    return jnp.cos(ang2), jnp.sin(ang2) * sign


def _head_major_cols(w):
    d = w.shape[0]
    return w.reshape(d, 2, DIFF_HEADS, DIFF_QK).transpose(0, 2, 1, 3).reshape(d, -1)


def kernel(x, c, ctx, c_ctx, w_ada, b_ada, g_norm, w_in, na_rel_bias, diff_lambda, diff_subln,
           w_out, peer_wq, peer_keys, peer_u, peer_v):
    assert w_ada.shape[0] == 1, "single-layer kernel"
    b, t, d = x.shape
    n_ctx = ctx.shape[1]
    rows = t // GRID_W
    assert t % (NA_QROWS * GRID_W) == 0 and rows >= NA_KROWS

    pad = (-(b + 1)) % 8
    cc = jnp.concatenate([c, c_ctx[None], jnp.zeros((pad, d), F32)], axis=0)
    mod = _ada(cc, w_ada[0], b_ada[0][None]).reshape(b + 1 + pad, N_MOD, d)
    lat = lambda i: mod[:b, i][:, None, :]
    cxm = lambda i: mod[b:b + 1, i][:, None, :]

    gw = NA_WIDTH
    w_in0 = w_in[0]
    w_perm = jnp.concatenate([w_in0[:, :3 * gw], _head_major_cols(w_in0[:, 3 * gw:4 * gw]),
                              _head_major_cols(w_in0[:, 4 * gw:5 * gw]), w_in0[:, 5 * gw:]], axis=1).astype(BF16)
    cos2, sin2 = _rope_tables(t)
    g0 = g_norm[0, 0][None]
    qa, ka, va, qd, kd, vd = _qkv(x, lat(0), lat(1), g0, w_perm, cos2, sin2,
                                  rope=True, per_batch_mod=True, tm=512)
    _, ka_c, va_c, _, kd_c, vd_c = _qkv(ctx, cxm(0), cxm(1), g0, w_perm, cos2[:n_ctx], sin2[:n_ctx],
                                        rope=False, per_batch_mod=False, tm=n_ctx)

    out_a = _na(qa, ka, va, ka_c, va_c, *_na_bias_table(na_rel_bias[0], rows))
    out_d = _diff(diff_lambda[0], qd, kd, vd, kd_c, vd_c, diff_subln[0][None], tq=DIFF_TQ, tk=DIFF_TK)

    keys = peer_keys[0].reshape(2 * PEER_HEADS, PEER_NKEYS, -1).astype(BF16)
    x1, h2, st = _out(out_a, out_d, x, lat(2), lat(3), lat(4), g_norm[0, 1][None], g_norm[0, 2][None],
                      w_out[0].astype(BF16), peer_wq[0].astype(BF16), keys, tm=512)

    idx4, gate2 = _topk(st, tl=256)
    idx4 = idx4.reshape(b * t, PEER_NJ)
    gate2 = gate2.reshape(b * t, 2 * PEER_NJ)

    w2 = _peer_u(idx4, h2, gate2, _pack_table(peer_u[0]))
    y3 = _peer_v(idx4, w2, _pack_table(peer_v[0]))
    return _final(x1, y3, lat(5), g_norm[0, 3][None], tm=512)
```

```python
import functools
import math

import numpy as np
import jax
import jax.numpy as jnp
from jax import lax
from jax.experimental import pallas as pl
from jax.experimental.pallas import tpu as pltpu

F32 = jnp.float32
BF16 = jnp.bfloat16
I32 = jnp.int32

GRID_W = 64
HEAD_DIM = 64
NA_HEADS = 8
NA_WIDTH = NA_HEADS * HEAD_DIM
NA_WIN_R = 8
NA_WIN_C = 16
DIFF_HEADS = 4
DIFF_QK = HEAD_DIM
DIFF_V = 2 * DIFF_QK
ROPE_THETA = 10000.0
PEER_HEADS = 8
PEER_NKEYS = 128
PEER_TOPK = 16
N_MOD = 6
EPS = 1e-6
LAMBDA_INIT = 0.8 - 0.6 * math.exp(-0.3 * 0)
LOG2E = math.log2(math.e)
DIFF_TQ = 512
DIFF_TK = 512

LANES = 128
VMEM_LIMIT = 56 * 1024 * 1024
NEG = -1e30

NA_QROWS = 8
NA_QSUBROWS = 2
NA_QSUB = NA_QSUBROWS * GRID_W
NA_KROWS = 10
PEER_TM = 128
PEER_NJ = PEER_HEADS * PEER_TOPK
TILE_STRIDE = PEER_NJ + 1
PEER_UNROLL = 8


def _dot_nt(a, b):
    return lax.dot_general(a, b, (((1,), (1,)), ((), ())), preferred_element_type=F32)


def _rms(x, g):
    return x * lax.rsqrt(jnp.mean(x * x, axis=-1, keepdims=True) + EPS) * g


def _ada_kernel(c_ref, w_ref, b_ref, o_ref):
    c = c_ref[...]
    s = c / (1.0 + jnp.exp(-c))
    o_ref[...] = jnp.dot(s, w_ref[...], precision=lax.Precision.HIGHEST,
                         preferred_element_type=F32) + b_ref[...]


def _ada(cc, w, b):
    m, d = cc.shape
    n = w.shape[1]
    tn = 1024
    return pl.pallas_call(
        _ada_kernel,
        grid=(n // tn,),
        in_specs=[pl.BlockSpec((m, d), lambda j: (0, 0)),
                  pl.BlockSpec((d, tn), lambda j: (0, j)),
                  pl.BlockSpec((1, tn), lambda j: (0, j))],
        out_specs=pl.BlockSpec((m, tn), lambda j: (0, j)),
        out_shape=jax.ShapeDtypeStruct((m, n), F32),
        compiler_params=pltpu.CompilerParams(vmem_limit_bytes=VMEM_LIMIT),
        name="ada",
    )(cc, w, b)


def _qkv_kernel(x_ref, sh_ref, sc_ref, g_ref, w_ref, cos_ref, sin_ref,
                qa_ref, ka_ref, va_ref, qd_ref, kd_ref, vd_ref, *, rope):
    x = x_ref[0]
    h = _rms(x, g_ref[...]) * (1.0 + sc_ref[0]) + sh_ref[0]
    hb = h.astype(BF16)
    gw = NA_WIDTH
    scale = HEAD_DIM ** -0.5

    def proj(g):
        return jnp.dot(hb, w_ref[:, g * gw:(g + 1) * gw], preferred_element_type=F32)

    def roped(p):
        if not rope:
            return p
        cos2 = cos_ref[...]
        sin2 = sin_ref[...]
        even = (lax.broadcasted_iota(I32, cos2.shape, 1) % 2) == 0
        outs = []
        for c in range(gw // LANES):
            v = p[:, c * LANES:(c + 1) * LANES]
            nxt = pltpu.roll(v, LANES - 1, axis=1)
            prv = pltpu.roll(v, 1, axis=1)
            outs.append(v * cos2 + jnp.where(even, nxt, prv) * sin2)
        return jnp.concatenate(outs, axis=1)

    qa_ref[0] = (proj(0) * scale).astype(BF16)
    ka_ref[0] = proj(1).astype(BF16)
    va_ref[0] = proj(2).astype(BF16)
    qd_ref[0] = (roped(proj(3)) * (scale * LOG2E)).astype(BF16)
    kd_ref[0] = roped(proj(4)).astype(BF16)
    vd = proj(5).astype(BF16)
    ones = jnp.ones((vd.shape[0], DIFF_V), BF16)
    pieces = []
    for hd in range(DIFF_HEADS):
        pieces += [vd[:, hd * DIFF_V:(hd + 1) * DIFF_V], ones]
    vd_ref[0] = jnp.concatenate(pieces, axis=1)


def _qkv(x, sh, sc, g, w, cos2, sin2, *, rope, per_batch_mod, tm):
    b, l, d = x.shape
    n = w.shape[1]
    gw = NA_WIDTH
    mod_map = (lambda i, j: (i, 0, 0)) if per_batch_mod else (lambda i, j: (0, 0, 0))
    out_sds = jax.ShapeDtypeStruct((b, l, gw), BF16)
    out_spec = pl.BlockSpec((1, tm, gw), lambda i, j: (i, j, 0))
    return pl.pallas_call(
        functools.partial(_qkv_kernel, rope=rope),
        grid=(b, l // tm),
        in_specs=[pl.BlockSpec((1, tm, d), lambda i, j: (i, j, 0)),
                  pl.BlockSpec((1, 1, d), mod_map),
                  pl.BlockSpec((1, 1, d), mod_map),
                  pl.BlockSpec((1, d), lambda i, j: (0, 0)),
                  pl.BlockSpec((d, n), lambda i, j: (0, 0)),
                  pl.BlockSpec((tm, LANES), lambda i, j: (j, 0)),
                  pl.BlockSpec((tm, LANES), lambda i, j: (j, 0))],
        out_specs=[out_spec] * 5 + [pl.BlockSpec((1, tm, 2 * gw), lambda i, j: (i, j, 0))],
        out_shape=[out_sds] * 5 + [jax.ShapeDtypeStruct((b, l, 2 * gw), BF16)],
        compiler_params=pltpu.CompilerParams(vmem_limit_bytes=VMEM_LIMIT),
        name="qkv_rope" if rope else "qkv_ctx",
    )(x, sh, sc, g, w, cos2, sin2)


def _na_window(qr0, rows):
    r_start = jnp.clip(qr0 - NA_WIN_R // 2, 0, rows - NA_WIN_R)
    return jnp.minimum(r_start, rows - NA_KROWS)


def _na_kernel(pid_ref, q_ref, k_ref, v_ref, kc_ref, vc_ref, bias_ref, o_ref, *, rows):
    rb = pl.program_id(2)
    nk = NA_KROWS * GRID_W
    kc = kc_ref[0]
    vc = vc_ref[0]
    lane = lax.broadcasted_iota(I32, (1, LANES), 1)
    nsub = NA_QROWS // NA_QSUBROWS
    for s in range(nsub):
        sl = slice(s * NA_QSUB, (s + 1) * NA_QSUB)
        start = pl.multiple_of(_na_window(rb * NA_QROWS + s * NA_QSUBROWS, rows) * GRID_W, GRID_W)
        kw = k_ref[0, pl.ds(start, nk), :]
        vw = v_ref[0, pl.ds(start, nk), :]
        pid = pid_ref[rb * nsub + s]
        q = q_ref[0, sl, :]
        zero = jnp.zeros_like(q)
        q2 = jnp.concatenate([jnp.where(lane < HEAD_DIM, q, zero), jnp.where(lane >= HEAD_DIM, q, zero)], axis=0)
        s_loc = _dot_nt(q2, kw) + jnp.concatenate([bias_ref[pid, 0], bias_ref[pid, 1]], axis=0)
        s_ctx = _dot_nt(q2, kc)
        mx = jnp.maximum(jnp.max(s_loc, axis=-1, keepdims=True), jnp.max(s_ctx, axis=-1, keepdims=True))
        p_loc = jnp.exp(s_loc - mx)
        p_ctx = jnp.exp(s_ctx - mx)
        l = jnp.sum(p_loc, axis=-1, keepdims=True) + jnp.sum(p_ctx, axis=-1, keepdims=True)
        o = (jnp.dot(p_loc.astype(BF16), vw, preferred_element_type=F32)
             + jnp.dot(p_ctx.astype(BF16), vc, preferred_element_type=F32)) / l
        o_ref[0, sl, :] = jnp.where(lane < HEAD_DIM, o[:NA_QSUB], o[NA_QSUB:]).astype(BF16)


def _na_bias_table(rel_bias, rows):
    n_ro = 2 * NA_WIN_R - 1
    qc = np.arange(GRID_W)[:, None]
    kc = np.arange(GRID_W)[None, :]
    c_start = np.clip(qc - NA_WIN_C // 2, 0, GRID_W - NA_WIN_C)
    valid_c = (kc >= c_start) & (kc < c_start + NA_WIN_C)
    co = np.clip(kc - qc + NA_WIN_C - 1, 0, 2 * NA_WIN_C - 2)
    tz = jnp.take(rel_bias, jnp.asarray(co.reshape(-1), I32), axis=2).reshape(NA_HEADS, n_ro, GRID_W, GRID_W)
    tz = jnp.where(jnp.asarray(valid_c)[None, None], tz, NEG)
    tz = jnp.concatenate([tz, jnp.full((NA_HEADS, 1, GRID_W, GRID_W), NEG, F32)], axis=1)
    patterns, pid = [], []
    for qr0 in range(0, rows, NA_QSUBROWS):
        ws = min(int(np.clip(qr0 - NA_WIN_R // 2, 0, rows - NA_WIN_R)), rows - NA_KROWS)
        qr = qr0 + np.arange(NA_QSUBROWS)[:, None]
        kr = ws + np.arange(NA_KROWS)[None, :]
        r_start = np.clip(qr - NA_WIN_R // 2, 0, rows - NA_WIN_R)
        valid_r = (kr >= r_start) & (kr < r_start + NA_WIN_R)
        assert valid_r.sum(axis=1).min() == NA_WIN_R
        sel = tuple(np.where(valid_r, kr - qr + NA_WIN_R - 1, n_ro).reshape(-1))
        if sel not in patterns:
            patterns.append(sel)
        pid.append(patterns.index(sel))
    sel = np.asarray(patterns, np.int32).reshape(-1)
    blocks = jnp.take(tz, jnp.asarray(sel), axis=1)
    blocks = blocks.reshape(NA_HEADS, len(patterns), NA_QSUBROWS, NA_KROWS, GRID_W, GRID_W)
    table = blocks.transpose(1, 0, 2, 4, 3, 5).reshape(len(patterns), NA_HEADS, NA_QSUB, NA_KROWS * GRID_W)
    return table, np.asarray(pid, np.int32)


def _na(qa, ka, va, ka_c, va_c, bias, pid):
    b, t, _ = qa.shape
    c = ka_c.shape[1]
    rows = t // GRID_W
    nrb = rows // NA_QROWS
    tq = NA_QROWS * GRID_W
    nk = NA_KROWS * GRID_W
    npat = bias.shape[0]
    return pl.pallas_call(
        functools.partial(_na_kernel, rows=rows),
        grid=(b, NA_HEADS // 2, nrb),
        in_specs=[pl.BlockSpec(memory_space=pltpu.SMEM),
                  pl.BlockSpec((1, tq, LANES), lambda i, hp, rb: (i, rb, hp)),
                  pl.BlockSpec((1, t, LANES), lambda i, hp, rb: (i, 0, hp)),
                  pl.BlockSpec((1, t, LANES), lambda i, hp, rb: (i, 0, hp)),
                  pl.BlockSpec((1, c, LANES), lambda i, hp, rb: (i, 0, hp)),
                  pl.BlockSpec((1, c, LANES), lambda i, hp, rb: (i, 0, hp)),
                  pl.BlockSpec((npat, 2, NA_QSUB, nk), lambda i, hp, rb: (0, hp, 0, 0))],
        out_specs=pl.BlockSpec((1, tq, LANES), lambda i, hp, rb: (i, rb, hp)),
        out_shape=jax.ShapeDtypeStruct((b, t, NA_WIDTH), BF16),
        compiler_params=pltpu.CompilerParams(vmem_limit_bytes=VMEM_LIMIT),
        name="na_attn",
    )(jnp.asarray(pid), qa, ka, va, ka_c, va_c, bias)


def _diff_kernel(lam_ref, q_ref, k_ref, v_ref, kc_ref, vc_ref, g_ref, o_ref, *, tk):
    q = q_ref[0]
    tq = q.shape[0]
    t = k_ref.shape[1]
    lane = lax.broadcasted_iota(I32, (1, LANES), 1)
    zero = jnp.zeros_like(q)
    q2 = jnp.concatenate([jnp.where(lane < DIFF_QK, q, zero), jnp.where(lane >= DIFF_QK, q, zero)], axis=0)

    def update(kb, vb, carry):
        m, a = carry
        s = _dot_nt(q2, kb)
        mn = jnp.maximum(m, jnp.max(s, axis=-1, keepdims=True))
        p = jnp.exp2(s - mn)
        return mn, jnp.exp2(m - mn) * a + jnp.dot(p.astype(BF16), vb, preferred_element_type=F32)

    init = (jnp.full((2 * tq, 1), NEG, F32), jnp.zeros((2 * tq, 2 * DIFF_V), F32))
    carry = update(kc_ref[0], vc_ref[0], init)
    for i in range(t // tk):
        carry = update(k_ref[0, i * tk:(i + 1) * tk, :], v_ref[0, i * tk:(i + 1) * tk, :], carry)
    lp = lam_ref[...]
    lam = (jnp.exp(jnp.sum(lp[0:1] * lp[1:2], axis=-1, keepdims=True))
           - jnp.exp(jnp.sum(lp[2:3] * lp[3:4], axis=-1, keepdims=True)) + LAMBDA_INIT)
    a = carry[1]
    a = a[:, :DIFF_V] / a[:, DIFF_V:DIFF_V + 1]
    out = a[:tq] - lam * a[tq:]
    o_ref[0] = (_rms(out, g_ref[...]) * (1.0 - LAMBDA_INIT)).astype(BF16)


def _diff(lam_params, qd, kd, vd, kd_c, vd_c, subln, *, tq, tk):
    b, t, w = qd.shape
    c = kd_c.shape[1]
    return pl.pallas_call(
        functools.partial(_diff_kernel, tk=tk),
        grid=(b, DIFF_HEADS, t // tq),
        in_specs=[pl.BlockSpec((4, DIFF_QK), lambda i, h, j: (0, 0)),
                  pl.BlockSpec((1, tq, LANES), lambda i, h, j: (i, j, h)),
                  pl.BlockSpec((1, t, LANES), lambda i, h, j: (i, 0, h)),
                  pl.BlockSpec((1, t, 2 * DIFF_V), lambda i, h, j: (i, 0, h)),
                  pl.BlockSpec((1, c, LANES), lambda i, h, j: (i, 0, h)),
                  pl.BlockSpec((1, c, 2 * DIFF_V), lambda i, h, j: (i, 0, h)),
                  pl.BlockSpec((1, DIFF_V), lambda i, h, j: (0, 0))],
        out_specs=pl.BlockSpec((1, tq, LANES), lambda i, h, j: (i, j, h)),
        out_shape=jax.ShapeDtypeStruct((b, t, w), BF16),
        compiler_params=pltpu.CompilerParams(vmem_limit_bytes=VMEM_LIMIT),
        name="diff_attn",
    )(lam_params, qd, kd, vd, kd_c, vd_c, subln)


def _out_kernel(oa_ref, od_ref, x_ref, gt1_ref, sh2_ref, sc2_ref, g1_ref, g2_ref,
                wout_ref, wq_ref, keys_ref, x1_ref, h2_ref, st_ref):
    wa = NA_WIDTH
    y = (jnp.dot(oa_ref[0], wout_ref[0:wa, :], preferred_element_type=F32)
         + jnp.dot(od_ref[0], wout_ref[wa:, :], preferred_element_type=F32))
    x1 = x_ref[0] + gt1_ref[0] * _rms(y, g1_ref[...])
    x1_ref[0] = x1
    h2 = _rms(x1, g2_ref[...]) * (1.0 + sc2_ref[0]) + sh2_ref[0]
    h2_ref[...] = h2.reshape(h2_ref.shape)
    qb = jnp.dot(h2.astype(BF16), wq_ref[...], preferred_element_type=F32).astype(BF16)
    for hp in range(2 * PEER_HEADS):
        st_ref[0, hp] = _dot_nt(keys_ref[hp], qb[:, hp * LANES:(hp + 1) * LANES])


def _out(oa, od, x, gt1, sh2, sc2, g1, g2, wout, wq, keys, *, tm):
    b, t, d = x.shape
    nq = wq.shape[1]
    nhp = keys.shape[0]
    mod = pl.BlockSpec((1, 1, d), lambda i, j: (i, 0, 0))
    gsp = pl.BlockSpec((1, d), lambda i, j: (0, 0))
    tok = pl.BlockSpec((1, tm, d), lambda i, j: (i, j, 0))
    half = pl.BlockSpec((1, tm, NA_WIDTH), lambda i, j: (i, j, 0))
    return pl.pallas_call(
        _out_kernel,
        grid=(b, t // tm),
        in_specs=[half, half, tok, mod, mod, mod, gsp, gsp,
                  pl.BlockSpec((d, d), lambda i, j: (0, 0)),
                  pl.BlockSpec((d, nq), lambda i, j: (0, 0)),
                  pl.BlockSpec((nhp, PEER_NKEYS, LANES), lambda i, j: (0, 0, 0))],
        out_specs=[tok, pl.BlockSpec((tm, d // LANES, LANES), lambda i, j: (i * (t // tm) + j, 0, 0)),
                   pl.BlockSpec((1, nhp, PEER_NKEYS, tm), lambda i, j: (i, 0, 0, j))],
        out_shape=[jax.ShapeDtypeStruct((b, t, d), F32), jax.ShapeDtypeStruct((b * t, d // LANES, LANES), F32),
                   jax.ShapeDtypeStruct((b, nhp, PEER_NKEYS, t), F32)],
        compiler_params=pltpu.CompilerParams(vmem_limit_bytes=VMEM_LIMIT),
        name="out_proj_peer_scores",
    )(oa, od, x, gt1, sh2, sc2, g1, g2, wout, wq, keys)


def _top1(s, iota):
    m = jnp.max(s, axis=0, keepdims=True)
    idx = jnp.min(jnp.where(s == m, iota, s.shape[0]), axis=0, keepdims=True)
    return m, idx, jnp.where(iota == idx, -jnp.inf, s)


def _topk_kernel(st_ref, idx_ref, gate_ref, stop_ref, itop_ref, best_ref, expert_ref):
    tl = st_ref.shape[-1]
    k = PEER_TOPK
    nlist = 2 * PEER_HEADS
    iota_n = lax.broadcasted_iota(I32, (PEER_NKEYS, tl), 0)

    row8 = lax.broadcasted_iota(I32, (8, PEER_NKEYS), 0)
    col8 = lax.broadcasted_iota(I32, (8, PEER_NKEYS), 1)
    probe = jnp.where(row8 == 0, col8, jnp.where(row8 == 1, 1, 0)).astype(BF16)
    unique = jnp.ones((1, tl), F32)
    for li in range(nlist):
        s = st_ref[0, li]
        vals, sums = [], []
        for _ in range(k):
            m = jnp.max(s, axis=0, keepdims=True)
            eq = s == m
            sums.append(jnp.dot(probe, jnp.where(eq, 1.0, 0.0).astype(BF16), preferred_element_type=F32)[0:2])
            s = jnp.where(eq, -jnp.inf, s)
            vals.append(m)
        stop_ref[li] = jnp.concatenate(vals, axis=0)
        itop_ref[li] = jnp.concatenate([x[0:1] for x in sums], axis=0).astype(I32)
        cnt = jnp.concatenate([x[1:2] for x in sums], axis=0)
        unique = jnp.minimum(unique, jnp.min(jnp.where(cnt == 1.0, 1.0, 0.0), axis=0, keepdims=True))

    @pl.when(jnp.min(unique) < 1.0)
    def _():
        for li in range(nlist):
            s = st_ref[0, li]
            vals, idxs = [], []
            for _ in range(k):
                m, i, s = _top1(s, iota_n)
                vals.append(m)
                idxs.append(i)
            stop_ref[li] = jnp.concatenate(vals, axis=0)
            itop_ref[li] = jnp.concatenate(idxs, axis=0)

    widths = [k // (a + 1) for a in range(k)]
    n_cand = -(-sum(widths) // 8) * 8
    iota_c = lax.broadcasted_iota(I32, (n_cand, tl), 0)
    pad = n_cand - sum(widths)

    def candidates(h):
        s0, i0 = stop_ref[2 * h], itop_ref[2 * h]
        s1, i1 = stop_ref[2 * h + 1], itop_ref[2 * h + 1]
        cand = jnp.concatenate([s0[a:a + 1] + s1[0:widths[a]] for a in range(k)]
                               + [jnp.full((pad, tl), -jnp.inf, F32)], axis=0)
        cidx = jnp.concatenate([i0[a:a + 1] * PEER_NKEYS + i1[0:widths[a]] for a in range(k)]
                               + [jnp.zeros((pad, tl), I32)], axis=0)
        return cand, cidx

    unique = jnp.ones((1, tl), F32)
    for h in range(PEER_HEADS):
        cand, cidx = candidates(h)
        best, experts = [], []
        for _ in range(k):
            m = jnp.max(cand, axis=0, keepdims=True)
            eq = cand == m
            experts.append(jnp.sum(jnp.where(eq, cidx, 0), axis=0, keepdims=True))
            cand = jnp.where(eq, -jnp.inf, cand)
            best.append(m)
        removed = jnp.sum(jnp.where(cand == -jnp.inf, 1.0, 0.0), axis=0, keepdims=True)
        unique = jnp.minimum(unique, jnp.where(removed == float(k + pad), 1.0, 0.0))
        best_ref[h] = jnp.concatenate(best, axis=0)
        expert_ref[h] = jnp.concatenate(experts, axis=0)

    @pl.when(jnp.min(unique) < 1.0)
    def _():
        for h in range(PEER_HEADS):
            cand, cidx = candidates(h)
            best, experts = [], []
            for _ in range(k):
                m, pos, cand = _top1(cand, iota_c)
                best.append(m)
                experts.append(jnp.sum(jnp.where(iota_c == pos, cidx, 0), axis=0, keepdims=True))
            best_ref[h] = jnp.concatenate(best, axis=0)
            expert_ref[h] = jnp.concatenate(experts, axis=0)

    gates = []
    for h in range(PEER_HEADS):
        best = best_ref[h]
        e = jnp.exp(best - best[0:1])
        gates.append(e / jnp.sum(e, axis=0, keepdims=True))
    nj = PEER_NJ
    idx_ref[0] = (expert_ref[...].reshape(nj, tl) * 4).T
    dup = (lax.broadcasted_iota(I32, (nj, 2 * nj), 1) // 2 == lax.broadcasted_iota(I32, (nj, 2 * nj), 0))
    gate_ref[0] = jnp.dot(jnp.concatenate(gates, axis=0).T, dup.astype(F32),
                          precision=lax.Precision.HIGHEST, preferred_element_type=F32)


def _topk(st, *, tl):
    b, nhp, nkeys, t = st.shape
    nj = PEER_NJ
    return pl.pallas_call(
        _topk_kernel,
        grid=(b, t // tl),
        in_specs=[pl.BlockSpec((1, nhp, nkeys, tl), lambda i, j: (i, 0, 0, j))],
        out_specs=[pl.BlockSpec((1, tl, nj), lambda i, j: (i, j, 0)),
                   pl.BlockSpec((1, tl, 2 * nj), lambda i, j: (i, j, 0))],
        out_shape=[jax.ShapeDtypeStruct((b, t, nj), I32), jax.ShapeDtypeStruct((b, t, 2 * nj), F32)],
        scratch_shapes=[pltpu.VMEM((nhp, PEER_TOPK, tl), F32), pltpu.VMEM((nhp, PEER_TOPK, tl), I32),
                        pltpu.VMEM((PEER_HEADS, PEER_TOPK, tl), F32), pltpu.VMEM((PEER_HEADS, PEER_TOPK, tl), I32)],
        compiler_params=pltpu.CompilerParams(vmem_limit_bytes=VMEM_LIMIT),
        name="peer_topk",
    )(st)


def _pack_kernel(t_ref, o_ref):
    half = t_ref.shape[1] // 2
    lo = pltpu.bitcast(t_ref[:, :half].astype(BF16).astype(F32), I32)
    hi = pltpu.bitcast(t_ref[:, half:].astype(BF16).astype(F32), I32)
    words = (hi & jnp.int32(-65536)) | lax.shift_right_logical(lo, 16)
    o_ref[...] = words.reshape(o_ref.shape)


def _pack_table(tab):
    e, d = tab.shape
    rows = d // 2 // LANES
    te = 512
    return pl.pallas_call(
        _pack_kernel,
        grid=(e // te,),
        in_specs=[pl.BlockSpec((te, d), lambda i: (i, 0))],
        out_specs=pl.BlockSpec((te * rows, LANES), lambda i: (i, 0)),
        out_shape=jax.ShapeDtypeStruct((e * rows, LANES), I32),
        compiler_params=pltpu.CompilerParams(vmem_limit_bytes=VMEM_LIMIT),
        name="pack_table",
    )(tab)


def _gather_rows(idx_ref, trow, table_ref, tile_ref):
    half = PEER_NJ // 2
    ahead = 2
    rows = {}
    for j in range(ahead):
        rows[j], rows[j + half] = idx_ref[trow, j], idx_ref[trow, j + half]
    for j in range(half):
        if j + ahead < half:
            rows[j + ahead], rows[j + ahead + half] = idx_ref[trow, j + ahead], idx_ref[trow, j + ahead + half]
        a = table_ref[pl.ds(pl.multiple_of(rows[j], 4), 4), :]
        b = table_ref[pl.ds(pl.multiple_of(rows[j + half], 4), 4), :]
        tile_ref[pl.ds(j, 8, stride=TILE_STRIDE), :] = jnp.concatenate([a, b], axis=0)


def _tile_chunk(tile_ref, c):
    half = PEER_NJ // 2
    lo, hi = c * TILE_STRIDE, (c + 4) * TILE_STRIDE
    return pltpu.bitcast(jnp.concatenate([tile_ref[lo:lo + half, :], tile_ref[hi:hi + half, :]], axis=0), BF16)


def _split_bf16(v):
    head = v.astype(BF16).astype(F32)
    return jnp.concatenate([head, v - head], axis=0)


def _idx_copy(idx_hbm, first_token, buf, sem):
    n = buf.shape[0]
    src = idx_hbm.at[pl.ds(pl.multiple_of(first_token, n), n), :]
    return pltpu.make_async_copy(src, buf, sem)


def _for_each_group(idx_hbm, ibufs, sems, group):
    step = pl.program_id(0)
    half = PEER_TM // 2
    nu = PEER_UNROLL

    @pl.when(step == 0)
    def _():
        for h in range(2):
            _idx_copy(idx_hbm, h * half, ibufs[h], sems.at[h]).start()

    for h in range(2):
        tok0 = step * PEER_TM + h * half
        _idx_copy(idx_hbm, tok0, ibufs[h], sems.at[h]).wait()
        for g in range(half // nu):
            group(ibufs[h], g * nu, h * half + g * nu)

        @pl.when(step + 1 < pl.num_programs(0))
        def _():
            _idx_copy(idx_hbm, tok0 + PEER_TM, ibufs[h], sems.at[h]).start()


def _peer_scratch():
    return ([pltpu.VMEM((8 * TILE_STRIDE + 8, LANES), I32) for _ in range(PEER_UNROLL)]
            + [pltpu.SMEM((PEER_TM // 2, PEER_NJ), I32) for _ in range(2)]
            + [pltpu.SemaphoreType.DMA((2,))])


def _peer_u_kernel(idx_hbm, x_ref, gate_ref, u_ref, w_ref, *scratch):
    nj = PEER_NJ
    nu = PEER_UNROLL
    tile_refs, ibufs, sems = scratch[:nu], scratch[nu:nu + 2], scratch[nu + 2]
    row16 = lax.broadcasted_iota(I32, (16, 1), 0) % 8
    even = (lax.broadcasted_iota(I32, (1, 2 * nj), 1) % 2) == 0

    def token(ibuf, off, t, tile_ref):
        _gather_rows(ibuf, off, u_ref, tile_ref)
        x16 = _split_bf16(x_ref[t])
        acc = jnp.zeros((16, 2 * nj), F32)
        for c in range(4):
            lhs = jnp.where((row16 == c) | (row16 == 4 + c), x16, 0.0).astype(BF16)
            acc = acc + _dot_nt(lhs, _tile_chunk(tile_ref, c))
        a = acc[0:8] + acc[8:16]
        return jnp.sum(jnp.where(even, a, pltpu.roll(a, 4, axis=0))[0:4], axis=0, keepdims=True)

    def group(ibuf, off, t0):
        z = jnp.concatenate([token(ibuf, off + i, t0 + i, tile_refs[i]) for i in range(nu)], axis=0)
        act = z + jnp.where(even, pltpu.roll(z, 2 * nj - 1, axis=1), pltpu.roll(z, 1, axis=1))
        gelu = 0.5 * act * (1.0 + lax.erf(act * (2.0 ** -0.5)))
        w_ref[t0:t0 + nu, :] = gate_ref[t0:t0 + nu, :] * gelu

    _for_each_group(idx_hbm, ibufs, sems, group)


def _peer_u(idx4, x3, gate2, table):
    n = x3.shape[0]
    nj = PEER_NJ
    tm = PEER_TM
    return pl.pallas_call(
        _peer_u_kernel,
        grid=(n // tm,),
        in_specs=[pl.BlockSpec(memory_space=pl.ANY),
                  pl.BlockSpec((tm, 8, LANES), lambda i: (i, 0, 0)),
                  pl.BlockSpec((tm, 2 * nj), lambda i: (i, 0)),
                  pl.BlockSpec(memory_space=pltpu.VMEM)],
        out_specs=pl.BlockSpec((tm, 2 * nj), lambda i: (i, 0)),
        out_shape=jax.ShapeDtypeStruct((n, 2 * nj), F32),
        scratch_shapes=_peer_scratch(),
        compiler_params=pltpu.CompilerParams(vmem_limit_bytes=VMEM_LIMIT, dimension_semantics=("arbitrary",)),
        name="peer_u",
    )(idx4, x3, gate2, table)


def _peer_v_kernel(idx_hbm, w_ref, v_ref, o_ref, *scratch):
    nj = PEER_NJ
    nu = PEER_UNROLL
    tile_refs, ibufs, sems = scratch[:nu], scratch[nu:nu + 2], scratch[nu + 2]
    row16 = lax.broadcasted_iota(I32, (16, 1), 0) % 8
    even = (lax.broadcasted_iota(I32, (1, 2 * nj), 1) % 2) == 0

    def token(ibuf, off, t, tile_ref):
        _gather_rows(ibuf, off, v_ref, tile_ref)
        w16 = _split_bf16(jnp.broadcast_to(w_ref[t:t + 1, :], (8, 2 * nj)))
        acc = jnp.zeros((16, LANES), F32)
        for c in range(4):
            keep = ((row16 == c) & even) | ((row16 == 4 + c) & jnp.logical_not(even))
            lhs = jnp.where(keep, w16, 0.0).astype(BF16)
            acc = acc + jnp.dot(lhs, _tile_chunk(tile_ref, c), preferred_element_type=F32)
        o_ref[t] = acc[0:8] + acc[8:16]

    def group(ibuf, off, t0):
        for i in range(nu):
            token(ibuf, off + i, t0 + i, tile_refs[i])

    _for_each_group(idx_hbm, ibufs, sems, group)


def _peer_v(idx4, w2, table):
    n = w2.shape[0]
    nj = PEER_NJ
    tm = PEER_TM
    return pl.pallas_call(
        _peer_v_kernel,
        grid=(n // tm,),
        in_specs=[pl.BlockSpec(memory_space=pl.ANY),
                  pl.BlockSpec((tm, 2 * nj), lambda i: (i, 0)),
                  pl.BlockSpec(memory_space=pltpu.VMEM)],
        out_specs=pl.BlockSpec((tm, 8, LANES), lambda i: (i, 0, 0)),
        out_shape=jax.ShapeDtypeStruct((n, 8, LANES), F32),
        scratch_shapes=_peer_scratch(),
        compiler_params=pltpu.CompilerParams(vmem_limit_bytes=VMEM_LIMIT, dimension_semantics=("arbitrary",)),
        name="peer_v",
    )(idx4, w2, table)


def _final_kernel(x1_ref, y_ref, gt2_ref, g_ref, o_ref):
    y = y_ref[...].reshape(x1_ref.shape[1:])
    o_ref[0] = x1_ref[0] + gt2_ref[0] * _rms(y, g_ref[...])


def _final(x1, y3, gt2, g, *, tm):
    b, t, d = x1.shape
    nt = t // tm
    tok = pl.BlockSpec((1, tm, d), lambda i, j: (i, j, 0))
    return pl.pallas_call(
        _final_kernel,
        grid=(b, nt),
        in_specs=[tok, pl.BlockSpec((tm, d // LANES, LANES), lambda i, j: (i * nt + j, 0, 0)),
                  pl.BlockSpec((1, 1, d), lambda i, j: (i, 0, 0)),
                  pl.BlockSpec((1, d), lambda i, j: (0, 0))],
        out_specs=tok,
        out_shape=jax.ShapeDtypeStruct((b, t, d), F32),
        compiler_params=pltpu.CompilerParams(vmem_limit_bytes=VMEM_LIMIT),
        name="final_residual",
    )(x1, y3, gt2, g)


def _rope_tables(t):
    tok = jnp.arange(t)
    row = (tok // GRID_W).astype(F32)
    col = (tok % GRID_W).astype(F32)
    n_freq = DIFF_QK // 4
    inv = ROPE_THETA ** (-jnp.arange(n_freq, dtype=F32) / n_freq)
    ang = jnp.concatenate([row[:, None] * inv, col[:, None] * inv], axis=-1)
    ang2 = jnp.tile(jnp.repeat(ang, 2, axis=1), (1, LANES // DIFF_QK))
    sign = jnp.where(jnp.arange(LANES) % 2 == 0, -1.0, 1.0).astype(F32)
    return jnp.cos(ang2), jnp.sin(ang2) * sign


def _head_major_cols(w):
    d = w.shape[0]
    return w.reshape(d, 2, DIFF_HEADS, DIFF_QK).transpose(0, 2, 1, 3).reshape(d, -1)


def kernel(x, c, ctx, c_ctx, w_ada, b_ada, g_norm, w_in, na_rel_bias, diff_lambda, diff_subln,
           w_out, peer_wq, peer_keys, peer_u, peer_v):
    assert w_ada.shape[0] == 1, "single-layer kernel"
    b, t, d = x.shape
    n_ctx = ctx.shape[1]
    rows = t // GRID_W
    assert t % (NA_QROWS * GRID_W) == 0 and rows >= NA_KROWS

    pad = (-(b + 1)) % 8
    cc = jnp.concatenate([c, c_ctx[None], jnp.zeros((pad, d), F32)], axis=0)
    mod = _ada(cc, w_ada[0], b_ada[0][None]).reshape(b + 1 + pad, N_MOD, d)
    lat = lambda i: mod[:b, i][:, None, :]
    cxm = lambda i: mod[b:b + 1, i][:, None, :]

    gw = NA_WIDTH
    w_in0 = w_in[0]
    w_perm = jnp.concatenate([w_in0[:, :3 * gw], _head_major_cols(w_in0[:, 3 * gw:4 * gw]),
                              _head_major_cols(w_in0[:, 4 * gw:5 * gw]), w_in0[:, 5 * gw:]], axis=1).astype(BF16)
    cos2, sin2 = _rope_tables(t)
    g0 = g_norm[0, 0][None]
    qa, ka, va, qd, kd, vd = _qkv(x, lat(0), lat(1), g0, w_perm, cos2, sin2,
                                  rope=True, per_batch_mod=True, tm=512)
    _, ka_c, va_c, _, kd_c, vd_c = _qkv(ctx, cxm(0), cxm(1), g0, w_perm, cos2[:n_ctx], sin2[:n_ctx],
                                        rope=False, per_batch_mod=False, tm=n_ctx)

    out_a = _na(qa, ka, va, ka_c, va_c, *_na_bias_table(na_rel_bias[0], rows))
    out_d = _diff(diff_lambda[0], qd, kd, vd, kd_c, vd_c, diff_subln[0][None], tq=DIFF_TQ, tk=DIFF_TK)

    keys = peer_keys[0].reshape(2 * PEER_HEADS, PEER_NKEYS, -1).astype(BF16)
    x1, h2, st = _out(out_a, out_d, x, lat(2), lat(3), lat(4), g_norm[0, 1][None], g_norm[0, 2][None],
                      w_out[0].astype(BF16), peer_wq[0].astype(BF16), keys, tm=512)

    idx4, gate2 = _topk(st, tl=256)
    idx4 = idx4.reshape(b * t, PEER_NJ)
    gate2 = gate2.reshape(b * t, 2 * PEER_NJ)

    w2 = _peer_u(idx4, h2, gate2, _pack_table(peer_u[0]))
    y3 = _peer_v(idx4, w2, _pack_table(peer_v[0]))
    return _final(x1, y3, lat(5), g_norm[0, 3][None], tm=512)
```
